```python
import jax, jax.numpy as jnp
from jax import lax
import numpy as np

D_MODEL = 1024
BATCH = 8
SEQ = 2048
DEPTH = 1
DEC_BATCH = 32
DEC_SEQ = 64
PAST_LEN = 4096

CHUNK = 64
HEAD_DIM = 64
SBA_HEADS = 8
SBA_W = SBA_HEADS * HEAD_DIM
CONV_W = 256
CONV_K = 31
MEM_HEADS = 4
MEM_W = MEM_HEADS * HEAD_DIM
N_MEM = 256
MIX_W = SBA_W + CONV_W + MEM_W
IN_W = 3 * SBA_W + 2 * CONV_W + MEM_W
QBLOCK = 128
N_GROUPS = 4
EXPERTS_PER_GROUP = 8
N_EXPERTS = N_GROUPS * EXPERTS_PER_GROUP
TOP_K = 2
D_EXPERT = 256
DEEPNORM_ALPHA = (2 * DEPTH) ** 0.25
DEEPNORM_BETA = (8 * DEPTH) ** -0.25
SBA_SCALE = HEAD_DIM ** -0.5
MEM_SCALE = HEAD_DIM ** -0.5
LN_EPS = 1e-5

kernel_name = 'stickbreak_conformer_memxattn_hmoe_stream_step'


def _ln(x, g, b):
    xf = x.astype(jnp.float32)
    mu = jnp.mean(xf, -1, keepdims=True)
    var = jnp.mean(jnp.square(xf - mu), -1, keepdims=True)
    return ((xf - mu) * lax.rsqrt(var + LN_EPS) * g.astype(jnp.float32) + b.astype(jnp.float32)).astype(x.dtype)


def _in_proj(h, w_in):
    B, T, _ = h.shape
    p = jnp.einsum('btd,de->bte', h, w_in)
    q, k, v, cv, cg, mq = jnp.split(p, [SBA_W, 2 * SBA_W, 3 * SBA_W, 3 * SBA_W + CONV_W, 3 * SBA_W + 2 * CONV_W], axis=-1)
    hd = (B, T, SBA_HEADS, HEAD_DIM)
    glu = cv * jax.nn.sigmoid(cg)
    return q.reshape(hd), k.reshape(hd), v.reshape(hd), glu, mq.reshape(B, T, MEM_HEADS, HEAD_DIM)


def _stick_breaking(q, q_pos, k, v, k_pos):
    z = jnp.einsum('bqhd,bshd->bhqs', q.astype(jnp.float32), k.astype(jnp.float32)) * SBA_SCALE
    causal = k_pos[None, :] < q_pos[:, None]
    log_keep = jnp.where(causal, jax.nn.log_sigmoid(-z), 0.0)
    suffix = lax.cumsum(log_keep, axis=3, reverse=True) - log_keep
    w = jnp.where(causal, jnp.exp(jax.nn.log_sigmoid(z) + suffix), 0.0)
    o = jnp.einsum('bhqs,bshd->bqhd', w, v.astype(jnp.float32))
    return o.astype(q.dtype)


def _sba_prompt(q, k, v):
    B, T, H, Dh = q.shape
    nb = T // QBLOCK
    k_pos = jnp.arange(T)
    qb = jnp.moveaxis(q.reshape(B, nb, QBLOCK, H, Dh), 1, 0)

    def block(args):
        qi, bi = args
        return _stick_breaking(qi, bi * QBLOCK + jnp.arange(QBLOCK), k, v, k_pos)

    o = lax.map(block, (qb, jnp.arange(nb)))
    return jnp.moveaxis(o, 0, 1).reshape(B, T, H, Dh)


def _conv_module(conv_in, w_dw, b_dw, lnc_g, lnc_b, w_cpw):
    dw = lax.conv_general_dilated(conv_in, w_dw[:, None, :].astype(conv_in.dtype), window_strides=(1,),
                                  padding='VALID', dimension_numbers=('NWC', 'WIO', 'NWC'),
                                  feature_group_count=CONV_W) + b_dw
    u = _ln(dw, lnc_g, lnc_b)
    return jnp.einsum('btc,ce->bte', jax.nn.silu(u), w_cpw)


def _mem_kv(mem, w):
    B, M, _ = mem.shape
    return jnp.einsum('bmd,de->bme', mem, w).reshape(B, M, MEM_HEADS, HEAD_DIM)


def _mem_attn(mq, mk, mv):
    s = jnp.einsum('bqhd,bmhd->bhqm', mq.astype(jnp.float32), mk.astype(jnp.float32)) * MEM_SCALE
    p = jax.nn.softmax(s, axis=-1)
    return jnp.einsum('bhqm,bmhd->bqhd', p, mv.astype(jnp.float32)).astype(mq.dtype)


def _hier_moe(h, w_rg, b_rg, w_re, b_re, w_eg, w_eu, w_ed):
    B, T, D = h.shape
    t = h.reshape(B * T, D)
    g_logits = jnp.einsum('nd,dg->ng', t, w_rg).astype(jnp.float32) + b_rg.astype(jnp.float32)
    g_prob = jax.nn.softmax(g_logits, axis=-1)
    g_idx = jnp.argmax(g_logits, axis=-1)
    g_w = jnp.max(g_prob, axis=-1, keepdims=True)
    e_logits = (jnp.einsum('nd,de->ne', t, w_re).astype(jnp.float32) + b_re.astype(jnp.float32))
    e_logits = e_logits.reshape(-1, N_GROUPS, EXPERTS_PER_GROUP)
    e_sel = jnp.einsum('nge,ng->ne', e_logits, jax.nn.one_hot(g_idx, N_GROUPS, dtype=jnp.float32))
    top_v, top_i = lax.top_k(e_sel, TOP_K)
    top_w = jax.nn.softmax(top_v, axis=-1) * g_w
    expert_id = g_idx[:, None] * EXPERTS_PER_GROUP + top_i
    gate = jnp.einsum('nke,nk->ne', jax.nn.one_hot(expert_id, N_EXPERTS, dtype=jnp.float32), top_w)
    a = jnp.einsum('nd,edf->nef', t, w_eg)
    u = jnp.einsum('nd,edf->nef', t, w_eu)
    hid = jax.nn.silu(a) * u * gate.astype(t.dtype)[:, :, None]
    y = jnp.einsum('nef,efd->nd', hid, w_ed)
    return y.reshape(B, T, D)


def _post(x, sba, conv, mem, w_out, ln1_g, ln1_b, w_rg, b_rg, w_re, b_re, w_eg, w_eu, w_ed, ln2_g, ln2_b):
    B, T, _ = x.shape
    mix = jnp.concatenate([sba.reshape(B, T, SBA_W), conv, mem.reshape(B, T, MEM_W)], axis=-1)
    x = _ln(DEEPNORM_ALPHA * x + jnp.einsum('bte,ed->btd', mix, w_out), ln1_g, ln1_b)
    f = _hier_moe(x, w_rg, b_rg, w_re, b_re, w_eg, w_eu, w_ed)
    return _ln(DEEPNORM_ALPHA * x + f, ln2_g, ln2_b)


def setup_inputs(seed: int = 0) -> dict:
    key = jax.random.key(seed)
    ks = jax.random.split(key, 40)
    L = DEPTH

    def nrm(k, shape, scale=1.0):
        return jax.random.normal(k, shape, jnp.float32) * scale

    return {
        'x_prompt': nrm(ks[0], (BATCH, SEQ, D_MODEL)),
        'x_sample': nrm(ks[1], (DEC_BATCH, DEC_SEQ, D_MODEL)),
        'mem_prompt': nrm(ks[2], (BATCH, N_MEM, D_MODEL)),
        'cache_sba_k': nrm(ks[3], (L, DEC_BATCH, PAST_LEN, SBA_HEADS, HEAD_DIM)),
        'cache_sba_v': nrm(ks[4], (L, DEC_BATCH, PAST_LEN, SBA_HEADS, HEAD_DIM)),
        'cache_conv': nrm(ks[5], (L, DEC_BATCH, CONV_K - 1, CONV_W), 0.5),
        'cache_mem_k': nrm(ks[6], (L, DEC_BATCH, N_MEM, MEM_HEADS, HEAD_DIM)),
        'cache_mem_v': nrm(ks[7], (L, DEC_BATCH, N_MEM, MEM_HEADS, HEAD_DIM)),
        'ln0_g': 1.0 + nrm(ks[8], (D_MODEL,), 0.02),
        'ln0_b': nrm(ks[9], (D_MODEL,), 0.02),
        'w_in': nrm(ks[10], (L, D_MODEL, IN_W), D_MODEL ** -0.5),
        'w_dw': nrm(ks[11], (L, CONV_K, CONV_W), CONV_K ** -0.5),
        'b_dw': nrm(ks[12], (L, CONV_W), 0.02),
        'lnc_g': 1.0 + nrm(ks[13], (L, CONV_W), 0.02),
        'lnc_b': nrm(ks[14], (L, CONV_W), 0.02),
        'w_cpw': nrm(ks[15], (L, CONV_W, CONV_W), CONV_W ** -0.5),
        'w_mk': nrm(ks[16], (L, D_MODEL, MEM_W), D_MODEL ** -0.5),
        'w_mv': nrm(ks[17], (L, D_MODEL, MEM_W), D_MODEL ** -0.5),
        'w_out': nrm(ks[18], (L, MIX_W, D_MODEL), MIX_W ** -0.5 * DEEPNORM_BETA),
        'ln1_g': 1.0 + nrm(ks[19], (L, D_MODEL), 0.02),
        'ln1_b': nrm(ks[20], (L, D_MODEL), 0.02),
        'w_rg': nrm(ks[21], (L, D_MODEL, N_GROUPS), D_MODEL ** -0.5),
        'b_rg': nrm(ks[22], (L, N_GROUPS), 0.01),
        'w_re': nrm(ks[23], (L, D_MODEL, N_EXPERTS), D_MODEL ** -0.5),
        'b_re': nrm(ks[24], (L, N_EXPERTS), 0.01),
        'w_eg': nrm(ks[25], (L, N_EXPERTS, D_MODEL, D_EXPERT), D_MODEL ** -0.5),
        'w_eu': nrm(ks[26], (L, N_EXPERTS, D_MODEL, D_EXPERT), D_MODEL ** -0.5),
        'w_ed': nrm(ks[27], (L, N_EXPERTS, D_EXPERT, D_MODEL), D_EXPERT ** -0.5 * DEEPNORM_BETA),
        'ln2_g': 1.0 + nrm(ks[28], (L, D_MODEL), 0.02),
        'ln2_b': nrm(ks[29], (L, D_MODEL), 0.02),
    }


def reference(x_prompt, x_sample, mem_prompt, cache_sba_k, cache_sba_v, cache_conv, cache_mem_k, cache_mem_v,
              ln0_g, ln0_b, w_in, w_dw, b_dw, lnc_g, lnc_b, w_cpw, w_mk, w_mv, w_out, ln1_g, ln1_b,
              w_rg, b_rg, w_re, b_re, w_eg, w_eu, w_ed, ln2_g, ln2_b):
    xp = _ln(x_prompt, ln0_g, ln0_b)
    xs = _ln(x_sample, ln0_g, ln0_b)
    past = cache_sba_k.shape[2]
    tn = x_sample.shape[1]
    q_pos_s = past + jnp.arange(tn)
    k_pos_s = jnp.arange(past + tn)
    kp_l, vp_l, cp_l, mkp_l, mvp_l, ks_l, vs_l, cs_l = [], [], [], [], [], [], [], []
    for l in range(DEPTH):
        moe = (w_out[l], ln1_g[l], ln1_b[l], w_rg[l], b_rg[l], w_re[l], b_re[l], w_eg[l], w_eu[l], w_ed[l], ln2_g[l], ln2_b[l])
        conv_w = (w_dw[l], b_dw[l], lnc_g[l], lnc_b[l], w_cpw[l])
        q, k, v, glu, mq = _in_proj(xp, w_in[l])
        sba = _sba_prompt(q, k, v)
        conv_in = jnp.pad(glu, ((0, 0), (CONV_K - 1, 0), (0, 0)))
        conv = _conv_module(conv_in, *conv_w)
        mk = _mem_kv(mem_prompt, w_mk[l])
        mv = _mem_kv(mem_prompt, w_mv[l])
        mem = _mem_attn(mq, mk, mv)
        xp = _post(xp, sba, conv, mem, *moe)
        kp_l.append(k); vp_l.append(v); cp_l.append(conv_in[:, -(CONV_K - 1):]); mkp_l.append(mk); mvp_l.append(mv)
        q, k, v, glu, mq = _in_proj(xs, w_in[l])
        k_all = jnp.concatenate([cache_sba_k[l], k], axis=1)
        v_all = jnp.concatenate([cache_sba_v[l], v], axis=1)
        sba = _stick_breaking(q, q_pos_s, k_all, v_all, k_pos_s)
        conv_in = jnp.concatenate([cache_conv[l], glu], axis=1)
        conv = _conv_module(conv_in, *conv_w)
        mem = _mem_attn(mq, cache_mem_k[l], cache_mem_v[l])
        xs = _post(xs, sba, conv, mem, *moe)
        ks_l.append(k); vs_l.append(v); cs_l.append(conv_in[:, -(CONV_K - 1):])
    return (xp, xs, jnp.stack(kp_l), jnp.stack(vp_l), jnp.stack(cp_l), jnp.stack(mkp_l), jnp.stack(mvp_l),
            jnp.stack(ks_l), jnp.stack(vs_l), jnp.stack(cs_l))
```

```python
import functools

import jax
import jax.numpy as jnp
from jax import lax
from jax.experimental import pallas as pl
from jax.experimental.pallas import tpu as pltpu

F32 = jnp.float32
BF16 = jnp.bfloat16

D_MODEL = 1024
HEAD_DIM = 64
SBA_HEADS = 8
SBA_W = SBA_HEADS * HEAD_DIM
CONV_W = 256
CONV_K = 31
MEM_HEADS = 4
MEM_W = MEM_HEADS * HEAD_DIM
N_MEM = 256
N_GROUPS = 4
EXPERTS_PER_GROUP = 8
N_EXPERTS = N_GROUPS * EXPERTS_PER_GROUP
D_EXPERT = 256
DEPTH = 1
DEEPNORM_ALPHA = (2 * DEPTH) ** 0.25
QK_SCALE = HEAD_DIM ** -0.5
LN_EPS = 1e-5

LANES = 128
HALO = 32
VMEM_LIMIT = 48 * 1024 * 1024


def _cparams(sem):
    return pltpu.CompilerParams(dimension_semantics=sem, vmem_limit_bytes=VMEM_LIMIT)


def _ln(x, g, b):
    mu = jnp.mean(x, axis=-1, keepdims=True)
    xc = x - mu
    var = jnp.mean(xc * xc, axis=-1, keepdims=True)
    return xc * lax.rsqrt(var + LN_EPS) * g + b


def _softplus(z):
    return jnp.maximum(z, 0.0) + jnp.log1p(jnp.exp(-jnp.abs(z)))


def _dot(a, b):
    return jnp.dot(a, b, preferred_element_type=F32)


def _dot_nt(a, b):
    return lax.dot_general(a, b, (((1,), (1,)), ((), ())), preferred_element_type=F32)


def _in_proj_kernel(x_ref, g_ref, b_ref, w_ref, q_ref, kb_ref, vb_ref, k_ref, v_ref, glu_ref, mq_ref):
    xn = _ln(x_ref[...], g_ref[...], b_ref[...]).astype(BF16)
    q = _dot(xn, w_ref[:, 0:SBA_W])
    q_ref[...] = (q * QK_SCALE).astype(BF16)
    k = _dot(xn, w_ref[:, SBA_W:2 * SBA_W])
    k_ref[...] = k
    kb_ref[...] = k.astype(BF16)
    v = _dot(xn, w_ref[:, 2 * SBA_W:3 * SBA_W])
    v_ref[...] = v
    vb_ref[...] = v.astype(BF16)
    c0 = 3 * SBA_W
    cv = _dot(xn, w_ref[:, c0:c0 + CONV_W])
    cg = _dot(xn, w_ref[:, c0 + CONV_W:c0 + 2 * CONV_W])
    glu_ref[...] = cv * jax.nn.sigmoid(cg)
    mq = _dot(xn, w_ref[:, c0 + 2 * CONV_W:c0 + 2 * CONV_W + MEM_W])
    mq_ref[...] = (mq * QK_SCALE).astype(BF16)


def _in_proj(x2d, g, b, w_bf16, tm):
    n = x2d.shape[0]
    in_w = w_bf16.shape[1]
    row = lambda w: pl.BlockSpec((tm, w), lambda i: (i, 0))
    full = lambda s: pl.BlockSpec(s, lambda i: (0, 0))
    return pl.pallas_call(
        _in_proj_kernel,
        grid=(n // tm,),
        in_specs=[row(D_MODEL), full((1, D_MODEL)), full((1, D_MODEL)), full((D_MODEL, in_w))],
        out_specs=[row(SBA_W), row(SBA_W), row(SBA_W), row(SBA_W), row(SBA_W), row(CONV_W), row(MEM_W)],
        out_shape=[
            jax.ShapeDtypeStruct((n, SBA_W), BF16),
            jax.ShapeDtypeStruct((n, SBA_W), BF16),
            jax.ShapeDtypeStruct((n, SBA_W), BF16),
            jax.ShapeDtypeStruct((n, SBA_W), F32),
            jax.ShapeDtypeStruct((n, SBA_W), F32),
            jax.ShapeDtypeStruct((n, CONV_W), F32),
            jax.ShapeDtypeStruct((n, MEM_W), BF16),
        ],
        compiler_params=_cparams(("parallel",)),
        name="in_proj",
    )(x2d, g, b, w_bf16)


def _sba_block(qm, kb, vb, u, carry, mask):
    c, acc = carry
    z = _dot_nt(qm, kb)
    lk = -_softplus(z)
    if mask is not None:
        lk = jnp.where(mask, lk, 0.0)
    hi = lk.astype(BF16)
    lo = (lk - hi.astype(F32)).astype(BF16)
    suffix = _dot(hi, u) + _dot(lo, u)
    w = jnp.exp(z + lk + suffix + c)
    if mask is not None:
        w = jnp.where(mask, w, 0.0)
    acc = acc + _dot(w.astype(BF16), vb)
    c = c + jnp.sum(lk, axis=1, keepdims=True)
    return c, acc


def _head_masks(shape):
    lane = lax.broadcasted_iota(jnp.int32, shape, 1)
    return lane < HEAD_DIM


def _sba_prompt_kernel(q_ref, k_ref, v_ref, u_ref, o_ref, *, tq):
    qi = pl.program_id(2)
    q = q_ref[...]
    low = _head_masks(q.shape)
    u = u_ref[...]
    row = lax.broadcasted_iota(jnp.int32, (tq, tq), 0)
    col = lax.broadcasted_iota(jnp.int32, (tq, tq), 1)
    diag_mask = col < row
    outs = []
    for h in range(2):
        qm = jnp.where(low if h == 0 else jnp.logical_not(low), q, jnp.zeros_like(q))

        def blk(j, carry, mask):
            start = pl.multiple_of(j * tq, tq)
            return _sba_block(qm, k_ref[pl.ds(start, tq), :], v_ref[pl.ds(start, tq), :], u, carry, mask)

        carry = (jnp.zeros((tq, 1), F32), jnp.zeros((tq, LANES), F32))
        carry = blk(qi, carry, diag_mask)
        carry = lax.fori_loop(0, qi, lambda t, cr: blk(qi - 1 - t, cr, None), carry)
        outs.append(carry[1])
    o_ref[...] = jnp.where(low, outs[0], outs[1]).astype(o_ref.dtype)


def _sba_prompt(q, k, v, u, batch, seq, tq):
    n = q.shape[0]
    nq = seq // tq
    pairs = SBA_W // LANES
    return pl.pallas_call(
        functools.partial(_sba_prompt_kernel, tq=tq),
        grid=(batch, pairs, nq),
        in_specs=[
            pl.BlockSpec((tq, LANES), lambda b, p, i: (b * nq + i, p)),
            pl.BlockSpec((seq, LANES), lambda b, p, i: (b, p)),
            pl.BlockSpec((seq, LANES), lambda b, p, i: (b, p)),
            pl.BlockSpec((tq, tq), lambda b, p, i: (0, 0)),
        ],
        out_specs=pl.BlockSpec((tq, LANES), lambda b, p, i: (b * nq + i, p)),
        out_shape=jax.ShapeDtypeStruct((n, SBA_W), BF16),
        compiler_params=_cparams(("parallel", "parallel", "arbitrary")),
        name="sba_prompt",
    )(q, k, v, u)


def _sba_sample_kernel(q_ref, kn_ref, vn_ref, kc_ref, vc_ref, u_ref, un_ref, o_ref, c_ref, acc_ref, *, tn, tk):
    j = pl.program_id(1)
    nj = pl.num_programs(1)
    q = q_ref[...]
    low = _head_masks((tn, LANES))
    pairs = SBA_W // LANES

    def run(get_kv, u, mask, first):
        for p in range(pairs):
            qp = q[:, p * LANES:(p + 1) * LANES]
            kb, vb = get_kv(p)
            for h in range(2):
                hd = 2 * p + h
                qm = jnp.where(low if h == 0 else jnp.logical_not(low), qp, jnp.zeros_like(qp))
                if first:
                    carry = (jnp.zeros((tn, 1), F32), jnp.zeros((tn, LANES), F32))
                else:
                    carry = (c_ref[hd][:, 0:1], acc_ref[hd])
                c, acc = _sba_block(qm, kb, vb, u, carry, mask)
                c_ref[hd] = jnp.broadcast_to(c, (tn, LANES))
                acc_ref[hd] = acc

    @pl.when(j == 0)
    def _():
        row = lax.broadcasted_iota(jnp.int32, (tn, tn), 0)
        col = lax.broadcasted_iota(jnp.int32, (tn, tn), 1)
        run(lambda p: (kn_ref[:, p * LANES:(p + 1) * LANES], vn_ref[:, p * LANES:(p + 1) * LANES]),
            un_ref[...], col < row, True)

    run(lambda p: (kc_ref[0, :, p * LANES:(p + 1) * LANES].astype(BF16),
                   vc_ref[0, :, p * LANES:(p + 1) * LANES].astype(BF16)),
        u_ref[...], None, False)

    @pl.when(j == nj - 1)
    def _():
        for p in range(pairs):
            o_ref[:, p * LANES:(p + 1) * LANES] = jnp.where(
                low, acc_ref[2 * p], acc_ref[2 * p + 1]).astype(o_ref.dtype)


def _sba_sample(q, kn, vn, kc, vc, u, un, batch, tn, tk):
    n = q.shape[0]
    past = kc.shape[1]
    nj = past // tk
    return pl.pallas_call(
        functools.partial(_sba_sample_kernel, tn=tn, tk=tk),
        grid=(batch, nj),
        in_specs=[
            pl.BlockSpec((tn, SBA_W), lambda b, j: (b, 0)),
            pl.BlockSpec((tn, SBA_W), lambda b, j: (b, 0)),
            pl.BlockSpec((tn, SBA_W), lambda b, j: (b, 0)),
            pl.BlockSpec((1, tk, SBA_W), lambda b, j: (b, nj - 1 - j, 0)),
            pl.BlockSpec((1, tk, SBA_W), lambda b, j: (b, nj - 1 - j, 0)),
            pl.BlockSpec((tk, tk), lambda b, j: (0, 0)),
            pl.BlockSpec((tn, tn), lambda b, j: (0, 0)),
        ],
        out_specs=pl.BlockSpec((tn, SBA_W), lambda b, j: (b, 0)),
        out_shape=jax.ShapeDtypeStruct((n, SBA_W), BF16),
        scratch_shapes=[pltpu.VMEM((SBA_HEADS, tn, LANES), F32), pltpu.VMEM((SBA_HEADS, tn, LANES), F32)],
        compiler_params=_cparams(("parallel", "arbitrary")),
        name="sba_sample",
    )(q, kn, vn, kc, vc, u, un)


def _conv_kernel(h0_ref, glu_ref, wdw_ref, bdw_ref, g_ref, b_ref, wpw_ref, o_ref, buf_ref, *, tt, rc):
    ti = pl.program_id(1)

    @pl.when(ti == 0)
    def _():
        buf_ref[0:HALO, :] = h0_ref[0]

    @pl.when(ti > 0)
    def _():
        buf_ref[0:HALO, :] = buf_ref[tt:tt + HALO, :]

    buf_ref[HALO:HALO + tt, :] = glu_ref[...]
    base = HALO - (CONV_K - 1)
    wdw = wdw_ref[...]
    for r0 in range(0, tt, rc):
        acc = jnp.zeros((rc, CONV_W), F32)
        for kk in range(CONV_K):
            acc = acc + buf_ref[pl.ds(base + r0 + kk, rc), :] * wdw[kk:kk + 1, :]
        u = _ln(acc + bdw_ref[...], g_ref[...], b_ref[...])
        s = u * jax.nn.sigmoid(u)
        o_ref[pl.ds(r0, rc), :] = _dot(s.astype(BF16), wpw_ref[...]).astype(o_ref.dtype)


def _conv_module(h0, glu, w_dw, b_dw, g, b, w_pw_bf16, batch, seq, tt):
    n = glu.shape[0]
    nt = seq // tt
    rc = min(tt, 64)
    full = lambda s: pl.BlockSpec(s, lambda bi, ti: (0,) * len(s))
    return pl.pallas_call(
        functools.partial(_conv_kernel, tt=tt, rc=rc),
        grid=(batch, nt),
        in_specs=[
            pl.BlockSpec((1, HALO, CONV_W), lambda bi, ti: (bi, 0, 0)),
            pl.BlockSpec((tt, CONV_W), lambda bi, ti: (bi * nt + ti, 0)),
            full((CONV_K, CONV_W)), full((1, CONV_W)), full((1, CONV_W)), full((1, CONV_W)),
            full((CONV_W, CONV_W)),
        ],
        out_specs=pl.BlockSpec((tt, CONV_W), lambda bi, ti: (bi * nt + ti, 0)),
        out_shape=jax.ShapeDtypeStruct((n, CONV_W), BF16),
        scratch_shapes=[pltpu.VMEM((HALO + tt, CONV_W), F32)],
        compiler_params=_cparams(("parallel", "arbitrary")),
        name="conv_module",
    )(h0, glu, w_dw, b_dw, g, b, w_pw_bf16)


def _mem_kv_kernel(m_ref, w_ref, k_ref, v_ref):
    m = m_ref[...].astype(BF16)
    k_ref[...] = _dot(m, w_ref[:, 0:MEM_W])
    v_ref[...] = _dot(m, w_ref[:, MEM_W:2 * MEM_W])


def _mem_kv(mem2d, w_kv_bf16, tm):
    n = mem2d.shape[0]
    return pl.pallas_call(
        _mem_kv_kernel,
        grid=(n // tm,),
        in_specs=[pl.BlockSpec((tm, D_MODEL), lambda i: (i, 0)),
                  pl.BlockSpec((D_MODEL, 2 * MEM_W), lambda i: (0, 0))],
        out_specs=[pl.BlockSpec((tm, MEM_W), lambda i: (i, 0))] * 2,
        out_shape=[jax.ShapeDtypeStruct((n, MEM_W), F32)] * 2,
        compiler_params=_cparams(("parallel",)),
        name="mem_kv",
    )(mem2d, w_kv_bf16)


def _mem_attn_kernel(q_ref, k_ref, v_ref, o_ref):
    q = q_ref[...]
    tq = q.shape[0]
    low = _head_masks((tq, LANES))
    for p in range(MEM_W // LANES):
        qp = q[:, p * LANES:(p + 1) * LANES]
        kb = k_ref[0, :, p * LANES:(p + 1) * LANES].astype(BF16)
        vb = v_ref[0, :, p * LANES:(p + 1) * LANES].astype(BF16)
        outs = []
        for h in range(2):
            qm = jnp.where(low if h == 0 else jnp.logical_not(low), qp, jnp.zeros_like(qp))
            s = _dot_nt(qm, kb)
            e = jnp.exp(s - jnp.max(s, axis=-1, keepdims=True))
            outs.append(_dot(e.astype(BF16), vb) / jnp.sum(e, axis=-1, keepdims=True))
        o_ref[:, p * LANES:(p + 1) * LANES] = jnp.where(low, outs[0], outs[1]).astype(o_ref.dtype)


def _mem_attn(mq, mk, mv, batch, seq, tq):
    n = mq.shape[0]
    nq = seq // tq
    return pl.pallas_call(
        _mem_attn_kernel,
        grid=(batch, nq),
        in_specs=[pl.BlockSpec((tq, MEM_W), lambda b, i: (b * nq + i, 0)),
                  pl.BlockSpec((1, N_MEM, MEM_W), lambda b, i: (b, 0, 0)),
                  pl.BlockSpec((1, N_MEM, MEM_W), lambda b, i: (b, 0, 0))],
        out_specs=pl.BlockSpec((tq, MEM_W), lambda b, i: (b * nq + i, 0)),
        out_shape=jax.ShapeDtypeStruct((n, MEM_W), BF16),
        compiler_params=_cparams(("parallel", "parallel")),
        name="mem_attn",
    )(mq, mk, mv)


def _split2(x):
    a = x.astype(BF16)
    b = (x - a.astype(F32)).astype(BF16)
    return a, b


def _post_kernel(x_ref, sba_ref, conv_ref, mem_ref, g0_ref, b0_ref, wo_ref, g1_ref, b1_ref,
                 wr_ref, br_ref, x1_ref, x1b_ref, gate_ref):
    xn = _ln(x_ref[...], g0_ref[...], b0_ref[...])
    mix = _dot(sba_ref[...], wo_ref[0:SBA_W, :])
    mix = mix + _dot(conv_ref[...], wo_ref[SBA_W:SBA_W + CONV_W, :])
    mix = mix + _dot(mem_ref[...], wo_ref[SBA_W + CONV_W:SBA_W + CONV_W + MEM_W, :])
    x1 = _ln(DEEPNORM_ALPHA * xn + mix, g1_ref[...], b1_ref[...])
    x1_ref[...] = x1
    x1b_ref[...] = x1.astype(BF16)

    a0, a1 = _split2(x1)
    w0, w1 = wr_ref[0], wr_ref[1]
    logits = _dot(a0, w0) + (_dot(a0, w1) + _dot(a1, w0)) + br_ref[...]
    tm = logits.shape[0]
    lane = lax.broadcasted_iota(jnp.int32, (tm, LANES), 1).astype(F32)
    neg = jnp.float32(-jnp.inf)
    big = jnp.float32(LANES)
    is_g = jnp.logical_and(lane >= N_EXPERTS, lane < N_EXPERTS + N_GROUPS)
    gl = jnp.where(is_g, logits, neg)
    gmax = jnp.max(gl, axis=-1, keepdims=True)
    g_idx = jnp.min(jnp.where(gl == gmax, lane, big), axis=-1, keepdims=True) - N_EXPERTS
    g_w = 1.0 / jnp.sum(jnp.exp(gl - gmax), axis=-1, keepdims=True)
    in_grp = jnp.logical_and(lane >= g_idx * EXPERTS_PER_GROUP, lane < (g_idx + 1.0) * EXPERTS_PER_GROUP)
    el = jnp.where(in_grp, logits, neg)
    v1 = jnp.max(el, axis=-1, keepdims=True)
    i1 = jnp.min(jnp.where(el == v1, lane, big), axis=-1, keepdims=True)
    el2 = jnp.where(lane == i1, neg, el)
    v2 = jnp.max(el2, axis=-1, keepdims=True)
    i2 = jnp.min(jnp.where(el2 == v2, lane, big), axis=-1, keepdims=True)
    e2 = jnp.exp(v2 - v1)
    p1 = 1.0 / (1.0 + e2)
    p2 = e2 / (1.0 + e2)
    gate_ref[...] = jnp.where(lane == i1, p1 * g_w, 0.0) + jnp.where(lane == i2, p2 * g_w, 0.0)


def _post(x2d, sba, conv, mem, g0, b0, wo_bf16, g1, b1, wr3, br, tm):
    n = x2d.shape[0]
    row = lambda w: pl.BlockSpec((tm, w), lambda i: (i, 0))
    full = lambda s: pl.BlockSpec(s, lambda i: (0,) * len(s))
    return pl.pallas_call(
        _post_kernel,
        grid=(n // tm,),
        in_specs=[row(D_MODEL), row(SBA_W), row(CONV_W), row(MEM_W),
                  full((1, D_MODEL)), full((1, D_MODEL)), full((D_MODEL, D_MODEL)),
                  full((1, D_MODEL)), full((1, D_MODEL)),
                  full((2, D_MODEL, LANES)), full((1, LANES))],
        out_specs=[row(D_MODEL), row(D_MODEL), row(LANES)],
        out_shape=[jax.ShapeDtypeStruct((n, D_MODEL), F32),
                   jax.ShapeDtypeStruct((n, D_MODEL), BF16),
                   jax.ShapeDtypeStruct((n, LANES), F32)],
        compiler_params=_cparams(("parallel",)),
        name="post",
    )(x2d, sba, conv, mem, g0, b0, wo_bf16, g1, b1, wr3, br)


def _moe_kernel(x1_ref, x1b_ref, gate_ref, wgu_ref, wd_ref, g2_ref, b2_ref, o_ref, acc_ref):
    e = pl.program_id(1)

    @pl.when(e == 0)
    def _():
        acc_ref[...] = jnp.zeros_like(acc_ref)

    xb = x1b_ref[...]
    a = _dot(xb, wgu_ref[0, :, 0:D_EXPERT])
    u = _dot(xb, wgu_ref[0, :, D_EXPERT:2 * D_EXPERT])
    gate = gate_ref[...]
    lane = lax.broadcasted_iota(jnp.int32, gate.shape, 1)
    ge = jnp.sum(jnp.where(lane == e, gate, 0.0), axis=-1, keepdims=True)
    hid = (a * jax.nn.sigmoid(a)) * u * ge
    acc_ref[...] += _dot(hid.astype(BF16), wd_ref[0])

    @pl.when(e == pl.num_programs(1) - 1)
    def _():
        o_ref[...] = _ln(DEEPNORM_ALPHA * x1_ref[...] + acc_ref[...], g2_ref[...], b2_ref[...])


def _moe(x1, x1b, gate, wgu_bf16, wd_bf16, g2, b2, tm):
    n = x1.shape[0]
    row = lambda w: pl.BlockSpec((tm, w), lambda i, e: (i, 0))
    return pl.pallas_call(
        _moe_kernel,
        grid=(n // tm, N_EXPERTS),
        in_specs=[row(D_MODEL), row(D_MODEL), row(LANES),
                  pl.BlockSpec((1, D_MODEL, 2 * D_EXPERT), lambda i, e: (e, 0, 0)),
                  pl.BlockSpec((1, D_EXPERT, D_MODEL), lambda i, e: (e, 0, 0)),
                  pl.BlockSpec((1, D_MODEL), lambda i, e: (0, 0)),
                  pl.BlockSpec((1, D_MODEL), lambda i, e: (0, 0))],
        out_specs=row(D_MODEL),
        out_shape=jax.ShapeDtypeStruct((n, D_MODEL), F32),
        scratch_shapes=[pltpu.VMEM((tm, D_MODEL), F32)],
        compiler_params=_cparams(("parallel", "arbitrary")),
        name="moe",
    )(x1, x1b, gate, wgu_bf16, wd_bf16, g2, b2)


def _strict_upper(n):
    r = lax.broadcasted_iota(jnp.int32, (n, n), 0)
    c = lax.broadcasted_iota(jnp.int32, (n, n), 1)
    return (r > c).astype(BF16)


def kernel(x_prompt, x_sample, mem_prompt, cache_sba_k, cache_sba_v, cache_conv, cache_mem_k, cache_mem_v,
           ln0_g, ln0_b, w_in, w_dw, b_dw, lnc_g, lnc_b, w_cpw, w_mk, w_mv, w_out, ln1_g, ln1_b,
           w_rg, b_rg, w_re, b_re, w_eg, w_eu, w_ed, ln2_g, ln2_b):
    bp, tp, _ = x_prompt.shape
    bs, ts, _ = x_sample.shape
    past = cache_sba_k.shape[2]
    l = 0
    r2 = lambda a: a.reshape(1, -1)

    w_in_b = w_in[l].astype(BF16)
    w_kv_b = jnp.concatenate([w_mk[l], w_mv[l]], axis=1).astype(BF16)
    w_out_b = w_out[l].astype(BF16)
    w_cpw_b = w_cpw[l].astype(BF16)
    wr = jnp.zeros((D_MODEL, LANES), F32)
    wr = wr.at[:, 0:N_EXPERTS].set(w_re[l]).at[:, N_EXPERTS:N_EXPERTS + N_GROUPS].set(w_rg[l])
    wr0 = wr.astype(BF16)
    wr1 = (wr - wr0.astype(F32)).astype(BF16)
    wr3 = jnp.stack([wr0, wr1])
    br = jnp.zeros((1, LANES), F32)
    br = br.at[0, 0:N_EXPERTS].set(b_re[l]).at[0, N_EXPERTS:N_EXPERTS + N_GROUPS].set(b_rg[l])
    wgu_b = jnp.concatenate([w_eg[l], w_eu[l]], axis=-1).astype(BF16)
    wd_b = w_ed[l].astype(BF16)
    g0, b0 = r2(ln0_g), r2(ln0_b)

    tq = 256
    tk_s = 512
    u_p = _strict_upper(tq)
    u_s = _strict_upper(tk_s)
    u_n = _strict_upper(ts)

    def group(x, batch, seq, tm):
        x2d = x.reshape(batch * seq, D_MODEL)
        return x2d, _in_proj(x2d, g0, b0, w_in_b, tm)

    xp2d, (qp, kbp, vbp, kp, vp, glup, mqp) = group(x_prompt, bp, tp, 512)
    xs2d, (qs, kbs, vbs, ks, vs, glus, mqs) = group(x_sample, bs, ts, 512)

    sba_p = _sba_prompt(qp, kbp, vbp, u_p, bp, tp, tq)
    sba_s = _sba_sample(qs, kbs, vbs, cache_sba_k[l].reshape(bs, past, SBA_W),
                        cache_sba_v[l].reshape(bs, past, SBA_W), u_s, u_n, bs, ts, tk_s)

    conv_w = (w_dw[l], r2(b_dw[l]), r2(lnc_g[l]), r2(lnc_b[l]), w_cpw_b)
    h0_p = jnp.zeros((bp, HALO, CONV_W), F32)
    h0_s = jnp.pad(cache_conv[l], ((0, 0), (HALO - (CONV_K - 1), 0), (0, 0)))
    conv_p = _conv_module(h0_p, glup, *conv_w, bp, tp, 256)
    conv_s = _conv_module(h0_s, glus, *conv_w, bs, ts, ts)

    mk, mv = _mem_kv(mem_prompt.reshape(bp * N_MEM, D_MODEL), w_kv_b, 512)
    mem_p = _mem_attn(mqp, mk.reshape(bp, N_MEM, MEM_W), mv.reshape(bp, N_MEM, MEM_W), bp, tp, 512)
    mem_s = _mem_attn(mqs, cache_mem_k[l].reshape(bs, N_MEM, MEM_W),
                      cache_mem_v[l].reshape(bs, N_MEM, MEM_W), bs, ts, ts)

    post_w = (g0, b0, w_out_b, r2(ln1_g[l]), r2(ln1_b[l]), wr3, br)
    x1p, x1bp, gatep = _post(xp2d, sba_p, conv_p, mem_p, *post_w, 512)
    x1s, x1bs, gates = _post(xs2d, sba_s, conv_s, mem_s, *post_w, 512)
    g2, b2 = r2(ln2_g[l]), r2(ln2_b[l])
    yp = _moe(x1p, x1bp, gatep, wgu_b, wd_b, g2, b2, 1024)
    ys = _moe(x1s, x1bs, gates, wgu_b, wd_b, g2, b2, 1024)

    hd = (SBA_HEADS, HEAD_DIM)
    glup3 = glup.reshape(bp, tp, CONV_W)
    glus3 = glus.reshape(bs, ts, CONV_W)
    conv_tail_s = jnp.concatenate([cache_conv[l], glus3], axis=1)[:, -(CONV_K - 1):]
    return (
        yp.reshape(bp, tp, D_MODEL),
        ys.reshape(bs, ts, D_MODEL),
        kp.reshape(1, bp, tp, *hd),
        vp.reshape(1, bp, tp, *hd),
        glup3[:, -(CONV_K - 1):][None],
        mk.reshape(1, bp, N_MEM, MEM_HEADS, HEAD_DIM),
        mv.reshape(1, bp, N_MEM, MEM_HEADS, HEAD_DIM),
        ks.reshape(1, bs, ts, *hd),
        vs.reshape(1, bs, ts, *hd),
        conv_tail_s[None],
    )
```

```python
import functools

import jax
import jax.numpy as jnp
from jax import lax
from jax.experimental import pallas as pl
from jax.experimental.pallas import tpu as pltpu

F32 = jnp.float32
BF16 = jnp.bfloat16

D_MODEL = 1024
HEAD_DIM = 64
SBA_HEADS = 8
SBA_W = SBA_HEADS * HEAD_DIM
CONV_W = 256
CONV_K = 31
MEM_HEADS = 4
MEM_W = MEM_HEADS * HEAD_DIM
N_MEM = 256
N_GROUPS = 4
EXPERTS_PER_GROUP = 8
N_EXPERTS = N_GROUPS * EXPERTS_PER_GROUP
D_EXPERT = 256
DEPTH = 1
DEEPNORM_ALPHA = (2 * DEPTH) ** 0.25
QK_SCALE = HEAD_DIM ** -0.5
LN_EPS = 1e-5
LOG2E = 1.4426950408889634

LANES = 128
HALO = 32
MOE_TM = 256
SP_LINEAR = 100.0
VMEM_LIMIT = 48 * 1024 * 1024


def _cparams(sem):
    return pltpu.CompilerParams(dimension_semantics=sem, vmem_limit_bytes=VMEM_LIMIT)


def _ln(x, g, b):
    mu = jnp.mean(x, axis=-1, keepdims=True)
    xc = x - mu
    var = jnp.mean(xc * xc, axis=-1, keepdims=True)
    return xc * lax.rsqrt(var + LN_EPS) * g + b


def _dot(a, b):
    return jnp.dot(a, b, preferred_element_type=F32)


def _dot_nt(a, b):
    return lax.dot_general(a, b, (((1,), (1,)), ((), ())), preferred_element_type=F32)


def _in_proj_kernel(x_ref, g_ref, b_ref, w_ref, q_ref, kb_ref, vb_ref, k_ref, v_ref, glu_ref, mq_ref):
    xn = _ln(x_ref[...], g_ref[...], b_ref[...]).astype(BF16)
    q = _dot(xn, w_ref[:, 0:SBA_W])
    q_ref[...] = (q * (QK_SCALE * LOG2E)).astype(BF16)
    k = _dot(xn, w_ref[:, SBA_W:2 * SBA_W])
    k_ref[...] = k
    kb_ref[...] = k.astype(BF16)
    v = _dot(xn, w_ref[:, 2 * SBA_W:3 * SBA_W])
    v_ref[...] = v
    vb_ref[...] = v.astype(BF16)
    c0 = 3 * SBA_W
    cv = _dot(xn, w_ref[:, c0:c0 + CONV_W])
    cg = _dot(xn, w_ref[:, c0 + CONV_W:c0 + 2 * CONV_W])
    glu_ref[...] = cv * jax.nn.sigmoid(cg)
    mq = _dot(xn, w_ref[:, c0 + 2 * CONV_W:c0 + 2 * CONV_W + MEM_W])
    mq_ref[...] = (mq * QK_SCALE).astype(BF16)


def _in_proj(x2d, g, b, w_bf16, tm):
    n = x2d.shape[0]
    in_w = w_bf16.shape[1]
    row = lambda w: pl.BlockSpec((tm, w), lambda i: (i, 0))
    full = lambda s: pl.BlockSpec(s, lambda i: (0, 0))
    return pl.pallas_call(
        _in_proj_kernel,
        grid=(n // tm,),
        in_specs=[row(D_MODEL), full((1, D_MODEL)), full((1, D_MODEL)), full((D_MODEL, in_w))],
        out_specs=[row(SBA_W), row(SBA_W), row(SBA_W), row(SBA_W), row(SBA_W), row(CONV_W), row(MEM_W)],
        out_shape=[
            jax.ShapeDtypeStruct((n, SBA_W), BF16),
            jax.ShapeDtypeStruct((n, SBA_W), BF16),
            jax.ShapeDtypeStruct((n, SBA_W), BF16),
            jax.ShapeDtypeStruct((n, SBA_W), F32),
            jax.ShapeDtypeStruct((n, SBA_W), F32),
            jax.ShapeDtypeStruct((n, CONV_W), F32),
            jax.ShapeDtypeStruct((n, MEM_W), BF16),
        ],
        compiler_params=_cparams(("parallel",)),
        name="in_proj",
    )(x2d, g, b, w_bf16)


def _head_masks(shape):
    lane = lax.broadcasted_iota(jnp.int32, shape, 1)
    return lane < HEAD_DIM


def _stack_heads(q, low):
    zero = jnp.zeros_like(q)
    return jnp.concatenate([jnp.where(low, q, zero), jnp.where(low, zero, q)], axis=0)


def _sba_tile(z, u, mask):
    sp = jnp.where(z > SP_LINEAR, z, jnp.log(1.0 + jnp.exp2(z)) * LOG2E)
    if mask is not None:
        sp = jnp.where(mask, sp, 0.0)
    w = jnp.exp2(z - _dot(sp.astype(BF16), u))
    if mask is not None:
        w = jnp.where(mask, w, 0.0)
    return w.astype(BF16), jnp.sum(sp, axis=1, keepdims=True)


def _sba_prompt_kernel(q_ref, k_ref, v_ref, u_ref, o_ref, *, tq, hp):
    qi = pl.program_id(2)
    low = _head_masks((tq, LANES))
    u = u_ref[...]
    row = lax.broadcasted_iota(jnp.int32, (2 * tq, tq), 0) & (tq - 1)
    col = lax.broadcasted_iota(jnp.int32, (2 * tq, tq), 1)
    qs = [_stack_heads(q_ref[:, p * LANES:(p + 1) * LANES], low) for p in range(hp)]

    def blk(j, carry, mask):
        start = pl.multiple_of(j * tq, tq)
        out = []
        for p in range(hp):
            c, acc = carry[p]
            z = _dot_nt(qs[p], k_ref[pl.ds(start, tq), p * LANES:(p + 1) * LANES])
            w, rs = _sba_tile(z, u, mask)
            acc = acc + _dot(w, v_ref[pl.ds(start, tq), p * LANES:(p + 1) * LANES]) * jnp.exp2(-c)
            out.append((c + rs, acc))
        return tuple(out)

    carry = tuple((jnp.zeros((2 * tq, 1), F32), jnp.zeros((2 * tq, LANES), F32)) for _ in range(hp))
    carry = blk(qi, carry, col < row)
    carry = lax.fori_loop(0, qi, lambda t, cr: blk(qi - 1 - t, cr, None), carry)
    for p in range(hp):
        acc = carry[p][1]
        o_ref[:, p * LANES:(p + 1) * LANES] = jnp.where(low, acc[0:tq], acc[tq:2 * tq]).astype(o_ref.dtype)


def _sba_prompt(q, k, v, u, batch, seq, tq, hp):
    n = q.shape[0]
    nq = seq // tq
    groups = SBA_W // (LANES * hp)
    return pl.pallas_call(
        functools.partial(_sba_prompt_kernel, tq=tq, hp=hp),
        grid=(batch, groups, nq),
        in_specs=[
            pl.BlockSpec((tq, hp * LANES), lambda b, p, i: (b * nq + i, p)),
            pl.BlockSpec((seq, hp * LANES), lambda b, p, i: (b, p)),
            pl.BlockSpec((seq, hp * LANES), lambda b, p, i: (b, p)),
            pl.BlockSpec((tq, tq), lambda b, p, i: (0, 0)),
        ],
        out_specs=pl.BlockSpec((tq, hp * LANES), lambda b, p, i: (b * nq + i, p)),
        out_shape=jax.ShapeDtypeStruct((n, SBA_W), BF16),
        compiler_params=_cparams(("parallel", "parallel", "arbitrary")),
        name="sba_prompt",
    )(q, k, v, u)


def _sba_sample_kernel(q_ref, kn_ref, vn_ref, kc_ref, vc_ref, u_ref, un_ref, o_ref, qs_ref, c_ref, acc_ref,
                       *, tn, tk, nsub):
    j = pl.program_id(1)
    nj = pl.num_programs(1)
    low = _head_masks((tn, LANES))
    pairs = SBA_W // LANES
    rows = 2 * tn

    def step(get_k, get_v, u, mask):
        z = jnp.concatenate([_dot_nt(qs_ref[p], get_k(p)) for p in range(pairs)], axis=0)
        w, rs = _sba_tile(z, u, mask)
        pv = jnp.concatenate([_dot(w[p * rows:(p + 1) * rows], get_v(p)) for p in range(pairs)], axis=0)
        c = c_ref[...]
        acc_ref[...] += pv * jnp.exp2(-c)
        c_ref[...] = c + rs

    @pl.when(j == 0)
    def _():
        for p in range(pairs):
            qs_ref[p] = _stack_heads(q_ref[:, p * LANES:(p + 1) * LANES], low)
        c_ref[...] = jnp.zeros_like(c_ref)
        acc_ref[...] = jnp.zeros_like(acc_ref)
        row = lax.broadcasted_iota(jnp.int32, (pairs * rows, tn), 0) & (tn - 1)
        col = lax.broadcasted_iota(jnp.int32, (pairs * rows, tn), 1)
        step(lambda p: kn_ref[:, p * LANES:(p + 1) * LANES], lambda p: vn_ref[:, p * LANES:(p + 1) * LANES],
             un_ref[...], col < row)

    for s in reversed(range(nsub)):
        step(lambda p: kc_ref[0, s * tk:(s + 1) * tk, p * LANES:(p + 1) * LANES].astype(BF16),
             lambda p: vc_ref[0, s * tk:(s + 1) * tk, p * LANES:(p + 1) * LANES].astype(BF16),
             u_ref[...], None)

    @pl.when(j == nj - 1)
    def _():
        for p in range(pairs):
            a0 = acc_ref[p * rows:p * rows + tn, :]
            a1 = acc_ref[p * rows + tn:(p + 1) * rows, :]
            o_ref[:, p * LANES:(p + 1) * LANES] = jnp.where(low, a0, a1).astype(o_ref.dtype)


def _sba_sample(q, kn, vn, kc, vc, u, un, batch, tn, tk, nsub):
    n = q.shape[0]
    past = kc.shape[1]
    tkb = tk * nsub
    nj = past // tkb
    rows = SBA_HEADS * tn
    return pl.pallas_call(
        functools.partial(_sba_sample_kernel, tn=tn, tk=tk, nsub=nsub),
        grid=(batch, nj),
        in_specs=[
            pl.BlockSpec((tn, SBA_W), lambda b, j: (b, 0)),
            pl.BlockSpec((tn, SBA_W), lambda b, j: (b, 0)),
            pl.BlockSpec((tn, SBA_W), lambda b, j: (b, 0)),
            pl.BlockSpec((1, tkb, SBA_W), lambda b, j: (b, nj - 1 - j, 0)),
            pl.BlockSpec((1, tkb, SBA_W), lambda b, j: (b, nj - 1 - j, 0)),
            pl.BlockSpec((tk, tk), lambda b, j: (0, 0)),
            pl.BlockSpec((tn, tn), lambda b, j: (0, 0)),
        ],
        out_specs=pl.BlockSpec((tn, SBA_W), lambda b, j: (b, 0)),
        out_shape=jax.ShapeDtypeStruct((n, SBA_W), BF16),
        scratch_shapes=[pltpu.VMEM((SBA_W // LANES, 2 * tn, LANES), BF16),
                        pltpu.VMEM((rows, 1), F32), pltpu.VMEM((rows, LANES), F32)],
        compiler_params=_cparams(("parallel", "arbitrary")),
        name="sba_sample",
    )(q, kn, vn, kc, vc, u, un)


def _conv_kernel(h0_ref, glu_ref, wdw_ref, bdw_ref, g_ref, b_ref, wpw_ref, o_ref, buf_ref, *, tt, rc):
    ti = pl.program_id(1)

    @pl.when(ti == 0)
    def _():
        buf_ref[0:HALO, :] = h0_ref[0]

    @pl.when(ti > 0)
    def _():
        buf_ref[0:HALO, :] = buf_ref[tt:tt + HALO, :]

    buf_ref[HALO:HALO + tt, :] = glu_ref[...]
    base = HALO - (CONV_K - 1)
    wdw = wdw_ref[...]
    for r0 in range(0, tt, rc):
        acc = jnp.zeros((rc, CONV_W), F32)
        for kk in range(CONV_K):
            acc = acc + buf_ref[pl.ds(base + r0 + kk, rc), :] * wdw[kk:kk + 1, :]
        u = _ln(acc + bdw_ref[...], g_ref[...], b_ref[...])
        s = u * jax.nn.sigmoid(u)
        o_ref[pl.ds(r0, rc), :] = _dot(s.astype(BF16), wpw_ref[...]).astype(o_ref.dtype)


def _conv_module(h0, glu, w_dw, b_dw, g, b, w_pw_bf16, batch, seq, tt):
    n = glu.shape[0]
    nt = seq // tt
    rc = min(tt, 64)
    full = lambda s: pl.BlockSpec(s, lambda bi, ti: (0,) * len(s))
    return pl.pallas_call(
        functools.partial(_conv_kernel, tt=tt, rc=rc),
        grid=(batch, nt),
        in_specs=[
            pl.BlockSpec((1, HALO, CONV_W), lambda bi, ti: (bi, 0, 0)),
            pl.BlockSpec((tt, CONV_W), lambda bi, ti: (bi * nt + ti, 0)),
            full((CONV_K, CONV_W)), full((1, CONV_W)), full((1, CONV_W)), full((1, CONV_W)),
            full((CONV_W, CONV_W)),
        ],
        out_specs=pl.BlockSpec((tt, CONV_W), lambda bi, ti: (bi * nt + ti, 0)),
        out_shape=jax.ShapeDtypeStruct((n, CONV_W), BF16),
        scratch_shapes=[pltpu.VMEM((HALO + tt, CONV_W), F32)],
        compiler_params=_cparams(("parallel", "arbitrary")),
        name="conv_module",
    )(h0, glu, w_dw, b_dw, g, b, w_pw_bf16)


def _mem_kv_kernel(m_ref, w_ref, k_ref, v_ref):
    m = m_ref[...].astype(BF16)
    k_ref[...] = _dot(m, w_ref[:, 0:MEM_W])
    v_ref[...] = _dot(m, w_ref[:, MEM_W:2 * MEM_W])


def _mem_kv(mem2d, w_kv_bf16, tm):
    n = mem2d.shape[0]
    return pl.pallas_call(
        _mem_kv_kernel,
        grid=(n // tm,),
        in_specs=[pl.BlockSpec((tm, D_MODEL), lambda i: (i, 0)),
                  pl.BlockSpec((D_MODEL, 2 * MEM_W), lambda i: (0, 0))],
        out_specs=[pl.BlockSpec((tm, MEM_W), lambda i: (i, 0))] * 2,
        out_shape=[jax.ShapeDtypeStruct((n, MEM_W), F32)] * 2,
        compiler_params=_cparams(("parallel",)),
        name="mem_kv",
    )(mem2d, w_kv_bf16)


def _mem_attn_kernel(q_ref, k_ref, v_ref, o_ref):
    q = q_ref[...]
    tq = q.shape[0]
    low = _head_masks((tq, LANES))
    for p in range(MEM_W // LANES):
        qp = q[:, p * LANES:(p + 1) * LANES]
        kb = k_ref[0, :, p * LANES:(p + 1) * LANES].astype(BF16)
        vb = v_ref[0, :, p * LANES:(p + 1) * LANES].astype(BF16)
        outs = []
        for h in range(2):
            qm = jnp.where(low if h == 0 else jnp.logical_not(low), qp, jnp.zeros_like(qp))
            s = _dot_nt(qm, kb)
            e = jnp.exp(s - jnp.max(s, axis=-1, keepdims=True))
            outs.append(_dot(e.astype(BF16), vb) / jnp.sum(e, axis=-1, keepdims=True))
        o_ref[:, p * LANES:(p + 1) * LANES] = jnp.where(low, outs[0], outs[1]).astype(o_ref.dtype)


def _mem_attn(mq, mk, mv, batch, seq, tq):
    n = mq.shape[0]
    nq = seq // tq
    return pl.pallas_call(
        _mem_attn_kernel,
        grid=(batch, nq),
        in_specs=[pl.BlockSpec((tq, MEM_W), lambda b, i: (b * nq + i, 0)),
                  pl.BlockSpec((1, N_MEM, MEM_W), lambda b, i: (b, 0, 0)),
                  pl.BlockSpec((1, N_MEM, MEM_W), lambda b, i: (b, 0, 0))],
        out_specs=pl.BlockSpec((tq, MEM_W), lambda b, i: (b * nq + i, 0)),
        out_shape=jax.ShapeDtypeStruct((n, MEM_W), BF16),
        compiler_params=_cparams(("parallel", "parallel")),
        name="mem_attn",
    )(mq, mk, mv)


def _split2(x):
    a = x.astype(BF16)
    b = (x - a.astype(F32)).astype(BF16)
    return a, b


def _post_kernel(x_ref, sba_ref, conv_ref, mem_ref, g0_ref, b0_ref, wo_ref, g1_ref, b1_ref,
                 wr_ref, br_ref, x1_ref, route_ref):
    xn = _ln(x_ref[...], g0_ref[...], b0_ref[...])
    mix = _dot(sba_ref[...], wo_ref[0:SBA_W, :])
    mix = mix + _dot(conv_ref[...], wo_ref[SBA_W:SBA_W + CONV_W, :])
    mix = mix + _dot(mem_ref[...], wo_ref[SBA_W + CONV_W:SBA_W + CONV_W + MEM_W, :])
    x1 = _ln(DEEPNORM_ALPHA * xn + mix, g1_ref[...], b1_ref[...])
    x1_ref[...] = x1

    a0, a1 = _split2(x1)
    w0, w1 = wr_ref[0], wr_ref[1]
    logits = _dot(a0, w0) + (_dot(a0, w1) + _dot(a1, w0)) + br_ref[...]
    tm = logits.shape[0]
    lane = lax.broadcasted_iota(jnp.int32, (tm, LANES), 1).astype(F32)
    neg = jnp.float32(-jnp.inf)
    big = jnp.float32(LANES)
    is_g = jnp.logical_and(lane >= N_EXPERTS, lane < N_EXPERTS + N_GROUPS)
    gl = jnp.where(is_g, logits, neg)
    gmax = jnp.max(gl, axis=-1, keepdims=True)
    g_idx = jnp.min(jnp.where(gl == gmax, lane, big), axis=-1, keepdims=True) - N_EXPERTS
    g_w = 1.0 / jnp.sum(jnp.exp(gl - gmax), axis=-1, keepdims=True)
    in_grp = jnp.logical_and(lane >= g_idx * EXPERTS_PER_GROUP, lane < (g_idx + 1.0) * EXPERTS_PER_GROUP)
    el = jnp.where(in_grp, logits, neg)
    v1 = jnp.max(el, axis=-1, keepdims=True)
    i1 = jnp.min(jnp.where(el == v1, lane, big), axis=-1, keepdims=True)
    el2 = jnp.where(lane == i1, neg, el)
    v2 = jnp.max(el2, axis=-1, keepdims=True)
    i2 = jnp.min(jnp.where(el2 == v2, lane, big), axis=-1, keepdims=True)
    e2 = jnp.exp(v2 - v1)
    p1 = 1.0 / (1.0 + e2)
    p2 = e2 / (1.0 + e2)
    route_ref[...] = jnp.where(lane == 0.0, i1, jnp.where(lane == 1.0, i2, jnp.where(
        lane == 2.0, p1 * g_w, jnp.where(lane == 3.0, p2 * g_w, 0.0))))


def _post(x2d, sba, conv, mem, g0, b0, wo_bf16, g1, b1, wr3, br, tm):
    n = x2d.shape[0]
    row = lambda w: pl.BlockSpec((tm, w), lambda i: (i, 0))
    full = lambda s: pl.BlockSpec(s, lambda i: (0,) * len(s))
    return pl.pallas_call(
        _post_kernel,
        grid=(n // tm,),
        in_specs=[row(D_MODEL), row(SBA_W), row(CONV_W), row(MEM_W),
                  full((1, D_MODEL)), full((1, D_MODEL)), full((D_MODEL, D_MODEL)),
                  full((1, D_MODEL)), full((1, D_MODEL)),
                  full((2, D_MODEL, LANES)), full((1, LANES))],
        out_specs=[row(D_MODEL), row(LANES)],
        out_shape=[jax.ShapeDtypeStruct((n, D_MODEL), F32),
                   jax.ShapeDtypeStruct((n, LANES), F32)],
        compiler_params=_cparams(("parallel",)),
        name="post",
    )(x2d, sba, conv, mem, g0, b0, wo_bf16, g1, b1, wr3, br)


def _moe_plan(route, n):
    pairs = 2 * n
    tiles = pairs // MOE_TM + N_EXPERTS
    e = jnp.concatenate([route[:, 0], route[:, 1]]).astype(jnp.int32)
    w = jnp.concatenate([route[:, 2], route[:, 3]])
    order = jnp.argsort(e, stable=True).astype(jnp.int32)
    ids = jnp.arange(N_EXPERTS, dtype=jnp.int32)
    counts = jnp.sum((e[:, None] == ids[None, :]).astype(jnp.int32), axis=0)
    ntile = (counts + MOE_TM - 1) // MOE_TM
    tile_end = jnp.cumsum(ntile)
    tile_start = tile_end - ntile
    first = jnp.cumsum(counts) - counts
    n_active = tile_end[-1]
    t = jnp.arange(tiles, dtype=jnp.int32)
    te = jnp.minimum(jnp.sum((t[:, None] >= tile_end[None, :]).astype(jnp.int32), axis=1), N_EXPERTS - 1)
    done = (t - tile_start[te]) * MOE_TM
    n_valid = jnp.where(t < n_active, jnp.clip(counts[te] - done, 0, MOE_TM), 0).astype(jnp.int32)
    r = jnp.arange(MOE_TM, dtype=jnp.int32)
    valid = r[None, :] < n_valid[:, None]
    pos = jnp.clip(first[te][:, None] + done[:, None] + r[None, :], 0, pairs - 1)
    pair = order[pos]
    src = jnp.where(valid, jnp.where(pair >= n, pair - n, pair), 0).astype(jnp.int32)
    dst = jnp.where(valid, pair, 0).astype(jnp.int32)
    wrow = jnp.where(valid, w[pair], 0.0).reshape(tiles * MOE_TM, 1)
    return (te.astype(jnp.int32), n_valid, n_active.reshape(1).astype(jnp.int32),
            src.reshape(tiles, 1, MOE_TM), dst.reshape(tiles, 1, MOE_TM), wrow)


def _moe_kernel(te_ref, nv_ref, na_ref, src0_ref, src1_ref, dst_ref, w_ref, x_hbm, wgu_ref, wd_ref,
                y_hbm, xg_ref, yo_ref, gsem, ssem):
    t = pl.program_id(0)
    n_active = na_ref[0]

    def row_in(tok, r, slot):
        return pltpu.make_async_copy(x_hbm.at[pl.ds(tok, 1)], xg_ref.at[slot, pl.ds(r, 1)], gsem.at[slot])

    def row_out(r, pair):
        return pltpu.make_async_copy(yo_ref.at[pl.ds(r, 1)], y_hbm.at[pl.ds(pair, 1)], ssem.at[0])

    def start_gather(src_ref, slot):
        def body(r, carry):
            row_in(src_ref[0, 0, r], r, slot).start()
            return carry
        lax.fori_loop(0, MOE_TM, body, 0, unroll=8)

    def wait_gather(slot):
        def body(r, carry):
            row_in(0, 0, slot).wait()
            return carry
        lax.fori_loop(0, MOE_TM, body, 0, unroll=8)

    def wait_scatter(count):
        def body(r, carry):
            row_out(0, 0).wait()
            return carry
        lax.fori_loop(0, count, body, 0)

    @pl.when(t < n_active)
    def _():
        slot = lax.rem(t, 2)

        @pl.when(t == 0)
        def _():
            start_gather(src0_ref, 0)

        @pl.when(t + 1 < n_active)
        def _():
            start_gather(src1_ref, 1 - slot)

        wait_gather(slot)
        x = xg_ref[slot].astype(BF16)
        a = _dot(x, wgu_ref[0, :, 0:D_EXPERT])
        u = _dot(x, wgu_ref[0, :, D_EXPERT:2 * D_EXPERT])
        hid = (a * jax.nn.sigmoid(a)) * u * w_ref[...]
        y = _dot(hid.astype(BF16), wd_ref[0])

        @pl.when(t > 0)
        def _():
            wait_scatter(nv_ref[t - 1])

        yo_ref[...] = y

        def body(r, carry):
            row_out(r, dst_ref[0, 0, r]).start()
            return carry
        lax.fori_loop(0, nv_ref[t], body, 0)

        @pl.when(t == n_active - 1)
        def _():
            wait_scatter(nv_ref[t])


def _moe(x1, plan, wgu_bf16, wd_bf16):
    te, n_valid, n_active, src, dst, wrow = plan
    n = x1.shape[0]
    tiles = src.shape[0]
    idx = lambda f: pl.BlockSpec((1, 1, MOE_TM), f, memory_space=pltpu.SMEM)
    grid_spec = pltpu.PrefetchScalarGridSpec(
        num_scalar_prefetch=3,
        grid=(tiles,),
        in_specs=[
            idx(lambda t, te, nv, na: (t, 0, 0)),
            idx(lambda t, te, nv, na: (jnp.minimum(t + 1, tiles - 1), 0, 0)),
            idx(lambda t, te, nv, na: (t, 0, 0)),
            pl.BlockSpec((MOE_TM, 1), lambda t, te, nv, na: (t, 0)),
            pl.BlockSpec(memory_space=pl.ANY),
            pl.BlockSpec((1, D_MODEL, 2 * D_EXPERT), lambda t, te, nv, na: (te[t], 0, 0)),
            pl.BlockSpec((1, D_EXPERT, D_MODEL), lambda t, te, nv, na: (te[t], 0, 0)),
        ],
        out_specs=pl.BlockSpec(memory_space=pl.ANY),
        scratch_shapes=[pltpu.VMEM((2, MOE_TM, D_MODEL), F32), pltpu.VMEM((MOE_TM, D_MODEL), F32),
                        pltpu.SemaphoreType.DMA((2,)), pltpu.SemaphoreType.DMA((1,))],
    )
    return pl.pallas_call(
        _moe_kernel,
        grid_spec=grid_spec,
        out_shape=jax.ShapeDtypeStruct((2 * n, D_MODEL), F32),
        compiler_params=_cparams(("arbitrary",)),
        name="moe",
    )(te, n_valid, n_active, src, src, dst, wrow, x1, wgu_bf16, wd_bf16)


def _combine_kernel(x1_ref, ya_ref, yb_ref, g_ref, b_ref, o_ref):
    f = ya_ref[...] + yb_ref[...]
    o_ref[...] = _ln(DEEPNORM_ALPHA * x1_ref[...] + f, g_ref[...], b_ref[...])


def _combine(x1, y2, g2, b2, row0, rows, tm):
    n = x1.shape[0]
    b0 = row0 // tm
    row = lambda off: pl.BlockSpec((tm, D_MODEL), lambda i: (i + off, 0))
    full = pl.BlockSpec((1, D_MODEL), lambda i: (0, 0))
    return pl.pallas_call(
        _combine_kernel,
        grid=(rows // tm,),
        in_specs=[row(b0), row(b0), row(b0 + n // tm), full, full],
        out_specs=pl.BlockSpec((tm, D_MODEL), lambda i: (i, 0)),
        out_shape=jax.ShapeDtypeStruct((rows, D_MODEL), F32),
        compiler_params=_cparams(("parallel",)),
        name="combine",
    )(x1, y2, y2, g2, b2)


def _later_or_same(n):
    r = lax.broadcasted_iota(jnp.int32, (n, n), 0)
    c = lax.broadcasted_iota(jnp.int32, (n, n), 1)
    return (r >= c).astype(BF16)


def kernel(x_prompt, x_sample, mem_prompt, cache_sba_k, cache_sba_v, cache_conv, cache_mem_k, cache_mem_v,
           ln0_g, ln0_b, w_in, w_dw, b_dw, lnc_g, lnc_b, w_cpw, w_mk, w_mv, w_out, ln1_g, ln1_b,
           w_rg, b_rg, w_re, b_re, w_eg, w_eu, w_ed, ln2_g, ln2_b):
    bp, tp, _ = x_prompt.shape
    bs, ts, _ = x_sample.shape
    past = cache_sba_k.shape[2]
    l = 0
    r2 = lambda a: a.reshape(1, -1)

    w_in_b = w_in[l].astype(BF16)
    w_kv_b = jnp.concatenate([w_mk[l], w_mv[l]], axis=1).astype(BF16)
    w_out_b = w_out[l].astype(BF16)
    w_cpw_b = w_cpw[l].astype(BF16)
    wr = jnp.zeros((D_MODEL, LANES), F32)
    wr = wr.at[:, 0:N_EXPERTS].set(w_re[l]).at[:, N_EXPERTS:N_EXPERTS + N_GROUPS].set(w_rg[l])
    wr0 = wr.astype(BF16)
    wr1 = (wr - wr0.astype(F32)).astype(BF16)
    wr3 = jnp.stack([wr0, wr1])
    br = jnp.zeros((1, LANES), F32)
    br = br.at[0, 0:N_EXPERTS].set(b_re[l]).at[0, N_EXPERTS:N_EXPERTS + N_GROUPS].set(b_rg[l])
    wgu_b = jnp.concatenate([w_eg[l], w_eu[l]], axis=-1).astype(BF16)
    wd_b = w_ed[l].astype(BF16)
    g0, b0 = r2(ln0_g), r2(ln0_b)

    tq = 256
    tk_s = 256
    u_p = _later_or_same(tq)
    u_s = _later_or_same(tk_s)
    u_n = _later_or_same(ts)

    def group(x, batch, seq, tm):
        x2d = x.reshape(batch * seq, D_MODEL)
        return x2d, _in_proj(x2d, g0, b0, w_in_b, tm)

    xp2d, (qp, kbp, vbp, kp, vp, glup, mqp) = group(x_prompt, bp, tp, 512)
    xs2d, (qs, kbs, vbs, ks, vs, glus, mqs) = group(x_sample, bs, ts, 512)

    sba_p = _sba_prompt(qp, kbp, vbp, u_p, bp, tp, tq, 4)
    sba_s = _sba_sample(qs, kbs, vbs, cache_sba_k[l].reshape(bs, past, SBA_W),
                        cache_sba_v[l].reshape(bs, past, SBA_W), u_s, u_n, bs, ts, tk_s, 4)

    conv_w = (w_dw[l], r2(b_dw[l]), r2(lnc_g[l]), r2(lnc_b[l]), w_cpw_b)
    h0_p = jnp.zeros((bp, HALO, CONV_W), F32)
    h0_s = jnp.pad(cache_conv[l], ((0, 0), (HALO - (CONV_K - 1), 0), (0, 0)))
    conv_p = _conv_module(h0_p, glup, *conv_w, bp, tp, 256)
    conv_s = _conv_module(h0_s, glus, *conv_w, bs, ts, ts)

    mk, mv = _mem_kv(mem_prompt.reshape(bp * N_MEM, D_MODEL), w_kv_b, 512)
    mem_p = _mem_attn(mqp, mk.reshape(bp, N_MEM, MEM_W), mv.reshape(bp, N_MEM, MEM_W), bp, tp, 512)
    mem_s = _mem_attn(mqs, cache_mem_k[l].reshape(bs, N_MEM, MEM_W),
                      cache_mem_v[l].reshape(bs, N_MEM, MEM_W), bs, ts, ts)

    post_w = (g0, b0, w_out_b, r2(ln1_g[l]), r2(ln1_b[l]), wr3, br)
    x1p, routep = _post(xp2d, sba_p, conv_p, mem_p, *post_w, 512)
    x1s, routes = _post(xs2d, sba_s, conv_s, mem_s, *post_w, 512)
    x1 = jnp.concatenate([x1p, x1s], axis=0)
    n_tok = x1.shape[0]
    y2 = _moe(x1, _moe_plan(jnp.concatenate([routep, routes], axis=0), n_tok), wgu_b, wd_b)
    g2, b2 = r2(ln2_g[l]), r2(ln2_b[l])
    yp = _combine(x1, y2, g2, b2, 0, bp * tp, 512)
    ys = _combine(x1, y2, g2, b2, bp * tp, bs * ts, 512)

    hd = (SBA_HEADS, HEAD_DIM)
    glup3 = glup.reshape(bp, tp, CONV_W)
    glus3 = glus.reshape(bs, ts, CONV_W)
    conv_tail_s = jnp.concatenate([cache_conv[l], glus3], axis=1)[:, -(CONV_K - 1):]
    return (
        yp.reshape(bp, tp, D_MODEL),
        ys.reshape(bs, ts, D_MODEL),
        kp.reshape(1, bp, tp, *hd),
        vp.reshape(1, bp, tp, *hd),
        glup3[:, -(CONV_K - 1):][None],
        mk.reshape(1, bp, N_MEM, MEM_HEADS, HEAD_DIM),
        mv.reshape(1, bp, N_MEM, MEM_HEADS, HEAD_DIM),
        ks.reshape(1, bs, ts, *hd),
        vs.reshape(1, bs, ts, *hd),
        conv_tail_s[None],
    )
```

```python
import functools

import jax
import jax.numpy as jnp
from jax import lax
from jax.experimental import pallas as pl
from jax.experimental.pallas import tpu as pltpu

F32 = jnp.float32
BF16 = jnp.bfloat16

D_MODEL = 1024
HEAD_DIM = 64
SBA_HEADS = 8
SBA_W = SBA_HEADS * HEAD_DIM
CONV_W = 256
CONV_K = 31
MEM_HEADS = 4
MEM_W = MEM_HEADS * HEAD_DIM
N_MEM = 256
N_GROUPS = 4
EXPERTS_PER_GROUP = 8
N_EXPERTS = N_GROUPS * EXPERTS_PER_GROUP
D_EXPERT = 256
DEPTH = 1
DEEPNORM_ALPHA = (2 * DEPTH) ** 0.25
QK_SCALE = HEAD_DIM ** -0.5
LN_EPS = 1e-5
LOG2E = 1.4426950408889634

LANES = 128
SUBLANES = 8
HALO = 32
MOE_TM = 256
MOE_UNROLL = 8
SP_LINEAR = 100.0
VMEM_LIMIT = 48 * 1024 * 1024


def _cparams(sem):
    return pltpu.CompilerParams(dimension_semantics=sem, vmem_limit_bytes=VMEM_LIMIT)


def _ln(x, g, b):
    mu = jnp.mean(x, axis=-1, keepdims=True)
    xc = x - mu
    var = jnp.mean(xc * xc, axis=-1, keepdims=True)
    return xc * lax.rsqrt(var + LN_EPS) * g + b


def _dot(a, b):
    return jnp.dot(a, b, preferred_element_type=F32)


def _dot_nt(a, b):
    return lax.dot_general(a, b, (((1,), (1,)), ((), ())), preferred_element_type=F32)


def _in_proj_kernel(x_ref, g_ref, b_ref, w_ref, q_ref, kb_ref, vb_ref, k_ref, v_ref, glu_ref, mq_ref):
    xn = _ln(x_ref[...], g_ref[...], b_ref[...]).astype(BF16)
    q = _dot(xn, w_ref[:, 0:SBA_W])
    q_ref[...] = (q * (QK_SCALE * LOG2E)).astype(BF16)
    k = _dot(xn, w_ref[:, SBA_W:2 * SBA_W])
    k_ref[...] = k
    kb_ref[...] = k.astype(BF16)
    v = _dot(xn, w_ref[:, 2 * SBA_W:3 * SBA_W])
    v_ref[...] = v
    vb_ref[...] = v.astype(BF16)
    c0 = 3 * SBA_W
    cv = _dot(xn, w_ref[:, c0:c0 + CONV_W])
    cg = _dot(xn, w_ref[:, c0 + CONV_W:c0 + 2 * CONV_W])
    glu_ref[...] = cv * jax.nn.sigmoid(cg)
    mq = _dot(xn, w_ref[:, c0 + 2 * CONV_W:c0 + 2 * CONV_W + MEM_W])
    mq_ref[...] = (mq * QK_SCALE).astype(BF16)


def _in_proj(x2d, g, b, w_bf16, tm):
    n = x2d.shape[0]
    in_w = w_bf16.shape[1]
    row = lambda w: pl.BlockSpec((tm, w), lambda i: (i, 0))
    full = lambda s: pl.BlockSpec(s, lambda i: (0, 0))
    return pl.pallas_call(
        _in_proj_kernel,
        grid=(n // tm,),
        in_specs=[row(D_MODEL), full((1, D_MODEL)), full((1, D_MODEL)), full((D_MODEL, in_w))],
        out_specs=[row(SBA_W), row(SBA_W), row(SBA_W), row(SBA_W), row(SBA_W), row(CONV_W), row(MEM_W)],
        out_shape=[
            jax.ShapeDtypeStruct((n, SBA_W), BF16),
            jax.ShapeDtypeStruct((n, SBA_W), BF16),
            jax.ShapeDtypeStruct((n, SBA_W), BF16),
            jax.ShapeDtypeStruct((n, SBA_W), F32),
            jax.ShapeDtypeStruct((n, SBA_W), F32),
            jax.ShapeDtypeStruct((n, CONV_W), F32),
            jax.ShapeDtypeStruct((n, MEM_W), BF16),
        ],
        compiler_params=_cparams(("parallel",)),
        name="in_proj",
    )(x2d, g, b, w_bf16)


def _head_masks(shape):
    lane = lax.broadcasted_iota(jnp.int32, shape, 1)
    return lane < HEAD_DIM


def _stack_heads(q, low):
    zero = jnp.zeros_like(q)
    return jnp.concatenate([jnp.where(low, q, zero), jnp.where(low, zero, q)], axis=0)


def _sba_tile(z, u, mask):
    sp = jnp.where(z > SP_LINEAR, z, jnp.log(1.0 + jnp.exp2(z)) * LOG2E)
    if mask is not None:
        sp = jnp.where(mask, sp, 0.0)
    w = jnp.exp2(z - _dot(sp.astype(BF16), u))
    if mask is not None:
        w = jnp.where(mask, w, 0.0)
    return w.astype(BF16), jnp.sum(sp, axis=1, keepdims=True)


def _sba_prompt_kernel(q_ref, k_ref, v_ref, u_ref, o_ref, *, tq, hp):
    qi = pl.program_id(2)
    low = _head_masks((tq, LANES))
    u = u_ref[...]
    row = lax.broadcasted_iota(jnp.int32, (2 * tq, tq), 0) & (tq - 1)
    col = lax.broadcasted_iota(jnp.int32, (2 * tq, tq), 1)
    qs = [_stack_heads(q_ref[:, p * LANES:(p + 1) * LANES], low) for p in range(hp)]

    def blk(j, carry, mask):
        start = pl.multiple_of(j * tq, tq)
        out = []
        for p in range(hp):
            c, acc = carry[p]
            z = _dot_nt(qs[p], k_ref[pl.ds(start, tq), p * LANES:(p + 1) * LANES])
            w, rs = _sba_tile(z, u, mask)
            acc = acc + _dot(w, v_ref[pl.ds(start, tq), p * LANES:(p + 1) * LANES]) * jnp.exp2(-c)
            out.append((c + rs, acc))
        return tuple(out)

    carry = tuple((jnp.zeros((2 * tq, 1), F32), jnp.zeros((2 * tq, LANES), F32)) for _ in range(hp))
    carry = blk(qi, carry, col < row)
    carry = lax.fori_loop(0, qi, lambda t, cr: blk(qi - 1 - t, cr, None), carry)
    for p in range(hp):
        acc = carry[p][1]
        o_ref[:, p * LANES:(p + 1) * LANES] = jnp.where(low, acc[0:tq], acc[tq:2 * tq]).astype(o_ref.dtype)


def _sba_prompt(q, k, v, u, batch, seq, tq, hp):
    n = q.shape[0]
    nq = seq // tq
    groups = SBA_W // (LANES * hp)
    return pl.pallas_call(
        functools.partial(_sba_prompt_kernel, tq=tq, hp=hp),
        grid=(batch, groups, nq),
        in_specs=[
            pl.BlockSpec((tq, hp * LANES), lambda b, p, i: (b * nq + i, p)),
            pl.BlockSpec((seq, hp * LANES), lambda b, p, i: (b, p)),
            pl.BlockSpec((seq, hp * LANES), lambda b, p, i: (b, p)),
            pl.BlockSpec((tq, tq), lambda b, p, i: (0, 0)),
        ],
        out_specs=pl.BlockSpec((tq, hp * LANES), lambda b, p, i: (b * nq + i, p)),
        out_shape=jax.ShapeDtypeStruct((n, SBA_W), BF16),
        compiler_params=_cparams(("parallel", "parallel", "arbitrary")),
        name="sba_prompt",
    )(q, k, v, u)


def _sba_sample_kernel(q_ref, kn_ref, vn_ref, kc_ref, vc_ref, u_ref, un_ref, o_ref, qs_ref, c_ref, acc_ref,
                       *, tn, tk, nsub):
    j = pl.program_id(1)
    nj = pl.num_programs(1)
    low = _head_masks((tn, LANES))
    pairs = SBA_W // LANES
    rows = 2 * tn

    def step(get_k, get_v, u, mask):
        z = jnp.concatenate([_dot_nt(qs_ref[p], get_k(p)) for p in range(pairs)], axis=0)
        w, rs = _sba_tile(z, u, mask)
        pv = jnp.concatenate([_dot(w[p * rows:(p + 1) * rows], get_v(p)) for p in range(pairs)], axis=0)
        c = c_ref[...]
        acc_ref[...] += pv * jnp.exp2(-c)
        c_ref[...] = c + rs

    @pl.when(j == 0)
    def _():
        for p in range(pairs):
            qs_ref[p] = _stack_heads(q_ref[:, p * LANES:(p + 1) * LANES], low)
        c_ref[...] = jnp.zeros_like(c_ref)
        acc_ref[...] = jnp.zeros_like(acc_ref)
        row = lax.broadcasted_iota(jnp.int32, (pairs * rows, tn), 0) & (tn - 1)
        col = lax.broadcasted_iota(jnp.int32, (pairs * rows, tn), 1)
        step(lambda p: kn_ref[:, p * LANES:(p + 1) * LANES], lambda p: vn_ref[:, p * LANES:(p + 1) * LANES],
             un_ref[...], col < row)

    for s in reversed(range(nsub)):
        step(lambda p: kc_ref[0, s * tk:(s + 1) * tk, p * LANES:(p + 1) * LANES],
             lambda p: vc_ref[0, s * tk:(s + 1) * tk, p * LANES:(p + 1) * LANES],
             u_ref[...], None)

    @pl.when(j == nj - 1)
    def _():
        for p in range(pairs):
            a0 = acc_ref[p * rows:p * rows + tn, :]
            a1 = acc_ref[p * rows + tn:(p + 1) * rows, :]
            o_ref[:, p * LANES:(p + 1) * LANES] = jnp.where(low, a0, a1).astype(o_ref.dtype)


def _sba_sample(q, kn, vn, kc, vc, u, un, batch, tn, tk, nsub):
    n = q.shape[0]
    past = kc.shape[1]
    tkb = tk * nsub
    nj = past // tkb
    rows = SBA_HEADS * tn
    return pl.pallas_call(
        functools.partial(_sba_sample_kernel, tn=tn, tk=tk, nsub=nsub),
        grid=(batch, nj),
        in_specs=[
            pl.BlockSpec((tn, SBA_W), lambda b, j: (b, 0)),
            pl.BlockSpec((tn, SBA_W), lambda b, j: (b, 0)),
            pl.BlockSpec((tn, SBA_W), lambda b, j: (b, 0)),
            pl.BlockSpec((1, tkb, SBA_W), lambda b, j: (b, nj - 1 - j, 0)),
            pl.BlockSpec((1, tkb, SBA_W), lambda b, j: (b, nj - 1 - j, 0)),
            pl.BlockSpec((tk, tk), lambda b, j: (0, 0)),
            pl.BlockSpec((tn, tn), lambda b, j: (0, 0)),
        ],
        out_specs=pl.BlockSpec((tn, SBA_W), lambda b, j: (b, 0)),
        out_shape=jax.ShapeDtypeStruct((n, SBA_W), BF16),
        scratch_shapes=[pltpu.VMEM((SBA_W // LANES, 2 * tn, LANES), BF16),
                        pltpu.VMEM((rows, 1), F32), pltpu.VMEM((rows, LANES), F32)],
        compiler_params=_cparams(("parallel", "arbitrary")),
        name="sba_sample",
    )(q, kn, vn, kc, vc, u, un)


def _conv_kernel(h0_ref, glu_ref, wdw_ref, bdw_ref, g_ref, b_ref, wpw_ref, o_ref, buf_ref, *, tt, rc):
    ti = pl.program_id(1)

    @pl.when(ti == 0)
    def _():
        buf_ref[0:HALO, :] = h0_ref[0]

    @pl.when(ti > 0)
    def _():
        buf_ref[0:HALO, :] = buf_ref[tt:tt + HALO, :]

    buf_ref[HALO:HALO + tt, :] = glu_ref[...]
    base = HALO - (CONV_K - 1)
    wdw = wdw_ref[...]
    for r0 in range(0, tt, rc):
        acc = jnp.zeros((rc, CONV_W), F32)
        for kk in range(CONV_K):
            acc = acc + buf_ref[pl.ds(base + r0 + kk, rc), :] * wdw[kk:kk + 1, :]
        u = _ln(acc + bdw_ref[...], g_ref[...], b_ref[...])
        s = u * jax.nn.sigmoid(u)
        o_ref[pl.ds(r0, rc), :] = _dot(s.astype(BF16), wpw_ref[...]).astype(o_ref.dtype)


def _conv_module(h0, glu, w_dw, b_dw, g, b, w_pw_bf16, batch, seq, tt):
    n = glu.shape[0]
    nt = seq // tt
    rc = min(tt, 64)
    full = lambda s: pl.BlockSpec(s, lambda bi, ti: (0,) * len(s))
    return pl.pallas_call(
        functools.partial(_conv_kernel, tt=tt, rc=rc),
        grid=(batch, nt),
        in_specs=[
            pl.BlockSpec((1, HALO, CONV_W), lambda bi, ti: (bi, 0, 0)),
            pl.BlockSpec((tt, CONV_W), lambda bi, ti: (bi * nt + ti, 0)),
            full((CONV_K, CONV_W)), full((1, CONV_W)), full((1, CONV_W)), full((1, CONV_W)),
            full((CONV_W, CONV_W)),
        ],
        out_specs=pl.BlockSpec((tt, CONV_W), lambda bi, ti: (bi * nt + ti, 0)),
        out_shape=jax.ShapeDtypeStruct((n, CONV_W), BF16),
        scratch_shapes=[pltpu.VMEM((HALO + tt, CONV_W), F32)],
        compiler_params=_cparams(("parallel", "arbitrary")),
        name="conv_module",
    )(h0, glu, w_dw, b_dw, g, b, w_pw_bf16)


def _mem_kv_kernel(m_ref, w_ref, k_ref, v_ref):
    m = m_ref[...].astype(BF16)
    k_ref[...] = _dot(m, w_ref[:, 0:MEM_W])
    v_ref[...] = _dot(m, w_ref[:, MEM_W:2 * MEM_W])


def _mem_kv(mem2d, w_kv_bf16, tm):
    n = mem2d.shape[0]
    return pl.pallas_call(
        _mem_kv_kernel,
        grid=(n // tm,),
        in_specs=[pl.BlockSpec((tm, D_MODEL), lambda i: (i, 0)),
                  pl.BlockSpec((D_MODEL, 2 * MEM_W), lambda i: (0, 0))],
        out_specs=[pl.BlockSpec((tm, MEM_W), lambda i: (i, 0))] * 2,
        out_shape=[jax.ShapeDtypeStruct((n, MEM_W), F32)] * 2,
        compiler_params=_cparams(("parallel",)),
        name="mem_kv",
    )(mem2d, w_kv_bf16)


def _mem_attn_kernel(q_ref, k_ref, v_ref, o_ref):
    q = q_ref[...]
    tq = q.shape[0]
    low = _head_masks((tq, LANES))
    for p in range(MEM_W // LANES):
        qp = q[:, p * LANES:(p + 1) * LANES]
        kb = k_ref[0, :, p * LANES:(p + 1) * LANES].astype(BF16)
        vb = v_ref[0, :, p * LANES:(p + 1) * LANES].astype(BF16)
        outs = []
        for h in range(2):
            qm = jnp.where(low if h == 0 else jnp.logical_not(low), qp, jnp.zeros_like(qp))
            s = _dot_nt(qm, kb)
            e = jnp.exp(s - jnp.max(s, axis=-1, keepdims=True))
            outs.append(_dot(e.astype(BF16), vb) / jnp.sum(e, axis=-1, keepdims=True))
        o_ref[:, p * LANES:(p + 1) * LANES] = jnp.where(low, outs[0], outs[1]).astype(o_ref.dtype)


def _mem_attn(mq, mk, mv, batch, seq, tq):
    n = mq.shape[0]
    nq = seq // tq
    return pl.pallas_call(
        _mem_attn_kernel,
        grid=(batch, nq),
        in_specs=[pl.BlockSpec((tq, MEM_W), lambda b, i: (b * nq + i, 0)),
                  pl.BlockSpec((1, N_MEM, MEM_W), lambda b, i: (b, 0, 0)),
                  pl.BlockSpec((1, N_MEM, MEM_W), lambda b, i: (b, 0, 0))],
        out_specs=pl.BlockSpec((tq, MEM_W), lambda b, i: (b * nq + i, 0)),
        out_shape=jax.ShapeDtypeStruct((n, MEM_W), BF16),
        compiler_params=_cparams(("parallel", "parallel")),
        name="mem_attn",
    )(mq, mk, mv)


def _store_token_tiles(ref, x):
    for c in range(SUBLANES):
        ref[:, c, :] = x[:, c * LANES:(c + 1) * LANES]


def _load_token_tiles(ref):
    return jnp.concatenate([ref[:, c, :] for c in range(SUBLANES)], axis=1)


def _split2(x):
    a = x.astype(BF16)
    b = (x - a.astype(F32)).astype(BF16)
    return a, b


def _post_kernel(x_ref, sba_ref, conv_ref, mem_ref, g0_ref, b0_ref, wo_ref, g1_ref, b1_ref,
                 wr_ref, br_ref, x1_ref, route_ref):
    xn = _ln(x_ref[...], g0_ref[...], b0_ref[...])
    mix = _dot(sba_ref[...], wo_ref[0:SBA_W, :])
    mix = mix + _dot(conv_ref[...], wo_ref[SBA_W:SBA_W + CONV_W, :])
    mix = mix + _dot(mem_ref[...], wo_ref[SBA_W + CONV_W:SBA_W + CONV_W + MEM_W, :])
    x1 = _ln(DEEPNORM_ALPHA * xn + mix, g1_ref[...], b1_ref[...])
    _store_token_tiles(x1_ref, x1)

    a0, a1 = _split2(x1)
    w0, w1 = wr_ref[0], wr_ref[1]
    logits = _dot(a0, w0) + (_dot(a0, w1) + _dot(a1, w0)) + br_ref[...]
    tm = logits.shape[0]
    lane = lax.broadcasted_iota(jnp.int32, (tm, LANES), 1).astype(F32)
    neg = jnp.float32(-jnp.inf)
    big = jnp.float32(LANES)
    is_g = jnp.logical_and(lane >= N_EXPERTS, lane < N_EXPERTS + N_GROUPS)
    gl = jnp.where(is_g, logits, neg)
    gmax = jnp.max(gl, axis=-1, keepdims=True)
    g_idx = jnp.min(jnp.where(gl == gmax, lane, big), axis=-1, keepdims=True) - N_EXPERTS
    g_w = 1.0 / jnp.sum(jnp.exp(gl - gmax), axis=-1, keepdims=True)
    in_grp = jnp.logical_and(lane >= g_idx * EXPERTS_PER_GROUP, lane < (g_idx + 1.0) * EXPERTS_PER_GROUP)
    el = jnp.where(in_grp, logits, neg)
    v1 = jnp.max(el, axis=-1, keepdims=True)
    i1 = jnp.min(jnp.where(el == v1, lane, big), axis=-1, keepdims=True)
    el2 = jnp.where(lane == i1, neg, el)
    v2 = jnp.max(el2, axis=-1, keepdims=True)
    i2 = jnp.min(jnp.where(el2 == v2, lane, big), axis=-1, keepdims=True)
    e2 = jnp.exp(v2 - v1)
    p1 = 1.0 / (1.0 + e2)
    p2 = e2 / (1.0 + e2)
    route_ref[...] = jnp.where(lane == 0.0, i1, jnp.where(lane == 1.0, i2, jnp.where(
        lane == 2.0, p1 * g_w, jnp.where(lane == 3.0, p2 * g_w, 0.0))))


def _post(x2d, sba, conv, mem, g0, b0, wo_bf16, g1, b1, wr3, br, tm):
    n = x2d.shape[0]
    row = lambda w: pl.BlockSpec((tm, w), lambda i: (i, 0))
    full = lambda s: pl.BlockSpec(s, lambda i: (0,) * len(s))
    return pl.pallas_call(
        _post_kernel,
        grid=(n // tm,),
        in_specs=[row(D_MODEL), row(SBA_W), row(CONV_W), row(MEM_W),
                  full((1, D_MODEL)), full((1, D_MODEL)), full((D_MODEL, D_MODEL)),
                  full((1, D_MODEL)), full((1, D_MODEL)),
                  full((2, D_MODEL, LANES)), full((1, LANES))],
        out_specs=[pl.BlockSpec((tm, SUBLANES, LANES), lambda i: (i, 0, 0)), row(LANES)],
        out_shape=[jax.ShapeDtypeStruct((n, SUBLANES, LANES), F32),
                   jax.ShapeDtypeStruct((n, LANES), F32)],
        compiler_params=_cparams(("parallel",)),
        name="post",
    )(x2d, sba, conv, mem, g0, b0, wo_bf16, g1, b1, wr3, br)


def _moe_plan(route, n):
    pairs = 2 * n
    tiles = pairs // MOE_TM + N_EXPERTS
    e = jnp.concatenate([route[:, 0], route[:, 1]]).astype(jnp.int32)
    w = jnp.concatenate([route[:, 2], route[:, 3]])
    order = jnp.argsort(e, stable=True).astype(jnp.int32)
    ids = jnp.arange(N_EXPERTS, dtype=jnp.int32)
    counts = jnp.sum((e[:, None] == ids[None, :]).astype(jnp.int32), axis=0)
    ntile = (counts + MOE_TM - 1) // MOE_TM
    tile_end = jnp.cumsum(ntile)
    tile_start = tile_end - ntile
    first = jnp.cumsum(counts) - counts
    n_active = tile_end[-1]
    t = jnp.arange(tiles, dtype=jnp.int32)
    te = jnp.minimum(jnp.sum((t[:, None] >= tile_end[None, :]).astype(jnp.int32), axis=1), N_EXPERTS - 1)
    done = (t - tile_start[te]) * MOE_TM
    n_valid = jnp.where(t < n_active, jnp.clip(counts[te] - done, 0, MOE_TM), 0)
    r = jnp.arange(MOE_TM, dtype=jnp.int32)
    valid = r[None, :] < n_valid[:, None]
    pos = jnp.clip(first[te][:, None] + done[:, None] + r[None, :], 0, pairs - 1)
    pair = order[pos]
    src = jnp.where(valid, jnp.where(pair >= n, pair - n, pair), 0).astype(jnp.int32)
    dst = jnp.where(valid, pair, pairs + r[None, :]).astype(jnp.int32)
    wrow = jnp.where(valid, w[pair], 0.0).reshape(tiles * MOE_TM, 1)
    return (te.astype(jnp.int32), n_active.reshape(1).astype(jnp.int32),
            src.reshape(tiles, 1, MOE_TM), dst.reshape(tiles, 1, MOE_TM), wrow)


def _moe_kernel(te_ref, na_ref, src0_ref, src1_ref, dst_ref, w_ref, x_hbm, wgu_ref, wd_ref,
                y_hbm, xg_ref, yo_ref, gsem, ssem):
    t = pl.program_id(0)
    n_active = na_ref[0]

    def start_gather(src_ref, slot):
        def body(i, carry):
            for k in range(MOE_UNROLL):
                r = i * MOE_UNROLL + k
                pltpu.make_async_copy(x_hbm.at[pl.ds(src_ref[0, 0, r], 1)], xg_ref.at[slot, pl.ds(r, 1)],
                                      gsem.at[slot]).start(priority=k % 2)
            return carry
        lax.fori_loop(0, MOE_TM // MOE_UNROLL, body, 0)

    def wait_gather(slot):
        pltpu.make_async_copy(x_hbm.at[pl.ds(0, MOE_TM)], xg_ref.at[slot], gsem.at[slot]).wait()

    def wait_scatter():
        pltpu.make_async_copy(yo_ref, y_hbm.at[pl.ds(0, MOE_TM)], ssem.at[0]).wait()

    @pl.when(t < n_active)
    def _():
        slot = lax.rem(t, 2)

        @pl.when(t == 0)
        def _():
            start_gather(src0_ref, 0)
            yo_ref[...] = jnp.zeros_like(yo_ref)
            sink = pltpu.make_async_copy(yo_ref, y_hbm.at[pl.ds(y_hbm.shape[0] - MOE_TM, MOE_TM)], ssem.at[0])
            sink.start()
            sink.wait()

        @pl.when(t + 1 < n_active)
        def _():
            start_gather(src1_ref, 1 - slot)

        wait_gather(slot)
        x = _load_token_tiles(xg_ref.at[slot]).astype(BF16)
        a = _dot(x, wgu_ref[0, :, 0:D_EXPERT])
        u = _dot(x, wgu_ref[0, :, D_EXPERT:2 * D_EXPERT])
        hid = (a * jax.nn.sigmoid(a)) * u * w_ref[...]
        y = _dot(hid.astype(BF16), wd_ref[0])

        @pl.when(t > 0)
        def _():
            wait_scatter()

        _store_token_tiles(yo_ref, y)

        def body(i, carry):
            for k in range(MOE_UNROLL):
                r = i * MOE_UNROLL + k
                pltpu.make_async_copy(yo_ref.at[pl.ds(r, 1)], y_hbm.at[pl.ds(dst_ref[0, 0, r], 1)],
                                      ssem.at[0]).start(priority=k % 2)
            return carry
        lax.fori_loop(0, MOE_TM // MOE_UNROLL, body, 0)

        @pl.when(t == n_active - 1)
        def _():
            wait_scatter()


def _moe(x1, plan, wgu_bf16, wd_bf16):
    te, n_active, src, dst, wrow = plan
    n = x1.shape[0]
    tiles = src.shape[0]
    idx = lambda f: pl.BlockSpec((1, 1, MOE_TM), f, memory_space=pltpu.SMEM)
    tile3 = (SUBLANES, LANES)
    grid_spec = pltpu.PrefetchScalarGridSpec(
        num_scalar_prefetch=2,
        grid=(tiles,),
        in_specs=[
            idx(lambda t, te, na: (t, 0, 0)),
            idx(lambda t, te, na: (jnp.minimum(t + 1, tiles - 1), 0, 0)),
            idx(lambda t, te, na: (t, 0, 0)),
            pl.BlockSpec((MOE_TM, 1), lambda t, te, na: (t, 0)),
            pl.BlockSpec(memory_space=pl.ANY),
            pl.BlockSpec((1, D_MODEL, 2 * D_EXPERT), lambda t, te, na: (te[t], 0, 0)),
            pl.BlockSpec((1, D_EXPERT, D_MODEL), lambda t, te, na: (te[t], 0, 0)),
        ],
        out_specs=pl.BlockSpec(memory_space=pl.ANY),
        scratch_shapes=[pltpu.VMEM((2, MOE_TM) + tile3, F32), pltpu.VMEM((MOE_TM,) + tile3, F32),
                        pltpu.SemaphoreType.DMA((2,)), pltpu.SemaphoreType.DMA((1,))],
    )
    return pl.pallas_call(
        _moe_kernel,
        grid_spec=grid_spec,
        out_shape=jax.ShapeDtypeStruct((2 * n + MOE_TM,) + tile3, F32),
        compiler_params=_cparams(("arbitrary",)),
        name="moe",
    )(te, n_active, src, src, dst, wrow, x1, wgu_bf16, wd_bf16)


def _combine_kernel(x1_ref, ya_ref, yb_ref, g_ref, b_ref, o_ref):
    f = _load_token_tiles(ya_ref) + _load_token_tiles(yb_ref)
    o_ref[...] = _ln(DEEPNORM_ALPHA * _load_token_tiles(x1_ref) + f, g_ref[...], b_ref[...])


def _combine(x1, y2, g2, b2, row0, rows, tm):
    n = x1.shape[0]
    b0 = row0 // tm
    row = lambda off: pl.BlockSpec((tm, SUBLANES, LANES), lambda i: (i + off, 0, 0))
    full = pl.BlockSpec((1, D_MODEL), lambda i: (0, 0))
    return pl.pallas_call(
        _combine_kernel,
        grid=(rows // tm,),
        in_specs=[row(b0), row(b0), row(b0 + n // tm), full, full],
        out_specs=pl.BlockSpec((tm, D_MODEL), lambda i: (i, 0)),
        out_shape=jax.ShapeDtypeStruct((rows, D_MODEL), F32),
        compiler_params=_cparams(("parallel",)),
        name="combine",
    )(x1, y2, y2, g2, b2)


def _later_or_same(n):
    r = lax.broadcasted_iota(jnp.int32, (n, n), 0)
    c = lax.broadcasted_iota(jnp.int32, (n, n), 1)
    return (r >= c).astype(BF16)


def kernel(x_prompt, x_sample, mem_prompt, cache_sba_k, cache_sba_v, cache_conv, cache_mem_k, cache_mem_v,
           ln0_g, ln0_b, w_in, w_dw, b_dw, lnc_g, lnc_b, w_cpw, w_mk, w_mv, w_out, ln1_g, ln1_b,
           w_rg, b_rg, w_re, b_re, w_eg, w_eu, w_ed, ln2_g, ln2_b):
    bp, tp, _ = x_prompt.shape
    bs, ts, _ = x_sample.shape
    past = cache_sba_k.shape[2]
    l = 0
    r2 = lambda a: a.reshape(1, -1)

    w_in_b = w_in[l].astype(BF16)
    w_kv_b = jnp.concatenate([w_mk[l], w_mv[l]], axis=1).astype(BF16)
    w_out_b = w_out[l].astype(BF16)
    w_cpw_b = w_cpw[l].astype(BF16)
    wr = jnp.zeros((D_MODEL, LANES), F32)
    wr = wr.at[:, 0:N_EXPERTS].set(w_re[l]).at[:, N_EXPERTS:N_EXPERTS + N_GROUPS].set(w_rg[l])
    wr0 = wr.astype(BF16)
    wr1 = (wr - wr0.astype(F32)).astype(BF16)
    wr3 = jnp.stack([wr0, wr1])
    br = jnp.zeros((1, LANES), F32)
    br = br.at[0, 0:N_EXPERTS].set(b_re[l]).at[0, N_EXPERTS:N_EXPERTS + N_GROUPS].set(b_rg[l])
    wgu_b = jnp.concatenate([w_eg[l], w_eu[l]], axis=-1).astype(BF16)
    wd_b = w_ed[l].astype(BF16)
    g0, b0 = r2(ln0_g), r2(ln0_b)

    tq = 256
    tk_s = 256
    u_p = _later_or_same(tq)
    u_s = _later_or_same(tk_s)
    u_n = _later_or_same(ts)

    def group(x, batch, seq, tm):
        x2d = x.reshape(batch * seq, D_MODEL)
        return x2d, _in_proj(x2d, g0, b0, w_in_b, tm)

    xp2d, (qp, kbp, vbp, kp, vp, glup, mqp) = group(x_prompt, bp, tp, 512)
    xs2d, (qs, kbs, vbs, ks, vs, glus, mqs) = group(x_sample, bs, ts, 512)

    sba_p = _sba_prompt(qp, kbp, vbp, u_p, bp, tp, tq, 4)
    kc = cache_sba_k[l].reshape(bs, past, SBA_W).astype(BF16)
    vc = cache_sba_v[l].reshape(bs, past, SBA_W).astype(BF16)
    sba_s = _sba_sample(qs, kbs, vbs, kc, vc, u_s, u_n, bs, ts, tk_s, 4)

    conv_w = (w_dw[l], r2(b_dw[l]), r2(lnc_g[l]), r2(lnc_b[l]), w_cpw_b)
    h0_p = jnp.zeros((bp, HALO, CONV_W), F32)
    h0_s = jnp.pad(cache_conv[l], ((0, 0), (HALO - (CONV_K - 1), 0), (0, 0)))
    conv_p = _conv_module(h0_p, glup, *conv_w, bp, tp, 256)
    conv_s = _conv_module(h0_s, glus, *conv_w, bs, ts, ts)

    mk, mv = _mem_kv(mem_prompt.reshape(bp * N_MEM, D_MODEL), w_kv_b, 512)
    mem_p = _mem_attn(mqp, mk.reshape(bp, N_MEM, MEM_W), mv.reshape(bp, N_MEM, MEM_W), bp, tp, 512)
    mem_s = _mem_attn(mqs, cache_mem_k[l].reshape(bs, N_MEM, MEM_W),
                      cache_mem_v[l].reshape(bs, N_MEM, MEM_W), bs, ts, ts)

    post_w = (g0, b0, w_out_b, r2(ln1_g[l]), r2(ln1_b[l]), wr3, br)
    x1p, routep = _post(xp2d, sba_p, conv_p, mem_p, *post_w, 512)
    x1s, routes = _post(xs2d, sba_s, conv_s, mem_s, *post_w, 512)
    x1 = jnp.concatenate([x1p, x1s], axis=0)
    n_tok = x1.shape[0]
    y2 = _moe(x1, _moe_plan(jnp.concatenate([routep, routes], axis=0), n_tok), wgu_b, wd_b)
    g2, b2 = r2(ln2_g[l]), r2(ln2_b[l])
    yp = _combine(x1, y2, g2, b2, 0, bp * tp, 512)
    ys = _combine(x1, y2, g2, b2, bp * tp, bs * ts, 512)

    hd = (SBA_HEADS, HEAD_DIM)
    glup3 = glup.reshape(bp, tp, CONV_W)
    glus3 = glus.reshape(bs, ts, CONV_W)
    conv_tail_s = jnp.concatenate([cache_conv[l], glus3], axis=1)[:, -(CONV_K - 1):]
    return (
        yp.reshape(bp, tp, D_MODEL),
        ys.reshape(bs, ts, D_MODEL),
        kp.reshape(1, bp, tp, *hd),
        vp.reshape(1, bp, tp, *hd),
        glup3[:, -(CONV_K - 1):][None],
        mk.reshape(1, bp, N_MEM, MEM_HEADS, HEAD_DIM),
        mv.reshape(1, bp, N_MEM, MEM_HEADS, HEAD_DIM),
        ks.reshape(1, bs, ts, *hd),
        vs.reshape(1, bs, ts, *hd),
        conv_tail_s[None],
    )
```

```python
import functools

import jax
import jax.numpy as jnp
from jax import lax
from jax.experimental import pallas as pl
from jax.experimental.pallas import tpu as pltpu

F32 = jnp.float32
BF16 = jnp.bfloat16

D_MODEL = 1024
HEAD_DIM = 64
SBA_HEADS = 8
SBA_W = SBA_HEADS * HEAD_DIM
CONV_W = 256
CONV_K = 31
MEM_HEADS = 4
MEM_W = MEM_HEADS * HEAD_DIM
N_MEM = 256
N_GROUPS = 4
EXPERTS_PER_GROUP = 8
N_EXPERTS = N_GROUPS * EXPERTS_PER_GROUP
D_EXPERT = 256
DEPTH = 1
DEEPNORM_ALPHA = (2 * DEPTH) ** 0.25
QK_SCALE = HEAD_DIM ** -0.5
LN_EPS = 1e-5
LOG2E = 1.4426950408889634

LANES = 128
SUBLANES = 8
HALO = 32
MOE_TM = 256
MOE_UNROLL = 8
SP_LINEAR = 100.0
VMEM_LIMIT = 48 * 1024 * 1024


def _cparams(sem):
    return pltpu.CompilerParams(dimension_semantics=sem, vmem_limit_bytes=VMEM_LIMIT)


def _ln(x, g, b):
    mu = jnp.mean(x, axis=-1, keepdims=True)
    xc = x - mu
    var = jnp.mean(xc * xc, axis=-1, keepdims=True)
    return xc * lax.rsqrt(var + LN_EPS) * g + b


def _dot(a, b):
    return jnp.dot(a, b, preferred_element_type=F32)


def _dot_nt(a, b):
    return lax.dot_general(a, b, (((1,), (1,)), ((), ())), preferred_element_type=F32)


def _store_head_rows(ref, x):
    rows = x.shape[0]
    for h in range(SBA_HEADS):
        ref[pl.ds(h, rows, stride=SBA_HEADS), :] = x[:, h * HEAD_DIM:(h + 1) * HEAD_DIM]


def _in_proj_kernel(x_ref, g_ref, b_ref, w_ref, q_ref, kb_ref, vb_ref, k_ref, v_ref, glu_ref, mq_ref):
    xn = _ln(x_ref[...], g_ref[...], b_ref[...]).astype(BF16)
    q = _dot(xn, w_ref[:, 0:SBA_W])
    q_ref[...] = (q * (QK_SCALE * LOG2E)).astype(BF16)
    k = _dot(xn, w_ref[:, SBA_W:2 * SBA_W])
    _store_head_rows(k_ref, k)
    kb_ref[...] = k.astype(BF16)
    v = _dot(xn, w_ref[:, 2 * SBA_W:3 * SBA_W])
    _store_head_rows(v_ref, v)
    vb_ref[...] = v.astype(BF16)
    c0 = 3 * SBA_W
    cv = _dot(xn, w_ref[:, c0:c0 + CONV_W])
    cg = _dot(xn, w_ref[:, c0 + CONV_W:c0 + 2 * CONV_W])
    glu_ref[...] = cv * jax.nn.sigmoid(cg)
    mq = _dot(xn, w_ref[:, c0 + 2 * CONV_W:c0 + 2 * CONV_W + MEM_W])
    mq_ref[...] = (mq * QK_SCALE).astype(BF16)


def _in_proj(x2d, g, b, w_bf16, tm):
    n = x2d.shape[0]
    in_w = w_bf16.shape[1]
    row = lambda w: pl.BlockSpec((tm, w), lambda i: (i, 0))
    full = lambda s: pl.BlockSpec(s, lambda i: (0, 0))
    heads = pl.BlockSpec((tm * SBA_HEADS, HEAD_DIM), lambda i: (i, 0))
    return pl.pallas_call(
        _in_proj_kernel,
        grid=(n // tm,),
        in_specs=[row(D_MODEL), full((1, D_MODEL)), full((1, D_MODEL)), full((D_MODEL, in_w))],
        out_specs=[row(SBA_W), row(SBA_W), row(SBA_W), heads, heads, row(CONV_W), row(MEM_W)],
        out_shape=[
            jax.ShapeDtypeStruct((n, SBA_W), BF16),
            jax.ShapeDtypeStruct((n, SBA_W), BF16),
            jax.ShapeDtypeStruct((n, SBA_W), BF16),
            jax.ShapeDtypeStruct((n * SBA_HEADS, HEAD_DIM), F32),
            jax.ShapeDtypeStruct((n * SBA_HEADS, HEAD_DIM), F32),
            jax.ShapeDtypeStruct((n, CONV_W), F32),
            jax.ShapeDtypeStruct((n, MEM_W), BF16),
        ],
        compiler_params=_cparams(("parallel",)),
        name="in_proj",
    )(x2d, g, b, w_bf16)


def _head_masks(shape):
    lane = lax.broadcasted_iota(jnp.int32, shape, 1)
    return lane < HEAD_DIM


def _stack_heads(q, low):
    zero = jnp.zeros_like(q)
    return jnp.concatenate([jnp.where(low, q, zero), jnp.where(low, zero, q)], axis=0)


def _sba_tile(z, u, mask):
    sp = jnp.where(z > SP_LINEAR, z, jnp.log(1.0 + jnp.exp2(z)) * LOG2E)
    if mask is not None:
        sp = jnp.where(mask, sp, 0.0)
    w = jnp.exp2(z - _dot(sp.astype(BF16), u))
    if mask is not None:
        w = jnp.where(mask, w, 0.0)
    return w.astype(BF16), jnp.sum(sp, axis=1, keepdims=True)


def _sba_prompt_kernel(q_ref, k_ref, v_ref, u_ref, o_ref, *, tq, hp):
    qi = pl.program_id(2)
    low = _head_masks((tq, LANES))
    u = u_ref[...]
    row = lax.broadcasted_iota(jnp.int32, (2 * tq, tq), 0) & (tq - 1)
    col = lax.broadcasted_iota(jnp.int32, (2 * tq, tq), 1)
    qs = [_stack_heads(q_ref[:, p * LANES:(p + 1) * LANES], low) for p in range(hp)]

    def blk(j, carry, mask):
        start = pl.multiple_of(j * tq, tq)
        out = []
        for p in range(hp):
            c, acc = carry[p]
            z = _dot_nt(qs[p], k_ref[pl.ds(start, tq), p * LANES:(p + 1) * LANES])
            w, rs = _sba_tile(z, u, mask)
            acc = acc + _dot(w, v_ref[pl.ds(start, tq), p * LANES:(p + 1) * LANES]) * jnp.exp2(-c)
            out.append((c + rs, acc))
        return tuple(out)

    carry = tuple((jnp.zeros((2 * tq, 1), F32), jnp.zeros((2 * tq, LANES), F32)) for _ in range(hp))
    carry = blk(qi, carry, col < row)
    carry = lax.fori_loop(0, qi, lambda t, cr: blk(qi - 1 - t, cr, None), carry)
    for p in range(hp):
        acc = carry[p][1]
        o_ref[:, p * LANES:(p + 1) * LANES] = jnp.where(low, acc[0:tq], acc[tq:2 * tq]).astype(o_ref.dtype)


def _sba_prompt(q, k, v, u, batch, seq, tq, hp):
    n = q.shape[0]
    nq = seq // tq
    groups = SBA_W // (LANES * hp)
    return pl.pallas_call(
        functools.partial(_sba_prompt_kernel, tq=tq, hp=hp),
        grid=(batch, groups, nq),
        in_specs=[
            pl.BlockSpec((tq, hp * LANES), lambda b, p, i: (b * nq + i, p)),
            pl.BlockSpec((seq, hp * LANES), lambda b, p, i: (b, p)),
            pl.BlockSpec((seq, hp * LANES), lambda b, p, i: (b, p)),
            pl.BlockSpec((tq, tq), lambda b, p, i: (0, 0)),
        ],
        out_specs=pl.BlockSpec((tq, hp * LANES), lambda b, p, i: (b * nq + i, p)),
        out_shape=jax.ShapeDtypeStruct((n, SBA_W), BF16),
        compiler_params=_cparams(("parallel", "parallel", "arbitrary")),
        name="sba_prompt",
    )(q, k, v, u)


def _sba_sample_kernel(q_ref, kn_ref, vn_ref, kc_ref, vc_ref, u_ref, un_ref, o_ref, qh_ref, c_ref, acc_ref,
                       *, tn, tk, nsub):
    j = pl.program_id(1)
    nj = pl.num_programs(1)
    nh = SBA_HEADS

    def step(get_k, get_v, u, mask):
        z = jnp.concatenate([_dot_nt(qh_ref[h], get_k(h)) for h in range(nh)], axis=0)
        w, rs = _sba_tile(z, u, mask)
        pv = jnp.concatenate([_dot(w[h * tn:(h + 1) * tn], get_v(h)) for h in range(nh)], axis=0)
        c = c_ref[...]
        acc_ref[...] += pv * jnp.exp2(-c)
        c_ref[...] = c + rs

    def head(ref, h):
        return ref[:, h * HEAD_DIM:(h + 1) * HEAD_DIM]

    @pl.when(j == 0)
    def _():
        for h in range(nh):
            qh_ref[h] = head(q_ref, h)
        c_ref[...] = jnp.zeros_like(c_ref)
        acc_ref[...] = jnp.zeros_like(acc_ref)
        row = lax.broadcasted_iota(jnp.int32, (nh * tn, tn), 0) & (tn - 1)
        col = lax.broadcasted_iota(jnp.int32, (nh * tn, tn), 1)
        step(lambda h: head(kn_ref, h), lambda h: head(vn_ref, h), un_ref[...], col < row)

    for s in reversed(range(nsub)):
        rows = lambda h: pl.ds(s * tk * nh + h, tk, stride=nh)
        step(lambda h: kc_ref[rows(h), :].astype(BF16), lambda h: vc_ref[rows(h), :].astype(BF16),
             u_ref[...], None)

    @pl.when(j == nj - 1)
    def _():
        o_ref[...] = jnp.concatenate([acc_ref[h * tn:(h + 1) * tn, :] for h in range(nh)],
                                     axis=1).astype(o_ref.dtype)


def _sba_sample(q, kn, vn, kc, vc, u, un, batch, past, tn, tk, nsub):
    n = q.shape[0]
    tkb = tk * nsub
    nj = past // tkb
    rows = SBA_HEADS * tn
    cache = pl.BlockSpec((tkb * SBA_HEADS, HEAD_DIM), lambda b, j: (b * nj + nj - 1 - j, 0))
    return pl.pallas_call(
        functools.partial(_sba_sample_kernel, tn=tn, tk=tk, nsub=nsub),
        grid=(batch, nj),
        in_specs=[
            pl.BlockSpec((tn, SBA_W), lambda b, j: (b, 0)),
            pl.BlockSpec((tn, SBA_W), lambda b, j: (b, 0)),
            pl.BlockSpec((tn, SBA_W), lambda b, j: (b, 0)),
            cache, cache,
            pl.BlockSpec((tk, tk), lambda b, j: (0, 0)),
            pl.BlockSpec((tn, tn), lambda b, j: (0, 0)),
        ],
        out_specs=pl.BlockSpec((tn, SBA_W), lambda b, j: (b, 0)),
        out_shape=jax.ShapeDtypeStruct((n, SBA_W), BF16),
        scratch_shapes=[pltpu.VMEM((SBA_HEADS, tn, HEAD_DIM), BF16),
                        pltpu.VMEM((rows, 1), F32), pltpu.VMEM((rows, HEAD_DIM), F32)],
        compiler_params=_cparams(("parallel", "arbitrary")),
        name="sba_sample",
    )(q, kn, vn, kc, vc, u, un)


def _conv_kernel(h0_ref, glu_ref, wdw_ref, bdw_ref, g_ref, b_ref, wpw_ref, o_ref, buf_ref, *, tt, rc):
    ti = pl.program_id(1)

    @pl.when(ti == 0)
    def _():
        buf_ref[0:HALO, :] = h0_ref[0]

    @pl.when(ti > 0)
    def _():
        buf_ref[0:HALO, :] = buf_ref[tt:tt + HALO, :]

    buf_ref[HALO:HALO + tt, :] = glu_ref[...]
    base = HALO - (CONV_K - 1)
    wdw = wdw_ref[...]
    for r0 in range(0, tt, rc):
        acc = jnp.zeros((rc, CONV_W), F32)
        for kk in range(CONV_K):
            acc = acc + buf_ref[pl.ds(base + r0 + kk, rc), :] * wdw[kk:kk + 1, :]
        u = _ln(acc + bdw_ref[...], g_ref[...], b_ref[...])
        s = u * jax.nn.sigmoid(u)
        o_ref[pl.ds(r0, rc), :] = _dot(s.astype(BF16), wpw_ref[...]).astype(o_ref.dtype)


def _conv_module(h0, glu, w_dw, b_dw, g, b, w_pw_bf16, batch, seq, tt):
    n = glu.shape[0]
    nt = seq // tt
    rc = min(tt, 64)
    full = lambda s: pl.BlockSpec(s, lambda bi, ti: (0,) * len(s))
    return pl.pallas_call(
        functools.partial(_conv_kernel, tt=tt, rc=rc),
        grid=(batch, nt),
        in_specs=[
            pl.BlockSpec((1, HALO, CONV_W), lambda bi, ti: (bi, 0, 0)),
            pl.BlockSpec((tt, CONV_W), lambda bi, ti: (bi * nt + ti, 0)),
            full((CONV_K, CONV_W)), full((1, CONV_W)), full((1, CONV_W)), full((1, CONV_W)),
            full((CONV_W, CONV_W)),
        ],
        out_specs=pl.BlockSpec((tt, CONV_W), lambda bi, ti: (bi * nt + ti, 0)),
        out_shape=jax.ShapeDtypeStruct((n, CONV_W), BF16),
        scratch_shapes=[pltpu.VMEM((HALO + tt, CONV_W), F32)],
        compiler_params=_cparams(("parallel", "arbitrary")),
        name="conv_module",
    )(h0, glu, w_dw, b_dw, g, b, w_pw_bf16)


def _mem_kv_kernel(m_ref, w_ref, k_ref, v_ref):
    m = m_ref[...].astype(BF16)
    k_ref[...] = _dot(m, w_ref[:, 0:MEM_W])
    v_ref[...] = _dot(m, w_ref[:, MEM_W:2 * MEM_W])


def _mem_kv(mem2d, w_kv_bf16, tm):
    n = mem2d.shape[0]
    return pl.pallas_call(
        _mem_kv_kernel,
        grid=(n // tm,),
        in_specs=[pl.BlockSpec((tm, D_MODEL), lambda i: (i, 0)),
                  pl.BlockSpec((D_MODEL, 2 * MEM_W), lambda i: (0, 0))],
        out_specs=[pl.BlockSpec((tm, MEM_W), lambda i: (i, 0))] * 2,
        out_shape=[jax.ShapeDtypeStruct((n, MEM_W), F32)] * 2,
        compiler_params=_cparams(("parallel",)),
        name="mem_kv",
    )(mem2d, w_kv_bf16)


def _mem_attn_kernel(q_ref, k_ref, v_ref, o_ref):
    q = q_ref[...]
    tq = q.shape[0]
    low = _head_masks((tq, LANES))
    for p in range(MEM_W // LANES):
        qp = q[:, p * LANES:(p + 1) * LANES]
        kb = k_ref[0, :, p * LANES:(p + 1) * LANES].astype(BF16)
        vb = v_ref[0, :, p * LANES:(p + 1) * LANES].astype(BF16)
        outs = []
        for h in range(2):
            qm = jnp.where(low if h == 0 else jnp.logical_not(low), qp, jnp.zeros_like(qp))
            s = _dot_nt(qm, kb)
            e = jnp.exp(s - jnp.max(s, axis=-1, keepdims=True))
            outs.append(_dot(e.astype(BF16), vb) / jnp.sum(e, axis=-1, keepdims=True))
        o_ref[:, p * LANES:(p + 1) * LANES] = jnp.where(low, outs[0], outs[1]).astype(o_ref.dtype)


def _mem_attn(mq, mk, mv, batch, seq, tq):
    n = mq.shape[0]
    nq = seq // tq
    return pl.pallas_call(
        _mem_attn_kernel,
        grid=(batch, nq),
        in_specs=[pl.BlockSpec((tq, MEM_W), lambda b, i: (b * nq + i, 0)),
                  pl.BlockSpec((1, N_MEM, MEM_W), lambda b, i: (b, 0, 0)),
                  pl.BlockSpec((1, N_MEM, MEM_W), lambda b, i: (b, 0, 0))],
        out_specs=pl.BlockSpec((tq, MEM_W), lambda b, i: (b * nq + i, 0)),
        out_shape=jax.ShapeDtypeStruct((n, MEM_W), BF16),
        compiler_params=_cparams(("parallel", "parallel")),
        name="mem_attn",
    )(mq, mk, mv)


def _store_token_tiles(ref, x):
    for c in range(SUBLANES):
        ref[:, c, :] = x[:, c * LANES:(c + 1) * LANES]


def _load_token_tiles(ref):
    return jnp.concatenate([ref[:, c, :] for c in range(SUBLANES)], axis=1)


def _split2(x):
    a = x.astype(BF16)
    b = (x - a.astype(F32)).astype(BF16)
    return a, b


def _post_kernel(x_ref, sba_ref, conv_ref, mem_ref, g0_ref, b0_ref, wo_ref, g1_ref, b1_ref,
                 wr_ref, br_ref, x1_ref, route_ref):
    xn = _ln(x_ref[...], g0_ref[...], b0_ref[...])
    mix = _dot(sba_ref[...], wo_ref[0:SBA_W, :])
    mix = mix + _dot(conv_ref[...], wo_ref[SBA_W:SBA_W + CONV_W, :])
    mix = mix + _dot(mem_ref[...], wo_ref[SBA_W + CONV_W:SBA_W + CONV_W + MEM_W, :])
    x1 = _ln(DEEPNORM_ALPHA * xn + mix, g1_ref[...], b1_ref[...])
    _store_token_tiles(x1_ref, x1)

    a0, a1 = _split2(x1)
    w0, w1 = wr_ref[0], wr_ref[1]
    logits = _dot(a0, w0) + (_dot(a0, w1) + _dot(a1, w0)) + br_ref[...]
    tm = logits.shape[0]
    lane = lax.broadcasted_iota(jnp.int32, (tm, LANES), 1).astype(F32)
    neg = jnp.float32(-jnp.inf)
    big = jnp.float32(LANES)
    is_g = jnp.logical_and(lane >= N_EXPERTS, lane < N_EXPERTS + N_GROUPS)
    gl = jnp.where(is_g, logits, neg)
    gmax = jnp.max(gl, axis=-1, keepdims=True)
    g_idx = jnp.min(jnp.where(gl == gmax, lane, big), axis=-1, keepdims=True) - N_EXPERTS
    g_w = 1.0 / jnp.sum(jnp.exp(gl - gmax), axis=-1, keepdims=True)
    in_grp = jnp.logical_and(lane >= g_idx * EXPERTS_PER_GROUP, lane < (g_idx + 1.0) * EXPERTS_PER_GROUP)
    el = jnp.where(in_grp, logits, neg)
    v1 = jnp.max(el, axis=-1, keepdims=True)
    i1 = jnp.min(jnp.where(el == v1, lane, big), axis=-1, keepdims=True)
    el2 = jnp.where(lane == i1, neg, el)
    v2 = jnp.max(el2, axis=-1, keepdims=True)
    i2 = jnp.min(jnp.where(el2 == v2, lane, big), axis=-1, keepdims=True)
    e2 = jnp.exp(v2 - v1)
    p1 = 1.0 / (1.0 + e2)
    p2 = e2 / (1.0 + e2)
    route_ref[...] = jnp.where(lane == 0.0, i1, jnp.where(lane == 1.0, i2, jnp.where(
        lane == 2.0, p1 * g_w, jnp.where(lane == 3.0, p2 * g_w, 0.0))))


def _post(x2d, sba, conv, mem, g0, b0, wo_bf16, g1, b1, wr3, br, tm):
    n = x2d.shape[0]
    row = lambda w: pl.BlockSpec((tm, w), lambda i: (i, 0))
    full = lambda s: pl.BlockSpec(s, lambda i: (0,) * len(s))
    return pl.pallas_call(
        _post_kernel,
        grid=(n // tm,),
        in_specs=[row(D_MODEL), row(SBA_W), row(CONV_W), row(MEM_W),
                  full((1, D_MODEL)), full((1, D_MODEL)), full((D_MODEL, D_MODEL)),
                  full((1, D_MODEL)), full((1, D_MODEL)),
                  full((2, D_MODEL, LANES)), full((1, LANES))],
        out_specs=[pl.BlockSpec((tm, SUBLANES, LANES), lambda i: (i, 0, 0)), row(LANES)],
        out_shape=[jax.ShapeDtypeStruct((n, SUBLANES, LANES), F32),
                   jax.ShapeDtypeStruct((n, LANES), F32)],
        compiler_params=_cparams(("parallel",)),
        name="post",
    )(x2d, sba, conv, mem, g0, b0, wo_bf16, g1, b1, wr3, br)


def _moe_plan(route, n):
    pairs = 2 * n
    tiles = pairs // MOE_TM + N_EXPERTS
    e = jnp.concatenate([route[:, 0], route[:, 1]]).astype(jnp.int32)
    w = jnp.concatenate([route[:, 2], route[:, 3]])
    order = jnp.argsort(e, stable=True).astype(jnp.int32)
    ids = jnp.arange(N_EXPERTS, dtype=jnp.int32)
    counts = jnp.sum((e[:, None] == ids[None, :]).astype(jnp.int32), axis=0)
    ntile = (counts + MOE_TM - 1) // MOE_TM
    tile_end = jnp.cumsum(ntile)
    tile_start = tile_end - ntile
    first = jnp.cumsum(counts) - counts
    n_active = tile_end[-1]
    t = jnp.arange(tiles, dtype=jnp.int32)
    te = jnp.minimum(jnp.sum((t[:, None] >= tile_end[None, :]).astype(jnp.int32), axis=1), N_EXPERTS - 1)
    done = (t - tile_start[te]) * MOE_TM
    n_valid = jnp.where(t < n_active, jnp.clip(counts[te] - done, 0, MOE_TM), 0)
    r = jnp.arange(MOE_TM, dtype=jnp.int32)
    valid = r[None, :] < n_valid[:, None]
    pos = jnp.clip(first[te][:, None] + done[:, None] + r[None, :], 0, pairs - 1)
    pair = order[pos]
    src = jnp.where(valid, jnp.where(pair >= n, pair - n, pair), 0).astype(jnp.int32)
    dst = jnp.where(valid, pair, pairs + r[None, :]).astype(jnp.int32)
    wrow = jnp.where(valid, w[pair], 0.0).reshape(tiles * MOE_TM, 1)
    return (te.astype(jnp.int32), n_active.reshape(1).astype(jnp.int32),
            src.reshape(tiles, 1, MOE_TM), dst.reshape(tiles, 1, MOE_TM), wrow)


def _moe_kernel(te_ref, na_ref, src0_ref, src1_ref, dst_ref, w_ref, x_hbm, wgu_ref, wd_ref,
                y_hbm, xg_ref, yo_ref, gsem, ssem):
    t = pl.program_id(0)
    n_active = na_ref[0]

    def start_gather(src_ref, slot):
        def body(i, carry):
            for k in range(MOE_UNROLL):
                r = i * MOE_UNROLL + k
                pltpu.make_async_copy(x_hbm.at[pl.ds(src_ref[0, 0, r], 1)], xg_ref.at[slot, pl.ds(r, 1)],
                                      gsem.at[slot]).start(priority=k % 2)
            return carry
        lax.fori_loop(0, MOE_TM // MOE_UNROLL, body, 0)

    def wait_gather(slot):
        pltpu.make_async_copy(x_hbm.at[pl.ds(0, MOE_TM)], xg_ref.at[slot], gsem.at[slot]).wait()

    def wait_scatter():
        pltpu.make_async_copy(yo_ref, y_hbm.at[pl.ds(0, MOE_TM)], ssem.at[0]).wait()

    @pl.when(t < n_active)
    def _():
        slot = lax.rem(t, 2)

        @pl.when(t == 0)
        def _():
            start_gather(src0_ref, 0)
            yo_ref[...] = jnp.zeros_like(yo_ref)
            sink = pltpu.make_async_copy(yo_ref, y_hbm.at[pl.ds(y_hbm.shape[0] - MOE_TM, MOE_TM)], ssem.at[0])
            sink.start()
            sink.wait()

        @pl.when(t + 1 < n_active)
        def _():
            start_gather(src1_ref, 1 - slot)

        wait_gather(slot)
        x = _load_token_tiles(xg_ref.at[slot]).astype(BF16)
        a = _dot(x, wgu_ref[0, :, 0:D_EXPERT])
        u = _dot(x, wgu_ref[0, :, D_EXPERT:2 * D_EXPERT])
        hid = (a * jax.nn.sigmoid(a)) * u * w_ref[...]
        y = _dot(hid.astype(BF16), wd_ref[0])

        @pl.when(t > 0)
        def _():
            wait_scatter()

        _store_token_tiles(yo_ref, y)

        def body(i, carry):
            for k in range(MOE_UNROLL):
                r = i * MOE_UNROLL + k
                pltpu.make_async_copy(yo_ref.at[pl.ds(r, 1)], y_hbm.at[pl.ds(dst_ref[0, 0, r], 1)],
                                      ssem.at[0]).start(priority=k % 2)
            return carry
        lax.fori_loop(0, MOE_TM // MOE_UNROLL, body, 0)

        @pl.when(t == n_active - 1)
        def _():
            wait_scatter()


def _moe(x1, plan, wgu_bf16, wd_bf16):
    te, n_active, src, dst, wrow = plan
    n = x1.shape[0]
    tiles = src.shape[0]
    idx = lambda f: pl.BlockSpec((1, 1, MOE_TM), f, memory_space=pltpu.SMEM)
    tile3 = (SUBLANES, LANES)
    grid_spec = pltpu.PrefetchScalarGridSpec(
        num_scalar_prefetch=2,
        grid=(tiles,),
        in_specs=[
            idx(lambda t, te, na: (t, 0, 0)),
            idx(lambda t, te, na: (jnp.minimum(t + 1, tiles - 1), 0, 0)),
            idx(lambda t, te, na: (t, 0, 0)),
            pl.BlockSpec((MOE_TM, 1), lambda t, te, na: (t, 0)),
            pl.BlockSpec(memory_space=pl.ANY),
            pl.BlockSpec((1, D_MODEL, 2 * D_EXPERT), lambda t, te, na: (te[t], 0, 0)),
            pl.BlockSpec((1, D_EXPERT, D_MODEL), lambda t, te, na: (te[t], 0, 0)),
        ],
        out_specs=pl.BlockSpec(memory_space=pl.ANY),
        scratch_shapes=[pltpu.VMEM((2, MOE_TM) + tile3, F32), pltpu.VMEM((MOE_TM,) + tile3, F32),
                        pltpu.SemaphoreType.DMA((2,)), pltpu.SemaphoreType.DMA((1,))],
    )
    return pl.pallas_call(
        _moe_kernel,
        grid_spec=grid_spec,
        out_shape=jax.ShapeDtypeStruct((2 * n + MOE_TM,) + tile3, F32),
        compiler_params=_cparams(("arbitrary",)),
        name="moe",
    )(te, n_active, src, src, dst, wrow, x1, wgu_bf16, wd_bf16)


def _combine_kernel(x1_ref, ya_ref, yb_ref, g_ref, b_ref, o_ref):
    f = _load_token_tiles(ya_ref) + _load_token_tiles(yb_ref)
    o_ref[...] = _ln(DEEPNORM_ALPHA * _load_token_tiles(x1_ref) + f, g_ref[...], b_ref[...])


def _combine(x1, y2, g2, b2, row0, rows, tm):
    n = x1.shape[0]
    b0 = row0 // tm
    row = lambda off: pl.BlockSpec((tm, SUBLANES, LANES), lambda i: (i + off, 0, 0))
    full = pl.BlockSpec((1, D_MODEL), lambda i: (0, 0))
    return pl.pallas_call(
        _combine_kernel,
        grid=(rows // tm,),
        in_specs=[row(b0), row(b0), row(b0 + n // tm), full, full],
        out_specs=pl.BlockSpec((tm, D_MODEL), lambda i: (i, 0)),
        out_shape=jax.ShapeDtypeStruct((rows, D_MODEL), F32),
        compiler_params=_cparams(("parallel",)),
        name="combine",
    )(x1, y2, y2, g2, b2)


def _later_or_same(n):
    r = lax.broadcasted_iota(jnp.int32, (n, n), 0)
    c = lax.broadcasted_iota(jnp.int32, (n, n), 1)
    return (r >= c).astype(BF16)


def kernel(x_prompt, x_sample, mem_prompt, cache_sba_k, cache_sba_v, cache_conv, cache_mem_k, cache_mem_v,
           ln0_g, ln0_b, w_in, w_dw, b_dw, lnc_g, lnc_b, w_cpw, w_mk, w_mv, w_out, ln1_g, ln1_b,
           w_rg, b_rg, w_re, b_re, w_eg, w_eu, w_ed, ln2_g, ln2_b):
    bp, tp, _ = x_prompt.shape
    bs, ts, _ = x_sample.shape
    past = cache_sba_k.shape[2]
    l = 0
    r2 = lambda a: a.reshape(1, -1)

    w_in_b = w_in[l].astype(BF16)
    w_kv_b = jnp.concatenate([w_mk[l], w_mv[l]], axis=1).astype(BF16)
    w_out_b = w_out[l].astype(BF16)
    w_cpw_b = w_cpw[l].astype(BF16)
    wr = jnp.zeros((D_MODEL, LANES), F32)
    wr = wr.at[:, 0:N_EXPERTS].set(w_re[l]).at[:, N_EXPERTS:N_EXPERTS + N_GROUPS].set(w_rg[l])
    wr0 = wr.astype(BF16)
    wr1 = (wr - wr0.astype(F32)).astype(BF16)
    wr3 = jnp.stack([wr0, wr1])
    br = jnp.zeros((1, LANES), F32)
    br = br.at[0, 0:N_EXPERTS].set(b_re[l]).at[0, N_EXPERTS:N_EXPERTS + N_GROUPS].set(b_rg[l])
    wgu_b = jnp.concatenate([w_eg[l], w_eu[l]], axis=-1).astype(BF16)
    wd_b = w_ed[l].astype(BF16)
    g0, b0 = r2(ln0_g), r2(ln0_b)

    tq = 256
    tk_s = 256
    u_p = _later_or_same(tq)
    u_s = _later_or_same(tk_s)
    u_n = _later_or_same(ts)

    def group(x, batch, seq, tm):
        x2d = x.reshape(batch * seq, D_MODEL)
        return x2d, _in_proj(x2d, g0, b0, w_in_b, tm)

    xp2d, (qp, kbp, vbp, kp, vp, glup, mqp) = group(x_prompt, bp, tp, 512)
    xs2d, (qs, kbs, vbs, ks, vs, glus, mqs) = group(x_sample, bs, ts, 512)

    sba_p = _sba_prompt(qp, kbp, vbp, u_p, bp, tp, tq, 4)
    kc = cache_sba_k[l].reshape(bs * past * SBA_HEADS, HEAD_DIM)
    vc = cache_sba_v[l].reshape(bs * past * SBA_HEADS, HEAD_DIM)
    sba_s = _sba_sample(qs, kbs, vbs, kc, vc, u_s, u_n, bs, past, ts, tk_s, 4)

    conv_w = (w_dw[l], r2(b_dw[l]), r2(lnc_g[l]), r2(lnc_b[l]), w_cpw_b)
    h0_p = jnp.zeros((bp, HALO, CONV_W), F32)
    h0_s = jnp.pad(cache_conv[l], ((0, 0), (HALO - (CONV_K - 1), 0), (0, 0)))
    conv_p = _conv_module(h0_p, glup, *conv_w, bp, tp, 256)
    conv_s = _conv_module(h0_s, glus, *conv_w, bs, ts, ts)

    mk, mv = _mem_kv(mem_prompt.reshape(bp * N_MEM, D_MODEL), w_kv_b, 512)
    mem_p = _mem_attn(mqp, mk.reshape(bp, N_MEM, MEM_W), mv.reshape(bp, N_MEM, MEM_W), bp, tp, 512)
    mem_s = _mem_attn(mqs, cache_mem_k[l].reshape(bs, N_MEM, MEM_W),
                      cache_mem_v[l].reshape(bs, N_MEM, MEM_W), bs, ts, ts)

    post_w = (g0, b0, w_out_b, r2(ln1_g[l]), r2(ln1_b[l]), wr3, br)
    x1p, routep = _post(xp2d, sba_p, conv_p, mem_p, *post_w, 512)
    x1s, routes = _post(xs2d, sba_s, conv_s, mem_s, *post_w, 512)
    x1 = jnp.concatenate([x1p, x1s], axis=0)
    n_tok = x1.shape[0]
    y2 = _moe(x1, _moe_plan(jnp.concatenate([routep, routes], axis=0), n_tok), wgu_b, wd_b)
    g2, b2 = r2(ln2_g[l]), r2(ln2_b[l])
    yp = _combine(x1, y2, g2, b2, 0, bp * tp, 512)
    ys = _combine(x1, y2, g2, b2, bp * tp, bs * ts, 512)

    hd = (SBA_HEADS, HEAD_DIM)
    glup3 = glup.reshape(bp, tp, CONV_W)
    glus3 = glus.reshape(bs, ts, CONV_W)
    conv_tail_s = jnp.concatenate([cache_conv[l], glus3], axis=1)[:, -(CONV_K - 1):]
    return (
        yp.reshape(bp, tp, D_MODEL),
        ys.reshape(bs, ts, D_MODEL),
        kp.reshape(1, bp, tp, *hd),
        vp.reshape(1, bp, tp, *hd),
        glup3[:, -(CONV_K - 1):][None],
        mk.reshape(1, bp, N_MEM, MEM_HEADS, HEAD_DIM),
        mv.reshape(1, bp, N_MEM, MEM_HEADS, HEAD_DIM),
        ks.reshape(1, bs, ts, *hd),
        vs.reshape(1, bs, ts, *hd),
        conv_tail_s[None],
    )
```

```python
import functools

import jax
import jax.numpy as jnp
from jax import lax
from jax.experimental import pallas as pl
from jax.experimental.pallas import tpu as pltpu

F32 = jnp.float32
BF16 = jnp.bfloat16

D_MODEL = 1024
HEAD_DIM = 64
SBA_HEADS = 8
SBA_W = SBA_HEADS * HEAD_DIM
CONV_W = 256
CONV_K = 31
MEM_HEADS = 4
MEM_W = MEM_HEADS * HEAD_DIM
N_MEM = 256
N_GROUPS = 4
EXPERTS_PER_GROUP = 8
N_EXPERTS = N_GROUPS * EXPERTS_PER_GROUP
D_EXPERT = 256
DEPTH = 1
DEEPNORM_ALPHA = (2 * DEPTH) ** 0.25
QK_SCALE = HEAD_DIM ** -0.5
LN_EPS = 1e-5
LOG2E = 1.4426950408889634

LANES = 128
SUBLANES = 8
HALO = 32
MOE_TM = 256
MOE_UNROLL = 8
SP_LINEAR = 100.0
VMEM_LIMIT = 48 * 1024 * 1024


def _cparams(sem):
    return pltpu.CompilerParams(dimension_semantics=sem, vmem_limit_bytes=VMEM_LIMIT)


def _ln(x, g, b):
    mu = jnp.mean(x, axis=-1, keepdims=True)
    xc = x - mu
    var = jnp.mean(xc * xc, axis=-1, keepdims=True)
    return xc * lax.rsqrt(var + LN_EPS) * g + b


def _dot(a, b):
    return jnp.dot(a, b, preferred_element_type=F32)


def _dot_nt(a, b):
    return lax.dot_general(a, b, (((1,), (1,)), ((), ())), preferred_element_type=F32)


def _in_proj_kernel(x_ref, g_ref, b_ref, w_ref, q_ref, kb_ref, vb_ref, k_ref, v_ref, glu_ref, mq_ref):
    xn = _ln(x_ref[...], g_ref[...], b_ref[...]).astype(BF16)
    q = _dot(xn, w_ref[:, 0:SBA_W])
    q_ref[...] = (q * (QK_SCALE * LOG2E)).astype(BF16)
    k = _dot(xn, w_ref[:, SBA_W:2 * SBA_W])
    k_ref[...] = k
    kb_ref[...] = k.astype(BF16)
    v = _dot(xn, w_ref[:, 2 * SBA_W:3 * SBA_W])
    v_ref[...] = v
    vb_ref[...] = v.astype(BF16)
    c0 = 3 * SBA_W
    cv = _dot(xn, w_ref[:, c0:c0 + CONV_W])
    cg = _dot(xn, w_ref[:, c0 + CONV_W:c0 + 2 * CONV_W])
    glu_ref[...] = cv * jax.nn.sigmoid(cg)
    mq = _dot(xn, w_ref[:, c0 + 2 * CONV_W:c0 + 2 * CONV_W + MEM_W])
    mq_ref[...] = (mq * QK_SCALE).astype(BF16)


def _in_proj_t_kernel(x_ref, g_ref, b_ref, w_ref, wkt_ref, wvt_ref,
                      q_ref, ktb_ref, vtb_ref, kt_ref, vt_ref, glu_ref, mq_ref, *, tq):
    xn = _ln(x_ref[...], g_ref[...], b_ref[...]).astype(BF16)
    tm = xn.shape[0]
    q = _dot(xn, w_ref[:, 0:SBA_W])
    q_ref[...] = (q * (QK_SCALE * LOG2E)).astype(BF16)
    for wt_ref, t_ref, tb_ref in ((wkt_ref, kt_ref, ktb_ref), (wvt_ref, vt_ref, vtb_ref)):
        t = _dot_nt(wt_ref[...], xn)
        t_ref[0] = t
        for i in range(tm // tq):
            tb_ref[0, i] = t[:, i * tq:(i + 1) * tq].astype(BF16)
    c0 = 3 * SBA_W
    cv = _dot(xn, w_ref[:, c0:c0 + CONV_W])
    cg = _dot(xn, w_ref[:, c0 + CONV_W:c0 + 2 * CONV_W])
    glu_ref[...] = cv * jax.nn.sigmoid(cg)
    mq = _dot(xn, w_ref[:, c0 + 2 * CONV_W:c0 + 2 * CONV_W + MEM_W])
    mq_ref[...] = (mq * QK_SCALE).astype(BF16)


def _in_proj_t(x2d, g, b, w_bf16, wkt, wvt, batch, seq, tm, tq):
    n = x2d.shape[0]
    in_w = w_bf16.shape[1]
    nt = seq // tm
    row = lambda w: pl.BlockSpec((tm, w), lambda bi, i: (bi * nt + i, 0))
    full = lambda s: pl.BlockSpec(s, lambda bi, i: (0, 0))
    tr = pl.BlockSpec((1, SBA_W, tm), lambda bi, i: (bi, 0, i))
    trb = pl.BlockSpec((1, tm // tq, SBA_W, tq), lambda bi, i: (bi, i, 0, 0))
    return pl.pallas_call(
        functools.partial(_in_proj_t_kernel, tq=tq),
        grid=(batch, nt),
        in_specs=[row(D_MODEL), full((1, D_MODEL)), full((1, D_MODEL)), full((D_MODEL, in_w)),
                  full((SBA_W, D_MODEL)), full((SBA_W, D_MODEL))],
        out_specs=[row(SBA_W), trb, trb, tr, tr, row(CONV_W), row(MEM_W)],
        out_shape=[
            jax.ShapeDtypeStruct((n, SBA_W), BF16),
            jax.ShapeDtypeStruct((batch, seq // tq, SBA_W, tq), BF16),
            jax.ShapeDtypeStruct((batch, seq // tq, SBA_W, tq), BF16),
            jax.ShapeDtypeStruct((batch, SBA_W, seq), F32),
            jax.ShapeDtypeStruct((batch, SBA_W, seq), F32),
            jax.ShapeDtypeStruct((n, CONV_W), F32),
            jax.ShapeDtypeStruct((n, MEM_W), BF16),
        ],
        compiler_params=_cparams(("parallel", "parallel")),
        name="in_proj_t",
    )(x2d, g, b, w_bf16, wkt, wvt)


def _in_proj(x2d, g, b, w_bf16, tm):
    n = x2d.shape[0]
    in_w = w_bf16.shape[1]
    row = lambda w: pl.BlockSpec((tm, w), lambda i: (i, 0))
    full = lambda s: pl.BlockSpec(s, lambda i: (0, 0))
    return pl.pallas_call(
        _in_proj_kernel,
        grid=(n // tm,),
        in_specs=[row(D_MODEL), full((1, D_MODEL)), full((1, D_MODEL)), full((D_MODEL, in_w))],
        out_specs=[row(SBA_W), row(SBA_W), row(SBA_W), row(SBA_W), row(SBA_W), row(CONV_W), row(MEM_W)],
        out_shape=[
            jax.ShapeDtypeStruct((n, SBA_W), BF16),
            jax.ShapeDtypeStruct((n, SBA_W), BF16),
            jax.ShapeDtypeStruct((n, SBA_W), BF16),
            jax.ShapeDtypeStruct((n, SBA_W), F32),
            jax.ShapeDtypeStruct((n, SBA_W), F32),
            jax.ShapeDtypeStruct((n, CONV_W), F32),
            jax.ShapeDtypeStruct((n, MEM_W), BF16),
        ],
        compiler_params=_cparams(("parallel",)),
        name="in_proj",
    )(x2d, g, b, w_bf16)


def _head_masks(shape):
    lane = lax.broadcasted_iota(jnp.int32, shape, 1)
    return lane < HEAD_DIM


def _stack_heads(q, low):
    zero = jnp.zeros_like(q)
    return jnp.concatenate([jnp.where(low, q, zero), jnp.where(low, zero, q)], axis=0)


def _sba_tile(z, u, mask):
    sp = jnp.where(z > SP_LINEAR, z, jnp.log(1.0 + jnp.exp2(z)) * LOG2E)
    if mask is not None:
        sp = jnp.where(mask, sp, 0.0)
    w = jnp.exp2(z - _dot(sp.astype(BF16), u))
    if mask is not None:
        w = jnp.where(mask, w, 0.0)
    return w.astype(BF16), jnp.sum(sp, axis=1, keepdims=True)


def _sba_prompt_kernel(q_ref, k_ref, v_ref, u_ref, o_ref, *, tq, hp):
    qi = pl.program_id(2)
    low = _head_masks((tq, LANES))
    u = u_ref[...]
    row = lax.broadcasted_iota(jnp.int32, (2 * tq, tq), 0) & (tq - 1)
    col = lax.broadcasted_iota(jnp.int32, (2 * tq, tq), 1)
    qs = [_stack_heads(q_ref[:, p * LANES:(p + 1) * LANES], low) for p in range(hp)]

    def blk(j, carry, mask):
        out = []
        for p in range(hp):
            c, acc = carry[p]
            z = _dot(qs[p], k_ref[0, j, p * LANES:(p + 1) * LANES, :])
            w, rs = _sba_tile(z, u, mask)
            acc = acc + _dot_nt(w, v_ref[0, j, p * LANES:(p + 1) * LANES, :]) * jnp.exp2(-c)
            out.append((c + rs, acc))
        return tuple(out)

    carry = tuple((jnp.zeros((2 * tq, 1), F32), jnp.zeros((2 * tq, LANES), F32)) for _ in range(hp))
    carry = blk(qi, carry, col < row)
    carry = lax.fori_loop(0, qi, lambda t, cr: blk(qi - 1 - t, cr, None), carry)
    for p in range(hp):
        acc = carry[p][1]
        o_ref[:, p * LANES:(p + 1) * LANES] = jnp.where(low, acc[0:tq], acc[tq:2 * tq]).astype(o_ref.dtype)


def _sba_prompt(q, kt, vt, u, batch, seq, tq, hp):
    n = q.shape[0]
    nq = seq // tq
    groups = SBA_W // (LANES * hp)
    kv = pl.BlockSpec((1, nq, hp * LANES, tq), lambda b, p, i: (b, 0, p, 0))
    return pl.pallas_call(
        functools.partial(_sba_prompt_kernel, tq=tq, hp=hp),
        grid=(batch, groups, nq),
        in_specs=[
            pl.BlockSpec((tq, hp * LANES), lambda b, p, i: (b * nq + i, p)),
            kv, kv,
            pl.BlockSpec((tq, tq), lambda b, p, i: (0, 0)),
        ],
        out_specs=pl.BlockSpec((tq, hp * LANES), lambda b, p, i: (b * nq + i, p)),
        out_shape=jax.ShapeDtypeStruct((n, SBA_W), BF16),
        compiler_params=_cparams(("parallel", "parallel", "arbitrary")),
        name="sba_prompt",
    )(q, kt, vt, u)


def _sba_sample_kernel(q_ref, kn_ref, vn_ref, kc_ref, vc_ref, u_ref, un_ref, o_ref, qs_ref, c_ref, acc_ref,
                       *, tn, tk, nsub):
    j = pl.program_id(1)
    nj = pl.num_programs(1)
    low = _head_masks((tn, LANES))
    pairs = SBA_W // LANES
    rows = 2 * tn

    def step(logits, weighted, u, mask):
        z = jnp.concatenate([logits(qs_ref[p], p) for p in range(pairs)], axis=0)
        w, rs = _sba_tile(z, u, mask)
        pv = jnp.concatenate([weighted(w[p * rows:(p + 1) * rows], p) for p in range(pairs)], axis=0)
        c = c_ref[...]
        acc_ref[...] += pv * jnp.exp2(-c)
        c_ref[...] = c + rs

    def lanes(p):
        return slice(p * LANES, (p + 1) * LANES)

    @pl.when(j == 0)
    def _():
        for p in range(pairs):
            qs_ref[p] = _stack_heads(q_ref[:, lanes(p)], low)
        c_ref[...] = jnp.zeros_like(c_ref)
        acc_ref[...] = jnp.zeros_like(acc_ref)
        row = lax.broadcasted_iota(jnp.int32, (pairs * rows, tn), 0) & (tn - 1)
        col = lax.broadcasted_iota(jnp.int32, (pairs * rows, tn), 1)
        step(lambda q, p: _dot_nt(q, kn_ref[:, lanes(p)]), lambda w, p: _dot(w, vn_ref[:, lanes(p)]),
             un_ref[...], col < row)

    for s in reversed(range(nsub)):
        keys = slice(s * tk, (s + 1) * tk)
        step(lambda q, p: _dot(q, kc_ref[0, lanes(p), keys].astype(BF16)),
             lambda w, p: _dot_nt(w, vc_ref[0, lanes(p), keys].astype(BF16)),
             u_ref[...], None)

    @pl.when(j == nj - 1)
    def _():
        for p in range(pairs):
            a0 = acc_ref[p * rows:p * rows + tn, :]
            a1 = acc_ref[p * rows + tn:(p + 1) * rows, :]
            o_ref[:, lanes(p)] = jnp.where(low, a0, a1).astype(o_ref.dtype)


def _sba_sample(q, kn, vn, kc, vc, u, un, batch, tn, tk, nsub):
    n = q.shape[0]
    past = kc.shape[2]
    tkb = tk * nsub
    nj = past // tkb
    rows = SBA_HEADS * tn
    cache = pl.BlockSpec((1, SBA_W, tkb), lambda b, j: (b, 0, nj - 1 - j))
    return pl.pallas_call(
        functools.partial(_sba_sample_kernel, tn=tn, tk=tk, nsub=nsub),
        grid=(batch, nj),
        in_specs=[
            pl.BlockSpec((tn, SBA_W), lambda b, j: (b, 0)),
            pl.BlockSpec((tn, SBA_W), lambda b, j: (b, 0)),
            pl.BlockSpec((tn, SBA_W), lambda b, j: (b, 0)),
            cache, cache,
            pl.BlockSpec((tk, tk), lambda b, j: (0, 0)),
            pl.BlockSpec((tn, tn), lambda b, j: (0, 0)),
        ],
        out_specs=pl.BlockSpec((tn, SBA_W), lambda b, j: (b, 0)),
        out_shape=jax.ShapeDtypeStruct((n, SBA_W), BF16),
        scratch_shapes=[pltpu.VMEM((SBA_W // LANES, 2 * tn, LANES), BF16),
                        pltpu.VMEM((rows, 1), F32), pltpu.VMEM((rows, LANES), F32)],
        compiler_params=_cparams(("parallel", "arbitrary")),
        name="sba_sample",
    )(q, kn, vn, kc, vc, u, un)


def _conv_kernel(h0_ref, glu_ref, wdw_ref, bdw_ref, g_ref, b_ref, wpw_ref, o_ref, buf_ref, *, tt, rc):
    ti = pl.program_id(1)

    @pl.when(ti == 0)
    def _():
        buf_ref[0:HALO, :] = h0_ref[0]

    @pl.when(ti > 0)
    def _():
        buf_ref[0:HALO, :] = buf_ref[tt:tt + HALO, :]

    buf_ref[HALO:HALO + tt, :] = glu_ref[...]
    base = HALO - (CONV_K - 1)
    wdw = wdw_ref[...]
    for r0 in range(0, tt, rc):
        acc = jnp.zeros((rc, CONV_W), F32)
        for kk in range(CONV_K):
            acc = acc + buf_ref[pl.ds(base + r0 + kk, rc), :] * wdw[kk:kk + 1, :]
        u = _ln(acc + bdw_ref[...], g_ref[...], b_ref[...])
        s = u * jax.nn.sigmoid(u)
        o_ref[pl.ds(r0, rc), :] = _dot(s.astype(BF16), wpw_ref[...]).astype(o_ref.dtype)


def _conv_module(h0, glu, w_dw, b_dw, g, b, w_pw_bf16, batch, seq, tt):
    n = glu.shape[0]
    nt = seq // tt
    rc = min(tt, 64)
    full = lambda s: pl.BlockSpec(s, lambda bi, ti: (0,) * len(s))
    return pl.pallas_call(
        functools.partial(_conv_kernel, tt=tt, rc=rc),
        grid=(batch, nt),
        in_specs=[
            pl.BlockSpec((1, HALO, CONV_W), lambda bi, ti: (bi, 0, 0)),
            pl.BlockSpec((tt, CONV_W), lambda bi, ti: (bi * nt + ti, 0)),
            full((CONV_K, CONV_W)), full((1, CONV_W)), full((1, CONV_W)), full((1, CONV_W)),
            full((CONV_W, CONV_W)),
        ],
        out_specs=pl.BlockSpec((tt, CONV_W), lambda bi, ti: (bi * nt + ti, 0)),
        out_shape=jax.ShapeDtypeStruct((n, CONV_W), BF16),
        scratch_shapes=[pltpu.VMEM((HALO + tt, CONV_W), F32)],
        compiler_params=_cparams(("parallel", "arbitrary")),
        name="conv_module",
    )(h0, glu, w_dw, b_dw, g, b, w_pw_bf16)


def _mem_kv_kernel(m_ref, wkt_ref, wvt_ref, k_ref, v_ref):
    m = m_ref[...].astype(BF16)
    k_ref[0] = _dot_nt(wkt_ref[...], m)
    v_ref[0] = _dot_nt(wvt_ref[...], m)


def _mem_kv(mem2d, wkt, wvt, batch):
    full = pl.BlockSpec((MEM_W, D_MODEL), lambda i: (0, 0))
    out = pl.BlockSpec((1, MEM_W, N_MEM), lambda i: (i, 0, 0))
    return pl.pallas_call(
        _mem_kv_kernel,
        grid=(batch,),
        in_specs=[pl.BlockSpec((N_MEM, D_MODEL), lambda i: (i, 0)), full, full],
        out_specs=[out, out],
        out_shape=[jax.ShapeDtypeStruct((batch, MEM_W, N_MEM), F32)] * 2,
        compiler_params=_cparams(("parallel",)),
        name="mem_kv",
    )(mem2d, wkt, wvt)


def _mem_attn_kernel(q_ref, k_ref, v_ref, o_ref):
    q = q_ref[...]
    tq = q.shape[0]
    low = _head_masks((tq, LANES))
    for p in range(MEM_W // LANES):
        qs = _stack_heads(q[:, p * LANES:(p + 1) * LANES], low)
        kt = k_ref[0, p * LANES:(p + 1) * LANES, :].astype(BF16)
        vt = v_ref[0, p * LANES:(p + 1) * LANES, :].astype(BF16)
        s = _dot(qs, kt)
        e = jnp.exp(s - jnp.max(s, axis=-1, keepdims=True))
        o = _dot_nt(e.astype(BF16), vt) / jnp.sum(e, axis=-1, keepdims=True)
        o_ref[:, p * LANES:(p + 1) * LANES] = jnp.where(low, o[0:tq], o[tq:2 * tq]).astype(o_ref.dtype)


def _mem_attn(mq, mkt, mvt, batch, seq, tq):
    n = mq.shape[0]
    nq = seq // tq
    kv = pl.BlockSpec((1, MEM_W, N_MEM), lambda b, i: (b, 0, 0))
    return pl.pallas_call(
        _mem_attn_kernel,
        grid=(batch, nq),
        in_specs=[pl.BlockSpec((tq, MEM_W), lambda b, i: (b * nq + i, 0)), kv, kv],
        out_specs=pl.BlockSpec((tq, MEM_W), lambda b, i: (b * nq + i, 0)),
        out_shape=jax.ShapeDtypeStruct((n, MEM_W), BF16),
        compiler_params=_cparams(("parallel", "parallel")),
        name="mem_attn",
    )(mq, mkt, mvt)


def _store_token_tiles(ref, x):
    for c in range(SUBLANES):
        ref[:, c, :] = x[:, c * LANES:(c + 1) * LANES]


def _load_token_tiles(ref):
    return jnp.concatenate([ref[:, c, :] for c in range(SUBLANES)], axis=1)


def _split2(x):
    a = x.astype(BF16)
    b = (x - a.astype(F32)).astype(BF16)
    return a, b


def _post_kernel(x_ref, sba_ref, conv_ref, mem_ref, g0_ref, b0_ref, wo_ref, g1_ref, b1_ref,
                 wr_ref, br_ref, x1_ref, route_ref):
    xn = _ln(x_ref[...], g0_ref[...], b0_ref[...])
    mix = _dot(sba_ref[...], wo_ref[0:SBA_W, :])
    mix = mix + _dot(conv_ref[...], wo_ref[SBA_W:SBA_W + CONV_W, :])
    mix = mix + _dot(mem_ref[...], wo_ref[SBA_W + CONV_W:SBA_W + CONV_W + MEM_W, :])
    x1 = _ln(DEEPNORM_ALPHA * xn + mix, g1_ref[...], b1_ref[...])
    _store_token_tiles(x1_ref, x1)

    a0, a1 = _split2(x1)
    w0, w1 = wr_ref[0], wr_ref[1]
    logits = _dot(a0, w0) + (_dot(a0, w1) + _dot(a1, w0)) + br_ref[...]
    tm = logits.shape[0]
    lane = lax.broadcasted_iota(jnp.int32, (tm, LANES), 1).astype(F32)
    neg = jnp.float32(-jnp.inf)
    big = jnp.float32(LANES)
    is_g = jnp.logical_and(lane >= N_EXPERTS, lane < N_EXPERTS + N_GROUPS)
    gl = jnp.where(is_g, logits, neg)
    gmax = jnp.max(gl, axis=-1, keepdims=True)
    g_idx = jnp.min(jnp.where(gl == gmax, lane, big), axis=-1, keepdims=True) - N_EXPERTS
    g_w = 1.0 / jnp.sum(jnp.exp(gl - gmax), axis=-1, keepdims=True)
    in_grp = jnp.logical_and(lane >= g_idx * EXPERTS_PER_GROUP, lane < (g_idx + 1.0) * EXPERTS_PER_GROUP)
    el = jnp.where(in_grp, logits, neg)
    v1 = jnp.max(el, axis=-1, keepdims=True)
    i1 = jnp.min(jnp.where(el == v1, lane, big), axis=-1, keepdims=True)
    el2 = jnp.where(lane == i1, neg, el)
    v2 = jnp.max(el2, axis=-1, keepdims=True)
    i2 = jnp.min(jnp.where(el2 == v2, lane, big), axis=-1, keepdims=True)
    e2 = jnp.exp(v2 - v1)
    p1 = 1.0 / (1.0 + e2)
    p2 = e2 / (1.0 + e2)
    route_ref[...] = jnp.where(lane == 0.0, i1, jnp.where(lane == 1.0, i2, jnp.where(
        lane == 2.0, p1 * g_w, jnp.where(lane == 3.0, p2 * g_w, 0.0))))


def _post(x2d, sba, conv, mem, g0, b0, wo_bf16, g1, b1, wr3, br, tm):
    n = x2d.shape[0]
    row = lambda w: pl.BlockSpec((tm, w), lambda i: (i, 0))
    full = lambda s: pl.BlockSpec(s, lambda i: (0,) * len(s))
    return pl.pallas_call(
        _post_kernel,
        grid=(n // tm,),
        in_specs=[row(D_MODEL), row(SBA_W), row(CONV_W), row(MEM_W),
                  full((1, D_MODEL)), full((1, D_MODEL)), full((D_MODEL, D_MODEL)),
                  full((1, D_MODEL)), full((1, D_MODEL)),
                  full((2, D_MODEL, LANES)), full((1, LANES))],
        out_specs=[pl.BlockSpec((tm, SUBLANES, LANES), lambda i: (i, 0, 0)), row(LANES)],
        out_shape=[jax.ShapeDtypeStruct((n, SUBLANES, LANES), F32),
                   jax.ShapeDtypeStruct((n, LANES), F32)],
        compiler_params=_cparams(("parallel",)),
        name="post",
    )(x2d, sba, conv, mem, g0, b0, wo_bf16, g1, b1, wr3, br)


def _moe_plan(route, n):
    pairs = 2 * n
    tiles = pairs // MOE_TM + N_EXPERTS
    e = jnp.concatenate([route[:, 0], route[:, 1]]).astype(jnp.int32)
    w = jnp.concatenate([route[:, 2], route[:, 3]])
    order = jnp.argsort(e, stable=True).astype(jnp.int32)
    ids = jnp.arange(N_EXPERTS, dtype=jnp.int32)
    counts = jnp.sum((e[:, None] == ids[None, :]).astype(jnp.int32), axis=0)
    ntile = (counts + MOE_TM - 1) // MOE_TM
    tile_end = jnp.cumsum(ntile)
    tile_start = tile_end - ntile
    first = jnp.cumsum(counts) - counts
    n_active = tile_end[-1]
    t = jnp.arange(tiles, dtype=jnp.int32)
    te = jnp.minimum(jnp.sum((t[:, None] >= tile_end[None, :]).astype(jnp.int32), axis=1), N_EXPERTS - 1)
    done = (t - tile_start[te]) * MOE_TM
    n_valid = jnp.where(t < n_active, jnp.clip(counts[te] - done, 0, MOE_TM), 0)
    r = jnp.arange(MOE_TM, dtype=jnp.int32)
    valid = r[None, :] < n_valid[:, None]
    pos = jnp.clip(first[te][:, None] + done[:, None] + r[None, :], 0, pairs - 1)
    pair = order[pos]
    src = jnp.where(valid, jnp.where(pair >= n, pair - n, pair), 0).astype(jnp.int32)
    dst = jnp.where(valid, pair, pairs + r[None, :]).astype(jnp.int32)
    wrow = jnp.where(valid, w[pair], 0.0).reshape(tiles * MOE_TM, 1)
    return (te.astype(jnp.int32), n_active.reshape(1).astype(jnp.int32),
            src.reshape(tiles, 1, MOE_TM), dst.reshape(tiles, 1, MOE_TM), wrow)


def _moe_kernel(te_ref, na_ref, src0_ref, src1_ref, dst_ref, w_ref, x_hbm, wgu_ref, wd_ref,
                y_hbm, xg_ref, yo_ref, gsem, ssem):
    t = pl.program_id(0)
    n_active = na_ref[0]

    def start_gather(src_ref, slot):
        def body(i, carry):
            for k in range(MOE_UNROLL):
                r = i * MOE_UNROLL + k
                pltpu.make_async_copy(x_hbm.at[pl.ds(src_ref[0, 0, r], 1)], xg_ref.at[slot, pl.ds(r, 1)],
                                      gsem.at[slot]).start(priority=k % 2)
            return carry
        lax.fori_loop(0, MOE_TM // MOE_UNROLL, body, 0)

    def wait_gather(slot):
        pltpu.make_async_copy(x_hbm.at[pl.ds(0, MOE_TM)], xg_ref.at[slot], gsem.at[slot]).wait()

    def wait_scatter():
        pltpu.make_async_copy(yo_ref, y_hbm.at[pl.ds(0, MOE_TM)], ssem.at[0]).wait()

    @pl.when(t < n_active)
    def _():
        slot = lax.rem(t, 2)

        @pl.when(t == 0)
        def _():
            start_gather(src0_ref, 0)
            yo_ref[...] = jnp.zeros_like(yo_ref)
            sink = pltpu.make_async_copy(yo_ref, y_hbm.at[pl.ds(y_hbm.shape[0] - MOE_TM, MOE_TM)], ssem.at[0])
            sink.start()
            sink.wait()

        @pl.when(t + 1 < n_active)
        def _():
            start_gather(src1_ref, 1 - slot)

        wait_gather(slot)
        x = _load_token_tiles(xg_ref.at[slot]).astype(BF16)
        a = _dot(x, wgu_ref[0, :, 0:D_EXPERT])
        u = _dot(x, wgu_ref[0, :, D_EXPERT:2 * D_EXPERT])
        hid = (a * jax.nn.sigmoid(a)) * u * w_ref[...]
        y = _dot(hid.astype(BF16), wd_ref[0])

        @pl.when(t > 0)
        def _():
            wait_scatter()

        _store_token_tiles(yo_ref, y)

        def body(i, carry):
            for k in range(MOE_UNROLL):
                r = i * MOE_UNROLL + k
                pltpu.make_async_copy(yo_ref.at[pl.ds(r, 1)], y_hbm.at[pl.ds(dst_ref[0, 0, r], 1)],
                                      ssem.at[0]).start(priority=k % 2)
            return carry
        lax.fori_loop(0, MOE_TM // MOE_UNROLL, body, 0)

        @pl.when(t == n_active - 1)
        def _():
            wait_scatter()


def _moe(x1, plan, wgu_bf16, wd_bf16):
    te, n_active, src, dst, wrow = plan
    n = x1.shape[0]
    tiles = src.shape[0]
    idx = lambda f: pl.BlockSpec((1, 1, MOE_TM), f, memory_space=pltpu.SMEM)
    tile3 = (SUBLANES, LANES)
    grid_spec = pltpu.PrefetchScalarGridSpec(
        num_scalar_prefetch=2,
        grid=(tiles,),
        in_specs=[
            idx(lambda t, te, na: (t, 0, 0)),
            idx(lambda t, te, na: (jnp.minimum(t + 1, tiles - 1), 0, 0)),
            idx(lambda t, te, na: (t, 0, 0)),
            pl.BlockSpec((MOE_TM, 1), lambda t, te, na: (t, 0)),
            pl.BlockSpec(memory_space=pl.ANY),
            pl.BlockSpec((1, D_MODEL, 2 * D_EXPERT), lambda t, te, na: (te[t], 0, 0)),
            pl.BlockSpec((1, D_EXPERT, D_MODEL), lambda t, te, na: (te[t], 0, 0)),
        ],
        out_specs=pl.BlockSpec(memory_space=pl.ANY),
        scratch_shapes=[pltpu.VMEM((2, MOE_TM) + tile3, F32), pltpu.VMEM((MOE_TM,) + tile3, F32),
                        pltpu.SemaphoreType.DMA((2,)), pltpu.SemaphoreType.DMA((1,))],
    )
    return pl.pallas_call(
        _moe_kernel,
        grid_spec=grid_spec,
        out_shape=jax.ShapeDtypeStruct((2 * n + MOE_TM,) + tile3, F32),
        compiler_params=_cparams(("arbitrary",)),
        name="moe",
    )(te, n_active, src, src, dst, wrow, x1, wgu_bf16, wd_bf16)


def _combine_kernel(x1_ref, ya_ref, yb_ref, g_ref, b_ref, o_ref):
    f = _load_token_tiles(ya_ref) + _load_token_tiles(yb_ref)
    o_ref[...] = _ln(DEEPNORM_ALPHA * _load_token_tiles(x1_ref) + f, g_ref[...], b_ref[...])


def _combine(x1, y2, g2, b2, row0, rows, tm):
    n = x1.shape[0]
    b0 = row0 // tm
    row = lambda off: pl.BlockSpec((tm, SUBLANES, LANES), lambda i: (i + off, 0, 0))
    full = pl.BlockSpec((1, D_MODEL), lambda i: (0, 0))
    return pl.pallas_call(
        _combine_kernel,
        grid=(rows // tm,),
        in_specs=[row(b0), row(b0), row(b0 + n // tm), full, full],
        out_specs=pl.BlockSpec((tm, D_MODEL), lambda i: (i, 0)),
        out_shape=jax.ShapeDtypeStruct((rows, D_MODEL), F32),
        compiler_params=_cparams(("parallel",)),
        name="combine",
    )(x1, y2, y2, g2, b2)


def _later_or_same(n):
    r = lax.broadcasted_iota(jnp.int32, (n, n), 0)
    c = lax.broadcasted_iota(jnp.int32, (n, n), 1)
    return (r >= c).astype(BF16)


def kernel(x_prompt, x_sample, mem_prompt, cache_sba_k, cache_sba_v, cache_conv, cache_mem_k, cache_mem_v,
           ln0_g, ln0_b, w_in, w_dw, b_dw, lnc_g, lnc_b, w_cpw, w_mk, w_mv, w_out, ln1_g, ln1_b,
           w_rg, b_rg, w_re, b_re, w_eg, w_eu, w_ed, ln2_g, ln2_b):
    bp, tp, _ = x_prompt.shape
    bs, ts, _ = x_sample.shape
    past = cache_sba_k.shape[2]
    l = 0
    r2 = lambda a: a.reshape(1, -1)

    w_in_b = w_in[l].astype(BF16)
    w_out_b = w_out[l].astype(BF16)
    w_cpw_b = w_cpw[l].astype(BF16)
    wr = jnp.zeros((D_MODEL, LANES), F32)
    wr = wr.at[:, 0:N_EXPERTS].set(w_re[l]).at[:, N_EXPERTS:N_EXPERTS + N_GROUPS].set(w_rg[l])
    wr0 = wr.astype(BF16)
    wr1 = (wr - wr0.astype(F32)).astype(BF16)
    wr3 = jnp.stack([wr0, wr1])
    br = jnp.zeros((1, LANES), F32)
    br = br.at[0, 0:N_EXPERTS].set(b_re[l]).at[0, N_EXPERTS:N_EXPERTS + N_GROUPS].set(b_rg[l])
    wgu_b = jnp.concatenate([w_eg[l], w_eu[l]], axis=-1).astype(BF16)
    wd_b = w_ed[l].astype(BF16)
    g0, b0 = r2(ln0_g), r2(ln0_b)

    tq = 256
    tk_s = 256
    u_p = _later_or_same(tq)
    u_s = _later_or_same(tk_s)
    u_n = _later_or_same(ts)

    xp2d = x_prompt.reshape(bp * tp, D_MODEL)
    xs2d = x_sample.reshape(bs * ts, D_MODEL)
    wkt = w_in[l][:, SBA_W:2 * SBA_W].T.astype(BF16)
    wvt = w_in[l][:, 2 * SBA_W:3 * SBA_W].T.astype(BF16)
    qp, ktbp, vtbp, ktp, vtp, glup, mqp = _in_proj_t(xp2d, g0, b0, w_in_b, wkt, wvt, bp, tp, 512, tq)
    qs, kbs, vbs, ks, vs, glus, mqs = _in_proj(xs2d, g0, b0, w_in_b, 512)

    sba_p = _sba_prompt(qp, ktbp, vtbp, u_p, bp, tp, tq, 4)
    kc = jnp.transpose(cache_sba_k[l], (0, 2, 3, 1)).reshape(bs, SBA_W, past)
    vc = jnp.transpose(cache_sba_v[l], (0, 2, 3, 1)).reshape(bs, SBA_W, past)
    sba_s = _sba_sample(qs, kbs, vbs, kc, vc, u_s, u_n, bs, ts, tk_s, 4)

    conv_w = (w_dw[l], r2(b_dw[l]), r2(lnc_g[l]), r2(lnc_b[l]), w_cpw_b)
    h0_p = jnp.zeros((bp, HALO, CONV_W), F32)
    h0_s = jnp.pad(cache_conv[l], ((0, 0), (HALO - (CONV_K - 1), 0), (0, 0)))
    conv_p = _conv_module(h0_p, glup, *conv_w, bp, tp, 256)
    conv_s = _conv_module(h0_s, glus, *conv_w, bs, ts, ts)

    mkt, mvt = _mem_kv(mem_prompt.reshape(bp * N_MEM, D_MODEL), w_mk[l].T.astype(BF16),
                       w_mv[l].T.astype(BF16), bp)
    mem_p = _mem_attn(mqp, mkt, mvt, bp, tp, 512)
    mem_s = _mem_attn(mqs, jnp.transpose(cache_mem_k[l], (0, 2, 3, 1)).reshape(bs, MEM_W, N_MEM),
                      jnp.transpose(cache_mem_v[l], (0, 2, 3, 1)).reshape(bs, MEM_W, N_MEM), bs, ts, ts)

    post_w = (g0, b0, w_out_b, r2(ln1_g[l]), r2(ln1_b[l]), wr3, br)
    x1p, routep = _post(xp2d, sba_p, conv_p, mem_p, *post_w, 512)
    x1s, routes = _post(xs2d, sba_s, conv_s, mem_s, *post_w, 512)
    x1 = jnp.concatenate([x1p, x1s], axis=0)
    n_tok = x1.shape[0]
    y2 = _moe(x1, _moe_plan(jnp.concatenate([routep, routes], axis=0), n_tok), wgu_b, wd_b)
    g2, b2 = r2(ln2_g[l]), r2(ln2_b[l])
    yp = _combine(x1, y2, g2, b2, 0, bp * tp, 512)
    ys = _combine(x1, y2, g2, b2, bp * tp, bs * ts, 512)

    hd = (SBA_HEADS, HEAD_DIM)
    glup3 = glup.reshape(bp, tp, CONV_W)
    glus3 = glus.reshape(bs, ts, CONV_W)
    conv_tail_s = jnp.concatenate([cache_conv[l], glus3], axis=1)[:, -(CONV_K - 1):]
    return (
        yp.reshape(bp, tp, D_MODEL),
        ys.reshape(bs, ts, D_MODEL),
        jnp.transpose(ktp.reshape(bp, *hd, tp), (0, 3, 1, 2))[None],
        jnp.transpose(vtp.reshape(bp, *hd, tp), (0, 3, 1, 2))[None],
        glup3[:, -(CONV_K - 1):][None],
        jnp.transpose(mkt.reshape(bp, MEM_HEADS, HEAD_DIM, N_MEM), (0, 3, 1, 2))[None],
        jnp.transpose(mvt.reshape(bp, MEM_HEADS, HEAD_DIM, N_MEM), (0, 3, 1, 2))[None],
        ks.reshape(1, bs, ts, *hd),
        vs.reshape(1, bs, ts, *hd),
        conv_tail_s[None],
    )
```

```python
import functools

import jax
import jax.numpy as jnp
from jax import lax
from jax.experimental import pallas as pl
from jax.experimental.pallas import tpu as pltpu

F32 = jnp.float32
BF16 = jnp.bfloat16

D_MODEL = 1024
HEAD_DIM = 64
SBA_HEADS = 8
SBA_W = SBA_HEADS * HEAD_DIM
CONV_W = 256
CONV_K = 31
MEM_HEADS = 4
MEM_W = MEM_HEADS * HEAD_DIM
N_MEM = 256
N_GROUPS = 4
EXPERTS_PER_GROUP = 8
N_EXPERTS = N_GROUPS * EXPERTS_PER_GROUP
D_EXPERT = 256
DEPTH = 1
DEEPNORM_ALPHA = (2 * DEPTH) ** 0.25
QK_SCALE = HEAD_DIM ** -0.5
LN_EPS = 1e-5
LOG2E = 1.4426950408889634

LANES = 128
SUBLANES = 8
HALO = 32
MOE_TM = 256
MOE_UNROLL = 8
SP_LINEAR = 100.0
VMEM_LIMIT = 48 * 1024 * 1024


def _cparams(sem):
    return pltpu.CompilerParams(dimension_semantics=sem, vmem_limit_bytes=VMEM_LIMIT)


def _ln(x, g, b):
    mu = jnp.mean(x, axis=-1, keepdims=True)
    xc = x - mu
    var = jnp.mean(xc * xc, axis=-1, keepdims=True)
    return xc * lax.rsqrt(var + LN_EPS) * g + b


def _dot(a, b):
    return jnp.dot(a, b, preferred_element_type=F32)


def _dot_nt(a, b):
    return lax.dot_general(a, b, (((1,), (1,)), ((), ())), preferred_element_type=F32)


def _in_proj_kernel(x_ref, g_ref, b_ref, w_ref, q_ref, kb_ref, vb_ref, k_ref, v_ref, glu_ref, mq_ref):
    xn = _ln(x_ref[...], g_ref[...], b_ref[...]).astype(BF16)
    q = _dot(xn, w_ref[:, 0:SBA_W])
    q_ref[...] = (q * (QK_SCALE * LOG2E)).astype(BF16)
    k = _dot(xn, w_ref[:, SBA_W:2 * SBA_W])
    k_ref[...] = k
    kb_ref[...] = k.astype(BF16)
    v = _dot(xn, w_ref[:, 2 * SBA_W:3 * SBA_W])
    v_ref[...] = v
    vb_ref[...] = v.astype(BF16)
    c0 = 3 * SBA_W
    cv = _dot(xn, w_ref[:, c0:c0 + CONV_W])
    cg = _dot(xn, w_ref[:, c0 + CONV_W:c0 + 2 * CONV_W])
    glu_ref[...] = cv * jax.nn.sigmoid(cg)
    mq = _dot(xn, w_ref[:, c0 + 2 * CONV_W:c0 + 2 * CONV_W + MEM_W])
    mq_ref[...] = (mq * QK_SCALE).astype(BF16)


def _in_proj_t_kernel(x_ref, g_ref, b_ref, w_ref, wkt_ref, wvt_ref,
                      q_ref, ktb_ref, vtb_ref, kt_ref, vt_ref, glu_ref, mq_ref, *, tq):
    xn = _ln(x_ref[...], g_ref[...], b_ref[...]).astype(BF16)
    tm = xn.shape[0]
    q = _dot(xn, w_ref[:, 0:SBA_W])
    q_ref[...] = (q * (QK_SCALE * LOG2E)).astype(BF16)
    for wt_ref, t_ref, tb_ref in ((wkt_ref, kt_ref, ktb_ref), (wvt_ref, vt_ref, vtb_ref)):
        t = _dot_nt(wt_ref[...], xn)
        t_ref[0] = t
        for i in range(tm // tq):
            tb_ref[0, i] = t[:, i * tq:(i + 1) * tq].astype(BF16)
    c0 = 3 * SBA_W
    cv = _dot(xn, w_ref[:, c0:c0 + CONV_W])
    cg = _dot(xn, w_ref[:, c0 + CONV_W:c0 + 2 * CONV_W])
    glu_ref[...] = cv * jax.nn.sigmoid(cg)
    mq = _dot(xn, w_ref[:, c0 + 2 * CONV_W:c0 + 2 * CONV_W + MEM_W])
    mq_ref[...] = (mq * QK_SCALE).astype(BF16)


def _in_proj_t(x2d, g, b, w_bf16, wkt, wvt, batch, seq, tm, tq):
    n = x2d.shape[0]
    in_w = w_bf16.shape[1]
    nt = seq // tm
    row = lambda w: pl.BlockSpec((tm, w), lambda bi, i: (bi * nt + i, 0))
    full = lambda s: pl.BlockSpec(s, lambda bi, i: (0, 0))
    tr = pl.BlockSpec((1, SBA_W, tm), lambda bi, i: (bi, 0, i))
    trb = pl.BlockSpec((1, tm // tq, SBA_W, tq), lambda bi, i: (bi, i, 0, 0))
    return pl.pallas_call(
        functools.partial(_in_proj_t_kernel, tq=tq),
        grid=(batch, nt),
        in_specs=[row(D_MODEL), full((1, D_MODEL)), full((1, D_MODEL)), full((D_MODEL, in_w)),
                  full((SBA_W, D_MODEL)), full((SBA_W, D_MODEL))],
        out_specs=[row(SBA_W), trb, trb, tr, tr, row(CONV_W), row(MEM_W)],
        out_shape=[
            jax.ShapeDtypeStruct((n, SBA_W), BF16),
            jax.ShapeDtypeStruct((batch, seq // tq, SBA_W, tq), BF16),
            jax.ShapeDtypeStruct((batch, seq // tq, SBA_W, tq), BF16),
            jax.ShapeDtypeStruct((batch, SBA_W, seq), F32),
            jax.ShapeDtypeStruct((batch, SBA_W, seq), F32),
            jax.ShapeDtypeStruct((n, CONV_W), F32),
            jax.ShapeDtypeStruct((n, MEM_W), BF16),
        ],
        compiler_params=_cparams(("parallel", "parallel")),
        name="in_proj_t",
    )(x2d, g, b, w_bf16, wkt, wvt)


def _in_proj(x2d, g, b, w_bf16, tm):
    n = x2d.shape[0]
    in_w = w_bf16.shape[1]
    row = lambda w: pl.BlockSpec((tm, w), lambda i: (i, 0))
    full = lambda s: pl.BlockSpec(s, lambda i: (0, 0))
    return pl.pallas_call(
        _in_proj_kernel,
        grid=(n // tm,),
        in_specs=[row(D_MODEL), full((1, D_MODEL)), full((1, D_MODEL)), full((D_MODEL, in_w))],
        out_specs=[row(SBA_W), row(SBA_W), row(SBA_W), row(SBA_W), row(SBA_W), row(CONV_W), row(MEM_W)],
        out_shape=[
            jax.ShapeDtypeStruct((n, SBA_W), BF16),
            jax.ShapeDtypeStruct((n, SBA_W), BF16),
            jax.ShapeDtypeStruct((n, SBA_W), BF16),
            jax.ShapeDtypeStruct((n, SBA_W), F32),
            jax.ShapeDtypeStruct((n, SBA_W), F32),
            jax.ShapeDtypeStruct((n, CONV_W), F32),
            jax.ShapeDtypeStruct((n, MEM_W), BF16),
        ],
        compiler_params=_cparams(("parallel",)),
        name="in_proj",
    )(x2d, g, b, w_bf16)


def _head_masks(shape):
    lane = lax.broadcasted_iota(jnp.int32, shape, 1)
    return lane < HEAD_DIM


def _stack_heads(q, low):
    zero = jnp.zeros_like(q)
    return jnp.concatenate([jnp.where(low, q, zero), jnp.where(low, zero, q)], axis=0)


def _sba_tile(z, u, mask):
    sp = jnp.where(z > SP_LINEAR, z, jnp.log(1.0 + jnp.exp2(z)) * LOG2E)
    if mask is not None:
        sp = jnp.where(mask, sp, 0.0)
    w = jnp.exp2(z - _dot(sp.astype(BF16), u))
    if mask is not None:
        w = jnp.where(mask, w, 0.0)
    return w.astype(BF16), jnp.sum(sp, axis=1, keepdims=True)


def _sba_prompt_kernel(q_ref, k_ref, v_ref, u_ref, o_ref, *, tq, hp):
    qi = pl.program_id(2)
    low = _head_masks((tq, LANES))
    u = u_ref[...]
    row = lax.broadcasted_iota(jnp.int32, (2 * tq, tq), 0) & (tq - 1)
    col = lax.broadcasted_iota(jnp.int32, (2 * tq, tq), 1)
    qs = [_stack_heads(q_ref[:, p * LANES:(p + 1) * LANES], low) for p in range(hp)]

    def blk(j, carry, mask):
        out = []
        for p in range(hp):
            c, acc = carry[p]
            z = _dot(qs[p], k_ref[0, j, p * LANES:(p + 1) * LANES, :])
            w, rs = _sba_tile(z, u, mask)
            acc = acc + _dot_nt(w, v_ref[0, j, p * LANES:(p + 1) * LANES, :]) * jnp.exp2(-c)
            out.append((c + rs, acc))
        return tuple(out)

    carry = tuple((jnp.zeros((2 * tq, 1), F32), jnp.zeros((2 * tq, LANES), F32)) for _ in range(hp))
    carry = blk(qi, carry, col < row)
    carry = lax.fori_loop(0, qi, lambda t, cr: blk(qi - 1 - t, cr, None), carry)
    for p in range(hp):
        acc = carry[p][1]
        o_ref[:, p * LANES:(p + 1) * LANES] = jnp.where(low, acc[0:tq], acc[tq:2 * tq]).astype(o_ref.dtype)


def _sba_prompt(q, kt, vt, u, batch, seq, tq, hp):
    n = q.shape[0]
    nq = seq // tq
    groups = SBA_W // (LANES * hp)
    kv = pl.BlockSpec((1, nq, hp * LANES, tq), lambda b, p, i: (b, 0, p, 0))
    return pl.pallas_call(
        functools.partial(_sba_prompt_kernel, tq=tq, hp=hp),
        grid=(batch, groups, nq),
        in_specs=[
            pl.BlockSpec((tq, hp * LANES), lambda b, p, i: (b * nq + i, p)),
            kv, kv,
            pl.BlockSpec((tq, tq), lambda b, p, i: (0, 0)),
        ],
        out_specs=pl.BlockSpec((tq, hp * LANES), lambda b, p, i: (b * nq + i, p)),
        out_shape=jax.ShapeDtypeStruct((n, SBA_W), BF16),
        compiler_params=_cparams(("parallel", "parallel", "arbitrary")),
        name="sba_prompt",
    )(q, kt, vt, u)


def _sba_sample_kernel(q_ref, kn_ref, vn_ref, kc_ref, vc_ref, u_ref, un_ref, o_ref, qs_ref, c_ref, acc_ref,
                       *, tn, tk, nsub):
    j = pl.program_id(1)
    nj = pl.num_programs(1)
    low = _head_masks((tn, LANES))
    pairs = SBA_W // LANES
    rows = 2 * tn

    def step(logits, weighted, u, mask):
        z = jnp.concatenate([logits(qs_ref[p], p) for p in range(pairs)], axis=0)
        w, rs = _sba_tile(z, u, mask)
        pv = jnp.concatenate([weighted(w[p * rows:(p + 1) * rows], p) for p in range(pairs)], axis=0)
        c = c_ref[...]
        acc_ref[...] += pv * jnp.exp2(-c)
        c_ref[...] = c + rs

    def lanes(p):
        return slice(p * LANES, (p + 1) * LANES)

    @pl.when(j == 0)
    def _():
        for p in range(pairs):
            qs_ref[p] = _stack_heads(q_ref[:, lanes(p)], low)
        c_ref[...] = jnp.zeros_like(c_ref)
        acc_ref[...] = jnp.zeros_like(acc_ref)
        row = lax.broadcasted_iota(jnp.int32, (pairs * rows, tn), 0) & (tn - 1)
        col = lax.broadcasted_iota(jnp.int32, (pairs * rows, tn), 1)
        step(lambda q, p: _dot_nt(q, kn_ref[:, lanes(p)]), lambda w, p: _dot(w, vn_ref[:, lanes(p)]),
             un_ref[...], col < row)

    for s in reversed(range(nsub)):
        keys = slice(s * tk, (s + 1) * tk)
        step(lambda q, p: _dot(q, kc_ref[0, lanes(p), keys].astype(BF16)),
             lambda w, p: _dot_nt(w, vc_ref[0, lanes(p), keys].astype(BF16)),
             u_ref[...], None)

    @pl.when(j == nj - 1)
    def _():
        for p in range(pairs):
            a0 = acc_ref[p * rows:p * rows + tn, :]
            a1 = acc_ref[p * rows + tn:(p + 1) * rows, :]
            o_ref[:, lanes(p)] = jnp.where(low, a0, a1).astype(o_ref.dtype)


def _sba_sample(q, kn, vn, kc, vc, u, un, batch, tn, tk, nsub):
    n = q.shape[0]
    past = kc.shape[2]
    tkb = tk * nsub
    nj = past // tkb
    rows = SBA_HEADS * tn
    cache = pl.BlockSpec((1, SBA_W, tkb), lambda b, j: (b, 0, nj - 1 - j))
    return pl.pallas_call(
        functools.partial(_sba_sample_kernel, tn=tn, tk=tk, nsub=nsub),
        grid=(batch, nj),
        in_specs=[
            pl.BlockSpec((tn, SBA_W), lambda b, j: (b, 0)),
            pl.BlockSpec((tn, SBA_W), lambda b, j: (b, 0)),
            pl.BlockSpec((tn, SBA_W), lambda b, j: (b, 0)),
            cache, cache,
            pl.BlockSpec((tk, tk), lambda b, j: (0, 0)),
            pl.BlockSpec((tn, tn), lambda b, j: (0, 0)),
        ],
        out_specs=pl.BlockSpec((tn, SBA_W), lambda b, j: (b, 0)),
        out_shape=jax.ShapeDtypeStruct((n, SBA_W), BF16),
        scratch_shapes=[pltpu.VMEM((SBA_W // LANES, 2 * tn, LANES), BF16),
                        pltpu.VMEM((rows, 1), F32), pltpu.VMEM((rows, LANES), F32)],
        compiler_params=_cparams(("parallel", "arbitrary")),
        name="sba_sample",
    )(q, kn, vn, kc, vc, u, un)


def _conv_kernel(h0_ref, glu_ref, wdw_ref, bdw_ref, g_ref, b_ref, wpw_ref, o_ref, buf_ref, *, tt, rc):
    ti = pl.program_id(1)

    @pl.when(ti == 0)
    def _():
        buf_ref[0:HALO, :] = h0_ref[0]

    @pl.when(ti > 0)
    def _():
        buf_ref[0:HALO, :] = buf_ref[tt:tt + HALO, :]

    buf_ref[HALO:HALO + tt, :] = glu_ref[...]
    base = HALO - (CONV_K - 1)
    wdw = wdw_ref[...]
    for r0 in range(0, tt, rc):
        acc = jnp.zeros((rc, CONV_W), F32)
        for kk in range(CONV_K):
            acc = acc + buf_ref[pl.ds(base + r0 + kk, rc), :] * wdw[kk:kk + 1, :]
        u = _ln(acc + bdw_ref[...], g_ref[...], b_ref[...])
        s = u * jax.nn.sigmoid(u)
        o_ref[pl.ds(r0, rc), :] = _dot(s.astype(BF16), wpw_ref[...]).astype(o_ref.dtype)


def _conv_module(h0, glu, w_dw, b_dw, g, b, w_pw_bf16, batch, seq, tt):
    n = glu.shape[0]
    nt = seq // tt
    rc = min(tt, 64)
    full = lambda s: pl.BlockSpec(s, lambda bi, ti: (0,) * len(s))
    return pl.pallas_call(
        functools.partial(_conv_kernel, tt=tt, rc=rc),
        grid=(batch, nt),
        in_specs=[
            pl.BlockSpec((1, HALO, CONV_W), lambda bi, ti: (bi, 0, 0)),
            pl.BlockSpec((tt, CONV_W), lambda bi, ti: (bi * nt + ti, 0)),
            full((CONV_K, CONV_W)), full((1, CONV_W)), full((1, CONV_W)), full((1, CONV_W)),
            full((CONV_W, CONV_W)),
        ],
        out_specs=pl.BlockSpec((tt, CONV_W), lambda bi, ti: (bi * nt + ti, 0)),
        out_shape=jax.ShapeDtypeStruct((n, CONV_W), BF16),
        scratch_shapes=[pltpu.VMEM((HALO + tt, CONV_W), F32)],
        compiler_params=_cparams(("parallel", "arbitrary")),
        name="conv_module",
    )(h0, glu, w_dw, b_dw, g, b, w_pw_bf16)


def _mem_kv_kernel(m_ref, wkt_ref, wvt_ref, k_ref, v_ref):
    m = m_ref[...].astype(BF16)
    k_ref[0] = _dot_nt(wkt_ref[...], m)
    v_ref[0] = _dot_nt(wvt_ref[...], m)


def _mem_kv(mem2d, wkt, wvt, batch):
    full = pl.BlockSpec((MEM_W, D_MODEL), lambda i: (0, 0))
    out = pl.BlockSpec((1, MEM_W, N_MEM), lambda i: (i, 0, 0))
    return pl.pallas_call(
        _mem_kv_kernel,
        grid=(batch,),
        in_specs=[pl.BlockSpec((N_MEM, D_MODEL), lambda i: (i, 0)), full, full],
        out_specs=[out, out],
        out_shape=[jax.ShapeDtypeStruct((batch, MEM_W, N_MEM), F32)] * 2,
        compiler_params=_cparams(("parallel",)),
        name="mem_kv",
    )(mem2d, wkt, wvt)


def _mem_attn_kernel(q_ref, k_ref, v_ref, o_ref):
    q = q_ref[...]
    tq = q.shape[0]
    low = _head_masks((tq, LANES))
    for p in range(MEM_W // LANES):
        qs = _stack_heads(q[:, p * LANES:(p + 1) * LANES], low)
        kt = k_ref[0, p * LANES:(p + 1) * LANES, :].astype(BF16)
        vt = v_ref[0, p * LANES:(p + 1) * LANES, :].astype(BF16)
        s = _dot(qs, kt)
        e = jnp.exp(s - jnp.max(s, axis=-1, keepdims=True))
        o = _dot_nt(e.astype(BF16), vt) / jnp.sum(e, axis=-1, keepdims=True)
        o_ref[:, p * LANES:(p + 1) * LANES] = jnp.where(low, o[0:tq], o[tq:2 * tq]).astype(o_ref.dtype)


def _mem_attn(mq, mkt, mvt, batch, seq, tq):
    n = mq.shape[0]
    nq = seq // tq
    kv = pl.BlockSpec((1, MEM_W, N_MEM), lambda b, i: (b, 0, 0))
    return pl.pallas_call(
        _mem_attn_kernel,
        grid=(batch, nq),
        in_specs=[pl.BlockSpec((tq, MEM_W), lambda b, i: (b * nq + i, 0)), kv, kv],
        out_specs=pl.BlockSpec((tq, MEM_W), lambda b, i: (b * nq + i, 0)),
        out_shape=jax.ShapeDtypeStruct((n, MEM_W), BF16),
        compiler_params=_cparams(("parallel", "parallel")),
        name="mem_attn",
    )(mq, mkt, mvt)


def _store_token_tiles(ref, x):
    rows = x.shape[0]
    for c in range(SUBLANES):
        ref[pl.ds(c, rows, stride=SUBLANES), :] = x[:, c * LANES:(c + 1) * LANES]


def _load_token_tiles(ref):
    rows = ref.shape[0] // SUBLANES
    return jnp.concatenate([ref[pl.ds(c, rows, stride=SUBLANES), :] for c in range(SUBLANES)], axis=1)


def _split2(x):
    a = x.astype(BF16)
    b = (x - a.astype(F32)).astype(BF16)
    return a, b


def _post_body(x_ref, sba_ref, conv_ref, mem_ref, g0_ref, b0_ref, wo_ref, g1_ref, b1_ref,
               wr_ref, br_ref, x1_ref, route_ref):
    xn = _ln(x_ref[...], g0_ref[...], b0_ref[...])
    mix = _dot(sba_ref[...], wo_ref[0:SBA_W, :])
    mix = mix + _dot(conv_ref[...], wo_ref[SBA_W:SBA_W + CONV_W, :])
    mix = mix + _dot(mem_ref[...], wo_ref[SBA_W + CONV_W:SBA_W + CONV_W + MEM_W, :])
    x1 = _ln(DEEPNORM_ALPHA * xn + mix, g1_ref[...], b1_ref[...])
    _store_token_tiles(x1_ref, x1)

    a0, a1 = _split2(x1)
    w0, w1 = wr_ref[0], wr_ref[1]
    logits = _dot(a0, w0) + (_dot(a0, w1) + _dot(a1, w0)) + br_ref[...]
    tm = logits.shape[0]
    lane = lax.broadcasted_iota(jnp.int32, (tm, LANES), 1).astype(F32)
    neg = jnp.float32(-jnp.inf)
    big = jnp.float32(LANES)
    is_g = jnp.logical_and(lane >= N_EXPERTS, lane < N_EXPERTS + N_GROUPS)
    gl = jnp.where(is_g, logits, neg)
    gmax = jnp.max(gl, axis=-1, keepdims=True)
    g_idx = jnp.min(jnp.where(gl == gmax, lane, big), axis=-1, keepdims=True) - N_EXPERTS
    g_w = 1.0 / jnp.sum(jnp.exp(gl - gmax), axis=-1, keepdims=True)
    in_grp = jnp.logical_and(lane >= g_idx * EXPERTS_PER_GROUP, lane < (g_idx + 1.0) * EXPERTS_PER_GROUP)
    el = jnp.where(in_grp, logits, neg)
    v1 = jnp.max(el, axis=-1, keepdims=True)
    i1 = jnp.min(jnp.where(el == v1, lane, big), axis=-1, keepdims=True)
    el2 = jnp.where(lane == i1, neg, el)
    v2 = jnp.max(el2, axis=-1, keepdims=True)
    i2 = jnp.min(jnp.where(el2 == v2, lane, big), axis=-1, keepdims=True)
    e2 = jnp.exp(v2 - v1)
    p1 = 1.0 / (1.0 + e2)
    p2 = e2 / (1.0 + e2)
    route_ref[...] = jnp.where(lane == 0.0, i1, jnp.where(lane == 1.0, i2, jnp.where(
        lane == 2.0, p1 * g_w, jnp.where(lane == 3.0, p2 * g_w, 0.0))))


def _post_kernel(xp_ref, xs_ref, sbap_ref, sbas_ref, convp_ref, convs_ref, memp_ref, mems_ref,
                 g0_ref, b0_ref, wo_ref, g1_ref, b1_ref, wr_ref, br_ref, x1_ref, route_ref, *, prompt_tiles):
    i = pl.program_id(0)
    shared = (g0_ref, b0_ref, wo_ref, g1_ref, b1_ref, wr_ref, br_ref, x1_ref, route_ref)

    @pl.when(i < prompt_tiles)
    def _():
        _post_body(xp_ref, sbap_ref, convp_ref, memp_ref, *shared)

    @pl.when(i >= prompt_tiles)
    def _():
        _post_body(xs_ref, sbas_ref, convs_ref, mems_ref, *shared)


def _post(xp, xs, sbap, sbas, convp, convs, memp, mems, g0, b0, wo_bf16, g1, b1, wr3, br, tm):
    pt = xp.shape[0] // tm
    st = xs.shape[0] // tm
    n = xp.shape[0] + xs.shape[0]
    prow = lambda w: pl.BlockSpec((tm, w), lambda i: (jnp.minimum(i, pt - 1), 0))
    srow = lambda w: pl.BlockSpec((tm, w), lambda i: (jnp.maximum(i - pt, 0), 0))
    full = lambda s: pl.BlockSpec(s, lambda i: (0,) * len(s))
    widths = (D_MODEL, SBA_W, CONV_W, MEM_W)
    return pl.pallas_call(
        functools.partial(_post_kernel, prompt_tiles=pt),
        grid=(pt + st,),
        in_specs=[spec(w) for w in widths for spec in (prow, srow)] + [
            full((1, D_MODEL)), full((1, D_MODEL)), full((D_MODEL, D_MODEL)),
            full((1, D_MODEL)), full((1, D_MODEL)),
            full((2, D_MODEL, LANES)), full((1, LANES))],
        out_specs=[pl.BlockSpec((tm * SUBLANES, LANES), lambda i: (i, 0)),
                   pl.BlockSpec((tm, LANES), lambda i: (i, 0))],
        out_shape=[jax.ShapeDtypeStruct((n * SUBLANES, LANES), F32),
                   jax.ShapeDtypeStruct((n, LANES), F32)],
        compiler_params=_cparams(("arbitrary",)),
        name="post",
    )(xp, xs, sbap, sbas, convp, convs, memp, mems, g0, b0, wo_bf16, g1, b1, wr3, br)


def _moe_plan(route, n):
    pairs = 2 * n
    tiles = pairs // MOE_TM + N_EXPERTS
    e = jnp.concatenate([route[:, 0], route[:, 1]]).astype(jnp.int32)
    order = jnp.argsort(e, stable=True).astype(jnp.int32)
    ids = jnp.arange(N_EXPERTS, dtype=jnp.int32)
    counts = jnp.sum((e[:, None] == ids[None, :]).astype(jnp.int32), axis=0)
    ntile = (counts + MOE_TM - 1) // MOE_TM
    tile_end = jnp.cumsum(ntile)
    tile_start = tile_end - ntile
    first = jnp.cumsum(counts) - counts
    n_active = tile_end[-1]
    t = jnp.arange(tiles, dtype=jnp.int32)
    te = jnp.minimum(jnp.sum((t[:, None] >= tile_end[None, :]).astype(jnp.int32), axis=1), N_EXPERTS - 1)
    done = (t - tile_start[te]) * MOE_TM
    n_valid = jnp.where(t < n_active, jnp.clip(counts[te] - done, 0, MOE_TM), 0)
    r = jnp.arange(MOE_TM, dtype=jnp.int32)
    valid = r[None, :] < n_valid[:, None]
    pos = jnp.clip(first[te][:, None] + done[:, None] + r[None, :], 0, pairs - 1)
    pair = order[pos]
    src = jnp.where(valid, jnp.where(pair >= n, pair - n, pair), 0)
    dst = jnp.where(valid, pair, pairs + r[None, :])
    as_rows = lambda a: (a * SUBLANES).astype(jnp.int32).reshape(tiles, 1, MOE_TM)
    return te.astype(jnp.int32), n_active.reshape(1).astype(jnp.int32), as_rows(src), as_rows(dst)


def _moe_kernel(te_ref, na_ref, src0_ref, src1_ref, dst_ref, x_hbm, wg_ref, wu_ref, wd_ref,
                y_hbm, xg_ref, yo_ref, wgu_s, wd_s, gsem, ssem):
    t = pl.program_id(0)
    n_active = na_ref[0]
    tile_rows = MOE_TM * SUBLANES

    def token(ref, row0):
        return ref.at[pl.ds(pl.multiple_of(row0, SUBLANES), SUBLANES)]

    def start_gather(src_ref, slot):
        def body(i, carry):
            for k in range(MOE_UNROLL):
                r = i * MOE_UNROLL + k
                pltpu.make_async_copy(token(x_hbm, src_ref[0, 0, r]), token(xg_ref.at[slot], r * SUBLANES),
                                      gsem.at[slot]).start(priority=k % 2)
            return carry
        lax.fori_loop(0, MOE_TM // MOE_UNROLL, body, 0)

    def wait_gather(slot):
        pltpu.make_async_copy(x_hbm.at[pl.ds(0, tile_rows)], xg_ref.at[slot], gsem.at[slot]).wait()

    def wait_scatter():
        pltpu.make_async_copy(yo_ref, y_hbm.at[pl.ds(0, tile_rows)], ssem.at[0]).wait()

    @pl.when(t < n_active)
    def _():
        slot = lax.rem(t, 2)

        @pl.when(t == 0)
        def _():
            start_gather(src0_ref, 0)
            yo_ref[...] = jnp.zeros_like(yo_ref)
            sink = pltpu.make_async_copy(yo_ref, y_hbm.at[pl.ds(y_hbm.shape[0] - tile_rows, tile_rows)],
                                         ssem.at[0])
            sink.start()
            sink.wait()

        @pl.when(t + 1 < n_active)
        def _():
            start_gather(src1_ref, 1 - slot)

        @pl.when(jnp.logical_or(t == 0, te_ref[t] != te_ref[jnp.maximum(t - 1, 0)]))
        def _():
            wgu_s[:, 0:D_EXPERT] = wg_ref[0].astype(BF16)
            wgu_s[:, D_EXPERT:2 * D_EXPERT] = wu_ref[0].astype(BF16)
            wd_s[...] = wd_ref[0].astype(BF16)

        wait_gather(slot)
        x = _load_token_tiles(xg_ref.at[slot]).astype(BF16)
        a = _dot(x, wgu_s[:, 0:D_EXPERT])
        u = _dot(x, wgu_s[:, D_EXPERT:2 * D_EXPERT])
        hid = (a * jax.nn.sigmoid(a)) * u
        y = _dot(hid.astype(BF16), wd_s[...])

        @pl.when(t > 0)
        def _():
            wait_scatter()

        _store_token_tiles(yo_ref, y)

        def body(i, carry):
            for k in range(MOE_UNROLL):
                r = i * MOE_UNROLL + k
                pltpu.make_async_copy(token(yo_ref, r * SUBLANES), token(y_hbm, dst_ref[0, 0, r]),
                                      ssem.at[0]).start(priority=k % 2)
            return carry
        lax.fori_loop(0, MOE_TM // MOE_UNROLL, body, 0)

        @pl.when(t == n_active - 1)
        def _():
            wait_scatter()


def _moe(x1, plan, w_eg, w_eu, w_ed):
    te, n_active, src, dst = plan
    tiles = src.shape[0]
    tile_rows = MOE_TM * SUBLANES
    idx = lambda f: pl.BlockSpec((1, 1, MOE_TM), f, memory_space=pltpu.SMEM)
    expert = lambda a, b: pl.BlockSpec((1, a, b), lambda t, te, na: (te[t], 0, 0))
    grid_spec = pltpu.PrefetchScalarGridSpec(
        num_scalar_prefetch=2,
        grid=(tiles,),
        in_specs=[
            idx(lambda t, te, na: (t, 0, 0)),
            idx(lambda t, te, na: (jnp.minimum(t + 1, tiles - 1), 0, 0)),
            idx(lambda t, te, na: (t, 0, 0)),
            pl.BlockSpec(memory_space=pl.ANY),
            expert(D_MODEL, D_EXPERT), expert(D_MODEL, D_EXPERT), expert(D_EXPERT, D_MODEL),
        ],
        out_specs=pl.BlockSpec(memory_space=pl.ANY),
        scratch_shapes=[pltpu.VMEM((2, tile_rows, LANES), F32), pltpu.VMEM((tile_rows, LANES), F32),
                        pltpu.VMEM((D_MODEL, 2 * D_EXPERT), BF16), pltpu.VMEM((D_EXPERT, D_MODEL), BF16),
                        pltpu.SemaphoreType.DMA((2,)), pltpu.SemaphoreType.DMA((1,))],
    )
    return pl.pallas_call(
        _moe_kernel,
        grid_spec=grid_spec,
        out_shape=jax.ShapeDtypeStruct((2 * x1.shape[0] + tile_rows, LANES), F32),
        compiler_params=_cparams(("arbitrary",)),
        name="moe",
    )(te, n_active, src, src, dst, x1, w_eg, w_eu, w_ed)


def _combine_kernel(x1_ref, ya_ref, yb_ref, route_ref, g_ref, b_ref, o_ref):
    route = route_ref[...]
    f = route[:, 2:3] * _load_token_tiles(ya_ref) + route[:, 3:4] * _load_token_tiles(yb_ref)
    o_ref[...] = _ln(DEEPNORM_ALPHA * _load_token_tiles(x1_ref) + f, g_ref[...], b_ref[...])


def _combine(x1, y2, route, g2, b2, row0, rows, tm):
    n = route.shape[0]
    b0 = row0 // tm
    tiles = lambda off: pl.BlockSpec((tm * SUBLANES, LANES), lambda i: (i + off, 0))
    full = pl.BlockSpec((1, D_MODEL), lambda i: (0, 0))
    return pl.pallas_call(
        _combine_kernel,
        grid=(rows // tm,),
        in_specs=[tiles(b0), tiles(b0), tiles(b0 + n // tm),
                  pl.BlockSpec((tm, LANES), lambda i: (i + b0, 0)), full, full],
        out_specs=pl.BlockSpec((tm, D_MODEL), lambda i: (i, 0)),
        out_shape=jax.ShapeDtypeStruct((rows, D_MODEL), F32),
        compiler_params=_cparams(("parallel",)),
        name="combine",
    )(x1, y2, y2, route, g2, b2)


def _later_or_same(n):
    r = lax.broadcasted_iota(jnp.int32, (n, n), 0)
    c = lax.broadcasted_iota(jnp.int32, (n, n), 1)
    return (r >= c).astype(BF16)


def kernel(x_prompt, x_sample, mem_prompt, cache_sba_k, cache_sba_v, cache_conv, cache_mem_k, cache_mem_v,
           ln0_g, ln0_b, w_in, w_dw, b_dw, lnc_g, lnc_b, w_cpw, w_mk, w_mv, w_out, ln1_g, ln1_b,
           w_rg, b_rg, w_re, b_re, w_eg, w_eu, w_ed, ln2_g, ln2_b):
    bp, tp, _ = x_prompt.shape
    bs, ts, _ = x_sample.shape
    past = cache_sba_k.shape[2]
    l = 0
    r2 = lambda a: a.reshape(1, -1)

    w_in_b = w_in[l].astype(BF16)
    w_out_b = w_out[l].astype(BF16)
    w_cpw_b = w_cpw[l].astype(BF16)
    wr = jnp.zeros((D_MODEL, LANES), F32)
    wr = wr.at[:, 0:N_EXPERTS].set(w_re[l]).at[:, N_EXPERTS:N_EXPERTS + N_GROUPS].set(w_rg[l])
    wr0 = wr.astype(BF16)
    wr1 = (wr - wr0.astype(F32)).astype(BF16)
    wr3 = jnp.stack([wr0, wr1])
    br = jnp.zeros((1, LANES), F32)
    br = br.at[0, 0:N_EXPERTS].set(b_re[l]).at[0, N_EXPERTS:N_EXPERTS + N_GROUPS].set(b_rg[l])
    g0, b0 = r2(ln0_g), r2(ln0_b)

    tq = 256
    tk_s = 256
    u_p = _later_or_same(tq)
    u_s = _later_or_same(tk_s)
    u_n = _later_or_same(ts)

    xp2d = x_prompt.reshape(bp * tp, D_MODEL)
    xs2d = x_sample.reshape(bs * ts, D_MODEL)
    wkt = w_in[l][:, SBA_W:2 * SBA_W].T.astype(BF16)
    wvt = w_in[l][:, 2 * SBA_W:3 * SBA_W].T.astype(BF16)
    qp, ktbp, vtbp, ktp, vtp, glup, mqp = _in_proj_t(xp2d, g0, b0, w_in_b, wkt, wvt, bp, tp, 512, tq)
    qs, kbs, vbs, ks, vs, glus, mqs = _in_proj(xs2d, g0, b0, w_in_b, 512)

    sba_p = _sba_prompt(qp, ktbp, vtbp, u_p, bp, tp, tq, 4)
    kc = jnp.transpose(cache_sba_k[l], (0, 2, 3, 1)).reshape(bs, SBA_W, past)
    vc = jnp.transpose(cache_sba_v[l], (0, 2, 3, 1)).reshape(bs, SBA_W, past)
    sba_s = _sba_sample(qs, kbs, vbs, kc, vc, u_s, u_n, bs, ts, tk_s, 4)

    conv_w = (w_dw[l], r2(b_dw[l]), r2(lnc_g[l]), r2(lnc_b[l]), w_cpw_b)
    h0_p = jnp.zeros((bp, HALO, CONV_W), F32)
    h0_s = jnp.pad(cache_conv[l], ((0, 0), (HALO - (CONV_K - 1), 0), (0, 0)))
    conv_p = _conv_module(h0_p, glup, *conv_w, bp, tp, 256)
    conv_s = _conv_module(h0_s, glus, *conv_w, bs, ts, ts)

    mkt, mvt = _mem_kv(mem_prompt.reshape(bp * N_MEM, D_MODEL), w_mk[l].T.astype(BF16),
                       w_mv[l].T.astype(BF16), bp)
    mem_p = _mem_attn(mqp, mkt, mvt, bp, tp, 512)
    mem_s = _mem_attn(mqs, jnp.transpose(cache_mem_k[l], (0, 2, 3, 1)).reshape(bs, MEM_W, N_MEM),
                      jnp.transpose(cache_mem_v[l], (0, 2, 3, 1)).reshape(bs, MEM_W, N_MEM), bs, ts, ts)

    post_w = (g0, b0, w_out_b, r2(ln1_g[l]), r2(ln1_b[l]), wr3, br)
    x1, route = _post(xp2d, xs2d, sba_p, sba_s, conv_p, conv_s, mem_p, mem_s, *post_w, 512)
    y2 = _moe(x1, _moe_plan(route, route.shape[0]), w_eg[l], w_eu[l], w_ed[l])
    g2, b2 = r2(ln2_g[l]), r2(ln2_b[l])
    yp = _combine(x1, y2, route, g2, b2, 0, bp * tp, 512)
    ys = _combine(x1, y2, route, g2, b2, bp * tp, bs * ts, 512)

    hd = (SBA_HEADS, HEAD_DIM)
    glup3 = glup.reshape(bp, tp, CONV_W)
    glus3 = glus.reshape(bs, ts, CONV_W)
    conv_tail_s = jnp.concatenate([cache_conv[l], glus3], axis=1)[:, -(CONV_K - 1):]
    return (
        yp.reshape(bp, tp, D_MODEL),
        ys.reshape(bs, ts, D_MODEL),
        jnp.transpose(ktp.reshape(bp, *hd, tp), (0, 3, 1, 2))[None],
        jnp.transpose(vtp.reshape(bp, *hd, tp), (0, 3, 1, 2))[None],
        glup3[:, -(CONV_K - 1):][None],
        jnp.transpose(mkt.reshape(bp, MEM_HEADS, HEAD_DIM, N_MEM), (0, 3, 1, 2))[None],
        jnp.transpose(mvt.reshape(bp, MEM_HEADS, HEAD_DIM, N_MEM), (0, 3, 1, 2))[None],
        ks.reshape(1, bs, ts, *hd),
        vs.reshape(1, bs, ts, *hd),
        conv_tail_s[None],
    )
```

```python
import functools

import jax
import jax.numpy as jnp
from jax import lax
from jax.experimental import pallas as pl
from jax.experimental.pallas import tpu as pltpu

F32 = jnp.float32
BF16 = jnp.bfloat16

D_MODEL = 1024
HEAD_DIM = 64
SBA_HEADS = 8
SBA_W = SBA_HEADS * HEAD_DIM
CONV_W = 256
CONV_K = 31
MEM_HEADS = 4
MEM_W = MEM_HEADS * HEAD_DIM
N_MEM = 256
N_GROUPS = 4
EXPERTS_PER_GROUP = 8
N_EXPERTS = N_GROUPS * EXPERTS_PER_GROUP
D_EXPERT = 256
DEPTH = 1
DEEPNORM_ALPHA = (2 * DEPTH) ** 0.25
QK_SCALE = HEAD_DIM ** -0.5
LN_EPS = 1e-5
LOG2E = 1.4426950408889634

LANES = 128
SUBLANES = 8
HALO = 32
MOE_TM = 1024
MOE_UNROLL = 8
SP_LINEAR = 100.0
VMEM_LIMIT = 48 * 1024 * 1024


def _cparams(sem):
    return pltpu.CompilerParams(dimension_semantics=sem, vmem_limit_bytes=VMEM_LIMIT)


def _ln(x, g, b):
    mu = jnp.mean(x, axis=-1, keepdims=True)
    xc = x - mu
    var = jnp.mean(xc * xc, axis=-1, keepdims=True)
    return xc * lax.rsqrt(var + LN_EPS) * g + b


def _dot(a, b):
    return jnp.dot(a, b, preferred_element_type=F32)


def _dot_nt(a, b):
    return lax.dot_general(a, b, (((1,), (1,)), ((), ())), preferred_element_type=F32)


def _in_proj_kernel(x_ref, g_ref, b_ref, w_ref, q_ref, kb_ref, vb_ref, k_ref, v_ref, glu_ref, mq_ref):
    xn = _ln(x_ref[...], g_ref[...], b_ref[...]).astype(BF16)
    q = _dot(xn, w_ref[:, 0:SBA_W])
    q_ref[...] = (q * (QK_SCALE * LOG2E)).astype(BF16)
    k = _dot(xn, w_ref[:, SBA_W:2 * SBA_W])
    k_ref[...] = k
    kb_ref[...] = k.astype(BF16)
    v = _dot(xn, w_ref[:, 2 * SBA_W:3 * SBA_W])
    v_ref[...] = v
    vb_ref[...] = v.astype(BF16)
    c0 = 3 * SBA_W
    cv = _dot(xn, w_ref[:, c0:c0 + CONV_W])
    cg = _dot(xn, w_ref[:, c0 + CONV_W:c0 + 2 * CONV_W])
    glu_ref[...] = cv * jax.nn.sigmoid(cg)
    mq = _dot(xn, w_ref[:, c0 + 2 * CONV_W:c0 + 2 * CONV_W + MEM_W])
    mq_ref[...] = (mq * QK_SCALE).astype(BF16)


def _in_proj_t_kernel(x_ref, g_ref, b_ref, w_ref, wkt_ref, wvt_ref,
                      q_ref, ktb_ref, vtb_ref, kt_ref, vt_ref, glu_ref, mq_ref, *, tq):
    xn = _ln(x_ref[...], g_ref[...], b_ref[...]).astype(BF16)
    tm = xn.shape[0]
    q = _dot(xn, w_ref[:, 0:SBA_W])
    q_ref[...] = (q * (QK_SCALE * LOG2E)).astype(BF16)
    for wt_ref, t_ref, tb_ref in ((wkt_ref, kt_ref, ktb_ref), (wvt_ref, vt_ref, vtb_ref)):
        t = _dot_nt(wt_ref[...], xn)
        t_ref[0] = t
        for i in range(tm // tq):
            tb_ref[0, i] = t[:, i * tq:(i + 1) * tq].astype(BF16)
    c0 = 3 * SBA_W
    cv = _dot(xn, w_ref[:, c0:c0 + CONV_W])
    cg = _dot(xn, w_ref[:, c0 + CONV_W:c0 + 2 * CONV_W])
    glu_ref[...] = cv * jax.nn.sigmoid(cg)
    mq = _dot(xn, w_ref[:, c0 + 2 * CONV_W:c0 + 2 * CONV_W + MEM_W])
    mq_ref[...] = (mq * QK_SCALE).astype(BF16)


def _in_proj_t(x2d, g, b, w_bf16, wkt, wvt, batch, seq, tm, tq):
    n = x2d.shape[0]
    in_w = w_bf16.shape[1]
    nt = seq // tm
    row = lambda w: pl.BlockSpec((tm, w), lambda bi, i: (bi * nt + i, 0))
    full = lambda s: pl.BlockSpec(s, lambda bi, i: (0, 0))
    tr = pl.BlockSpec((1, SBA_W, tm), lambda bi, i: (bi, 0, i))
    trb = pl.BlockSpec((1, tm // tq, SBA_W, tq), lambda bi, i: (bi, i, 0, 0))
    return pl.pallas_call(
        functools.partial(_in_proj_t_kernel, tq=tq),
        grid=(batch, nt),
        in_specs=[row(D_MODEL), full((1, D_MODEL)), full((1, D_MODEL)), full((D_MODEL, in_w)),
                  full((SBA_W, D_MODEL)), full((SBA_W, D_MODEL))],
        out_specs=[row(SBA_W), trb, trb, tr, tr, row(CONV_W), row(MEM_W)],
        out_shape=[
            jax.ShapeDtypeStruct((n, SBA_W), BF16),
            jax.ShapeDtypeStruct((batch, seq // tq, SBA_W, tq), BF16),
            jax.ShapeDtypeStruct((batch, seq // tq, SBA_W, tq), BF16),
            jax.ShapeDtypeStruct((batch, SBA_W, seq), F32),
            jax.ShapeDtypeStruct((batch, SBA_W, seq), F32),
            jax.ShapeDtypeStruct((n, CONV_W), F32),
            jax.ShapeDtypeStruct((n, MEM_W), BF16),
        ],
        compiler_params=_cparams(("parallel", "parallel")),
        name="in_proj_t",
    )(x2d, g, b, w_bf16, wkt, wvt)


def _in_proj(x2d, g, b, w_bf16, tm):
    n = x2d.shape[0]
    in_w = w_bf16.shape[1]
    row = lambda w: pl.BlockSpec((tm, w), lambda i: (i, 0))
    full = lambda s: pl.BlockSpec(s, lambda i: (0, 0))
    return pl.pallas_call(
        _in_proj_kernel,
        grid=(n // tm,),
        in_specs=[row(D_MODEL), full((1, D_MODEL)), full((1, D_MODEL)), full((D_MODEL, in_w))],
        out_specs=[row(SBA_W), row(SBA_W), row(SBA_W), row(SBA_W), row(SBA_W), row(CONV_W), row(MEM_W)],
        out_shape=[
            jax.ShapeDtypeStruct((n, SBA_W), BF16),
            jax.ShapeDtypeStruct((n, SBA_W), BF16),
            jax.ShapeDtypeStruct((n, SBA_W), BF16),
            jax.ShapeDtypeStruct((n, SBA_W), F32),
            jax.ShapeDtypeStruct((n, SBA_W), F32),
            jax.ShapeDtypeStruct((n, CONV_W), F32),
            jax.ShapeDtypeStruct((n, MEM_W), BF16),
        ],
        compiler_params=_cparams(("parallel",)),
        name="in_proj",
    )(x2d, g, b, w_bf16)


def _head_masks(shape):
    lane = lax.broadcasted_iota(jnp.int32, shape, 1)
    return lane < HEAD_DIM


def _stack_heads(q, low):
    zero = jnp.zeros_like(q)
    return jnp.concatenate([jnp.where(low, q, zero), jnp.where(low, zero, q)], axis=0)


def _sba_tile(z, u, mask):
    sp = jnp.where(z > SP_LINEAR, z, jnp.log(1.0 + jnp.exp2(z)) * LOG2E)
    if mask is not None:
        sp = jnp.where(mask, sp, 0.0)
    w = jnp.exp2(z - _dot(sp.astype(BF16), u))
    if mask is not None:
        w = jnp.where(mask, w, 0.0)
    return w.astype(BF16), jnp.sum(sp, axis=1, keepdims=True)


def _sba_prompt_kernel(q_ref, k_ref, v_ref, u_ref, o_ref, *, tq, hp):
    qi = pl.program_id(2)
    low = _head_masks((tq, LANES))
    u = u_ref[...]
    row = lax.broadcasted_iota(jnp.int32, (2 * tq, tq), 0) & (tq - 1)
    col = lax.broadcasted_iota(jnp.int32, (2 * tq, tq), 1)
    qs = [_stack_heads(q_ref[:, p * LANES:(p + 1) * LANES], low) for p in range(hp)]

    def blk(j, carry, mask):
        out = []
        for p in range(hp):
            c, acc = carry[p]
            z = _dot(qs[p], k_ref[0, j, p * LANES:(p + 1) * LANES, :])
            w, rs = _sba_tile(z, u, mask)
            acc = acc + _dot_nt(w, v_ref[0, j, p * LANES:(p + 1) * LANES, :]) * jnp.exp2(-c)
            out.append((c + rs, acc))
        return tuple(out)

    carry = tuple((jnp.zeros((2 * tq, 1), F32), jnp.zeros((2 * tq, LANES), F32)) for _ in range(hp))
    carry = blk(qi, carry, col < row)
    carry = lax.fori_loop(0, qi, lambda t, cr: blk(qi - 1 - t, cr, None), carry)
    for p in range(hp):
        acc = carry[p][1]
        o_ref[:, p * LANES:(p + 1) * LANES] = jnp.where(low, acc[0:tq], acc[tq:2 * tq]).astype(o_ref.dtype)


def _sba_prompt(q, kt, vt, u, batch, seq, tq, hp):
    n = q.shape[0]
    nq = seq // tq
    groups = SBA_W // (LANES * hp)
    kv = pl.BlockSpec((1, nq, hp * LANES, tq), lambda b, p, i: (b, 0, p, 0))
    return pl.pallas_call(
        functools.partial(_sba_prompt_kernel, tq=tq, hp=hp),
        grid=(batch, groups, nq),
        in_specs=[
            pl.BlockSpec((tq, hp * LANES), lambda b, p, i: (b * nq + i, p)),
            kv, kv,
            pl.BlockSpec((tq, tq), lambda b, p, i: (0, 0)),
        ],
        out_specs=pl.BlockSpec((tq, hp * LANES), lambda b, p, i: (b * nq + i, p)),
        out_shape=jax.ShapeDtypeStruct((n, SBA_W), BF16),
        compiler_params=_cparams(("parallel", "parallel", "arbitrary")),
        name="sba_prompt",
    )(q, kt, vt, u)


def _sba_sample_kernel(q_ref, kn_ref, vn_ref, kc_ref, vc_ref, u_ref, un_ref, o_ref, qs_ref, c_ref, acc_ref,
                       *, tn, tk, nsub):
    j = pl.program_id(1)
    nj = pl.num_programs(1)
    low = _head_masks((tn, LANES))
    pairs = SBA_W // LANES
    rows = 2 * tn

    def step(logits, weighted, u, mask):
        z = jnp.concatenate([logits(qs_ref[p], p) for p in range(pairs)], axis=0)
        w, rs = _sba_tile(z, u, mask)
        pv = jnp.concatenate([weighted(w[p * rows:(p + 1) * rows], p) for p in range(pairs)], axis=0)
        c = c_ref[...]
        acc_ref[...] += pv * jnp.exp2(-c)
        c_ref[...] = c + rs

    def lanes(p):
        return slice(p * LANES, (p + 1) * LANES)

    @pl.when(j == 0)
    def _():
        for p in range(pairs):
            qs_ref[p] = _stack_heads(q_ref[:, lanes(p)], low)
        c_ref[...] = jnp.zeros_like(c_ref)
        acc_ref[...] = jnp.zeros_like(acc_ref)
        row = lax.broadcasted_iota(jnp.int32, (pairs * rows, tn), 0) & (tn - 1)
        col = lax.broadcasted_iota(jnp.int32, (pairs * rows, tn), 1)
        step(lambda q, p: _dot_nt(q, kn_ref[:, lanes(p)]), lambda w, p: _dot(w, vn_ref[:, lanes(p)]),
             un_ref[...], col < row)

    for s in reversed(range(nsub)):
        keys = slice(s * tk, (s + 1) * tk)
        step(lambda q, p: _dot(q, kc_ref[0, lanes(p), keys].astype(BF16)),
             lambda w, p: _dot_nt(w, vc_ref[0, lanes(p), keys].astype(BF16)),
             u_ref[...], None)

    @pl.when(j == nj - 1)
    def _():
        for p in range(pairs):
            a0 = acc_ref[p * rows:p * rows + tn, :]
            a1 = acc_ref[p * rows + tn:(p + 1) * rows, :]
            o_ref[:, lanes(p)] = jnp.where(low, a0, a1).astype(o_ref.dtype)


def _sba_sample(q, kn, vn, kc, vc, u, un, batch, tn, tk, nsub):
    n = q.shape[0]
    past = kc.shape[2]
    tkb = tk * nsub
    nj = past // tkb
    rows = SBA_HEADS * tn
    cache = pl.BlockSpec((1, SBA_W, tkb), lambda b, j: (b, 0, nj - 1 - j))
    return pl.pallas_call(
        functools.partial(_sba_sample_kernel, tn=tn, tk=tk, nsub=nsub),
        grid=(batch, nj),
        in_specs=[
            pl.BlockSpec((tn, SBA_W), lambda b, j: (b, 0)),
            pl.BlockSpec((tn, SBA_W), lambda b, j: (b, 0)),
            pl.BlockSpec((tn, SBA_W), lambda b, j: (b, 0)),
            cache, cache,
            pl.BlockSpec((tk, tk), lambda b, j: (0, 0)),
            pl.BlockSpec((tn, tn), lambda b, j: (0, 0)),
        ],
        out_specs=pl.BlockSpec((tn, SBA_W), lambda b, j: (b, 0)),
        out_shape=jax.ShapeDtypeStruct((n, SBA_W), BF16),
        scratch_shapes=[pltpu.VMEM((SBA_W // LANES, 2 * tn, LANES), BF16),
                        pltpu.VMEM((rows, 1), F32), pltpu.VMEM((rows, LANES), F32)],
        compiler_params=_cparams(("parallel", "arbitrary")),
        name="sba_sample",
    )(q, kn, vn, kc, vc, u, un)


def _conv_kernel(h0_ref, glu_ref, wdw_ref, bdw_ref, g_ref, b_ref, wpw_ref, o_ref, buf_ref, *, tt, rc):
    ti = pl.program_id(1)

    @pl.when(ti == 0)
    def _():
        buf_ref[0:HALO, :] = h0_ref[0]

    @pl.when(ti > 0)
    def _():
        buf_ref[0:HALO, :] = buf_ref[tt:tt + HALO, :]

    buf_ref[HALO:HALO + tt, :] = glu_ref[...]
    base = HALO - (CONV_K - 1)
    wdw = wdw_ref[...]
    for r0 in range(0, tt, rc):
        acc = jnp.zeros((rc, CONV_W), F32)
        for kk in range(CONV_K):
            acc = acc + buf_ref[pl.ds(base + r0 + kk, rc), :] * wdw[kk:kk + 1, :]
        u = _ln(acc + bdw_ref[...], g_ref[...], b_ref[...])
        s = u * jax.nn.sigmoid(u)
        o_ref[pl.ds(r0, rc), :] = _dot(s.astype(BF16), wpw_ref[...]).astype(o_ref.dtype)


def _conv_module(h0, glu, w_dw, b_dw, g, b, w_pw_bf16, batch, seq, tt):
    n = glu.shape[0]
    nt = seq // tt
    rc = min(tt, 64)
    full = lambda s: pl.BlockSpec(s, lambda bi, ti: (0,) * len(s))
    return pl.pallas_call(
        functools.partial(_conv_kernel, tt=tt, rc=rc),
        grid=(batch, nt),
        in_specs=[
            pl.BlockSpec((1, HALO, CONV_W), lambda bi, ti: (bi, 0, 0)),
            pl.BlockSpec((tt, CONV_W), lambda bi, ti: (bi * nt + ti, 0)),
            full((CONV_K, CONV_W)), full((1, CONV_W)), full((1, CONV_W)), full((1, CONV_W)),
            full((CONV_W, CONV_W)),
        ],
        out_specs=pl.BlockSpec((tt, CONV_W), lambda bi, ti: (bi * nt + ti, 0)),
        out_shape=jax.ShapeDtypeStruct((n, CONV_W), BF16),
        scratch_shapes=[pltpu.VMEM((HALO + tt, CONV_W), F32)],
        compiler_params=_cparams(("parallel", "arbitrary")),
        name="conv_module",
    )(h0, glu, w_dw, b_dw, g, b, w_pw_bf16)


def _mem_kv_kernel(m_ref, wkt_ref, wvt_ref, k_ref, v_ref):
    m = m_ref[...].astype(BF16)
    k_ref[0] = _dot_nt(wkt_ref[...], m)
    v_ref[0] = _dot_nt(wvt_ref[...], m)


def _mem_kv(mem2d, wkt, wvt, batch):
    full = pl.BlockSpec((MEM_W, D_MODEL), lambda i: (0, 0))
    out = pl.BlockSpec((1, MEM_W, N_MEM), lambda i: (i, 0, 0))
    return pl.pallas_call(
        _mem_kv_kernel,
        grid=(batch,),
        in_specs=[pl.BlockSpec((N_MEM, D_MODEL), lambda i: (i, 0)), full, full],
        out_specs=[out, out],
        out_shape=[jax.ShapeDtypeStruct((batch, MEM_W, N_MEM), F32)] * 2,
        compiler_params=_cparams(("parallel",)),
        name="mem_kv",
    )(mem2d, wkt, wvt)


def _mem_attn_kernel(q_ref, k_ref, v_ref, o_ref):
    q = q_ref[...]
    tq = q.shape[0]
    low = _head_masks((tq, LANES))
    for p in range(MEM_W // LANES):
        qs = _stack_heads(q[:, p * LANES:(p + 1) * LANES], low)
        kt = k_ref[0, p * LANES:(p + 1) * LANES, :].astype(BF16)
        vt = v_ref[0, p * LANES:(p + 1) * LANES, :].astype(BF16)
        s = _dot(qs, kt)
        e = jnp.exp(s - jnp.max(s, axis=-1, keepdims=True))
        o = _dot_nt(e.astype(BF16), vt) / jnp.sum(e, axis=-1, keepdims=True)
        o_ref[:, p * LANES:(p + 1) * LANES] = jnp.where(low, o[0:tq], o[tq:2 * tq]).astype(o_ref.dtype)


def _mem_attn(mq, mkt, mvt, batch, seq, tq):
    n = mq.shape[0]
    nq = seq // tq
    kv = pl.BlockSpec((1, MEM_W, N_MEM), lambda b, i: (b, 0, 0))
    return pl.pallas_call(
        _mem_attn_kernel,
        grid=(batch, nq),
        in_specs=[pl.BlockSpec((tq, MEM_W), lambda b, i: (b * nq + i, 0)), kv, kv],
        out_specs=pl.BlockSpec((tq, MEM_W), lambda b, i: (b * nq + i, 0)),
        out_shape=jax.ShapeDtypeStruct((n, MEM_W), BF16),
        compiler_params=_cparams(("parallel", "parallel")),
        name="mem_attn",
    )(mq, mkt, mvt)


def _store_token_tiles(ref, x):
    rows = x.shape[0]
    for c in range(SUBLANES):
        ref[pl.ds(c, rows, stride=SUBLANES), :] = x[:, c * LANES:(c + 1) * LANES]


def _load_token_tiles(ref):
    rows = ref.shape[0] // SUBLANES
    return jnp.concatenate([ref[pl.ds(c, rows, stride=SUBLANES), :] for c in range(SUBLANES)], axis=1)


def _split2(x):
    a = x.astype(BF16)
    b = (x - a.astype(F32)).astype(BF16)
    return a, b


def _post_body(x_ref, sba_ref, conv_ref, mem_ref, g0_ref, b0_ref, wo_ref, g1_ref, b1_ref,
               wr_ref, br_ref, x1_ref, route_ref):
    xn = _ln(x_ref[...], g0_ref[...], b0_ref[...])
    mix = _dot(sba_ref[...], wo_ref[0:SBA_W, :])
    mix = mix + _dot(conv_ref[...], wo_ref[SBA_W:SBA_W + CONV_W, :])
    mix = mix + _dot(mem_ref[...], wo_ref[SBA_W + CONV_W:SBA_W + CONV_W + MEM_W, :])
    x1 = _ln(DEEPNORM_ALPHA * xn + mix, g1_ref[...], b1_ref[...])
    _store_token_tiles(x1_ref, x1)

    a0, a1 = _split2(x1)
    w0, w1 = wr_ref[0], wr_ref[1]
    logits = _dot(a0, w0) + (_dot(a0, w1) + _dot(a1, w0)) + br_ref[...]
    tm = logits.shape[0]
    lane = lax.broadcasted_iota(jnp.int32, (tm, LANES), 1).astype(F32)
    neg = jnp.float32(-jnp.inf)
    big = jnp.float32(LANES)
    is_g = jnp.logical_and(lane >= N_EXPERTS, lane < N_EXPERTS + N_GROUPS)
    gl = jnp.where(is_g, logits, neg)
    gmax = jnp.max(gl, axis=-1, keepdims=True)
    g_idx = jnp.min(jnp.where(gl == gmax, lane, big), axis=-1, keepdims=True) - N_EXPERTS
    g_w = 1.0 / jnp.sum(jnp.exp(gl - gmax), axis=-1, keepdims=True)
    in_grp = jnp.logical_and(lane >= g_idx * EXPERTS_PER_GROUP, lane < (g_idx + 1.0) * EXPERTS_PER_GROUP)
    el = jnp.where(in_grp, logits, neg)
    v1 = jnp.max(el, axis=-1, keepdims=True)
    i1 = jnp.min(jnp.where(el == v1, lane, big), axis=-1, keepdims=True)
    el2 = jnp.where(lane == i1, neg, el)
    v2 = jnp.max(el2, axis=-1, keepdims=True)
    i2 = jnp.min(jnp.where(el2 == v2, lane, big), axis=-1, keepdims=True)
    e2 = jnp.exp(v2 - v1)
    p1 = 1.0 / (1.0 + e2)
    p2 = e2 / (1.0 + e2)
    route_ref[...] = jnp.where(lane == 0.0, i1, jnp.where(lane == 1.0, i2, jnp.where(
        lane == 2.0, p1 * g_w, jnp.where(lane == 3.0, p2 * g_w, 0.0))))


def _post_kernel(xp_ref, xs_ref, sbap_ref, sbas_ref, convp_ref, convs_ref, memp_ref, mems_ref,
                 g0_ref, b0_ref, wo_ref, g1_ref, b1_ref, wr_ref, br_ref, x1_ref, route_ref, *, prompt_tiles):
    i = pl.program_id(0)
    shared = (g0_ref, b0_ref, wo_ref, g1_ref, b1_ref, wr_ref, br_ref, x1_ref, route_ref)

    @pl.when(i < prompt_tiles)
    def _():
        _post_body(xp_ref, sbap_ref, convp_ref, memp_ref, *shared)

    @pl.when(i >= prompt_tiles)
    def _():
        _post_body(xs_ref, sbas_ref, convs_ref, mems_ref, *shared)


def _post(xp, xs, sbap, sbas, convp, convs, memp, mems, g0, b0, wo_bf16, g1, b1, wr3, br, tm):
    pt = xp.shape[0] // tm
    st = xs.shape[0] // tm
    n = xp.shape[0] + xs.shape[0]
    prow = lambda w: pl.BlockSpec((tm, w), lambda i: (jnp.minimum(i, pt - 1), 0))
    srow = lambda w: pl.BlockSpec((tm, w), lambda i: (jnp.maximum(i - pt, 0), 0))
    full = lambda s: pl.BlockSpec(s, lambda i: (0,) * len(s))
    widths = (D_MODEL, SBA_W, CONV_W, MEM_W)
    return pl.pallas_call(
        functools.partial(_post_kernel, prompt_tiles=pt),
        grid=(pt + st,),
        in_specs=[spec(w) for w in widths for spec in (prow, srow)] + [
            full((1, D_MODEL)), full((1, D_MODEL)), full((D_MODEL, D_MODEL)),
            full((1, D_MODEL)), full((1, D_MODEL)),
            full((2, D_MODEL, LANES)), full((1, LANES))],
        out_specs=[pl.BlockSpec((tm * SUBLANES, LANES), lambda i: (i, 0)),
                   pl.BlockSpec((tm, LANES), lambda i: (i, 0))],
        out_shape=[jax.ShapeDtypeStruct((n * SUBLANES, LANES), F32),
                   jax.ShapeDtypeStruct((n, LANES), F32)],
        compiler_params=_cparams(("arbitrary",)),
        name="post",
    )(xp, xs, sbap, sbas, convp, convs, memp, mems, g0, b0, wo_bf16, g1, b1, wr3, br)


def _moe_plan(route, n):
    tiles = n // MOE_TM + N_GROUPS
    g = route[:, 0].astype(jnp.int32) // EXPERTS_PER_GROUP
    order = jnp.argsort(g, stable=True).astype(jnp.int32)
    counts = jnp.sum((g[:, None] == jnp.arange(N_GROUPS, dtype=jnp.int32)[None, :]).astype(jnp.int32), axis=0)
    ntile = (counts + MOE_TM - 1) // MOE_TM
    tile_end = jnp.cumsum(ntile)
    tile_start = tile_end - ntile
    first = jnp.cumsum(counts) - counts
    n_active = tile_end[-1]
    t = jnp.arange(tiles, dtype=jnp.int32)
    tg = jnp.minimum(jnp.sum((t[:, None] >= tile_end[None, :]).astype(jnp.int32), axis=1), N_GROUPS - 1)
    done = (t - tile_start[tg]) * MOE_TM
    n_valid = jnp.where(t < n_active, jnp.clip(counts[tg] - done, 0, MOE_TM), 0)
    r = jnp.arange(MOE_TM, dtype=jnp.int32)
    valid = r[None, :] < n_valid[:, None]
    tok = order[jnp.clip(first[tg][:, None] + done[:, None] + r[None, :], 0, n - 1)]
    src = jnp.where(valid, tok, 0)
    dst = jnp.where(valid, tok, n + r[None, :])
    picks = route[:, 0:4][tok]
    local = picks[..., 0:2].astype(jnp.int32) % EXPERTS_PER_GROUP
    hit = local[..., None] == jnp.arange(EXPERTS_PER_GROUP, dtype=jnp.int32)
    gate = jnp.sum(jnp.where(hit, picks[..., 2:4, None], 0.0), axis=2)
    gate = jnp.where(valid[..., None], gate, 0.0).reshape(tiles * MOE_TM, EXPERTS_PER_GROUP)
    as_rows = lambda a: (a * SUBLANES).astype(jnp.int32).reshape(tiles, 1, MOE_TM)
    return tg.astype(jnp.int32), n_active.reshape(1).astype(jnp.int32), as_rows(src), as_rows(dst), gate


def _moe_kernel(tg_ref, na_ref, src0_ref, src1_ref, dst_ref, gate_ref, x_hbm, wg_ref, wu_ref, wd_ref,
                y_hbm, xg_ref, xb_ref, acc_ref, yo_ref, gsem, ssem):
    t = pl.program_id(0)
    j = pl.program_id(1)
    n_active = na_ref[0]
    tile_rows = MOE_TM * SUBLANES

    def token(ref, row0):
        return ref.at[pl.ds(pl.multiple_of(row0, SUBLANES), SUBLANES)]

    def start_gather(src_ref, slot):
        def body(i, carry):
            for k in range(MOE_UNROLL):
                r = i * MOE_UNROLL + k
                pltpu.make_async_copy(token(x_hbm, src_ref[0, 0, r]), token(xg_ref.at[slot], r * SUBLANES),
                                      gsem.at[slot]).start(priority=k % 2)
            return carry
        lax.fori_loop(0, MOE_TM // MOE_UNROLL, body, 0)

    def wait_gather(slot):
        pltpu.make_async_copy(x_hbm.at[pl.ds(0, tile_rows)], xg_ref.at[slot], gsem.at[slot]).wait()

    def wait_scatter():
        pltpu.make_async_copy(yo_ref, y_hbm.at[pl.ds(0, tile_rows)], ssem.at[0]).wait()

    @pl.when(t < n_active)
    def _():
        slot = lax.rem(t, 2)

        @pl.when(j == 0)
        def _():
            @pl.when(t == 0)
            def _():
                start_gather(src0_ref, 0)
                yo_ref[...] = jnp.zeros_like(yo_ref)
                sink = pltpu.make_async_copy(yo_ref, y_hbm.at[pl.ds(y_hbm.shape[0] - tile_rows, tile_rows)],
                                             ssem.at[0])
                sink.start()
                sink.wait()

            @pl.when(t + 1 < n_active)
            def _():
                start_gather(src1_ref, 1 - slot)

            wait_gather(slot)
            xb_ref[...] = _load_token_tiles(xg_ref.at[slot]).astype(BF16)
            acc_ref[...] = jnp.zeros_like(acc_ref)

        x = xb_ref[...]
        a = _dot(x, wg_ref[0].astype(BF16))
        u = _dot(x, wu_ref[0].astype(BF16))
        gate = gate_ref[...]
        lane = lax.broadcasted_iota(jnp.int32, gate.shape, 1)
        ge = jnp.sum(jnp.where(lane == j, gate, 0.0), axis=1, keepdims=True)
        hid = (a * jax.nn.sigmoid(a)) * u * ge
        acc_ref[...] += _dot(hid.astype(BF16), wd_ref[0].astype(BF16))

        @pl.when(j == EXPERTS_PER_GROUP - 1)
        def _():
            @pl.when(t > 0)
            def _():
                wait_scatter()

            _store_token_tiles(yo_ref, acc_ref[...])

            def body(i, carry):
                for k in range(MOE_UNROLL):
                    r = i * MOE_UNROLL + k
                    pltpu.make_async_copy(token(yo_ref, r * SUBLANES), token(y_hbm, dst_ref[0, 0, r]),
                                          ssem.at[0]).start(priority=k % 2)
                return carry
            lax.fori_loop(0, MOE_TM // MOE_UNROLL, body, 0)

            @pl.when(t == n_active - 1)
            def _():
                wait_scatter()


def _moe(x1, plan, w_eg, w_eu, w_ed):
    tg, n_active, src, dst, gate = plan
    tiles = src.shape[0]
    tile_rows = MOE_TM * SUBLANES
    idx = lambda f: pl.BlockSpec((1, 1, MOE_TM), f, memory_space=pltpu.SMEM)
    expert = lambda a, b: pl.BlockSpec((1, a, b), lambda t, j, tg, na: (tg[t] * EXPERTS_PER_GROUP + j, 0, 0))
    grid_spec = pltpu.PrefetchScalarGridSpec(
        num_scalar_prefetch=2,
        grid=(tiles, EXPERTS_PER_GROUP),
        in_specs=[
            idx(lambda t, j, tg, na: (t, 0, 0)),
            idx(lambda t, j, tg, na: (jnp.minimum(t + 1, tiles - 1), 0, 0)),
            idx(lambda t, j, tg, na: (t, 0, 0)),
            pl.BlockSpec((MOE_TM, EXPERTS_PER_GROUP), lambda t, j, tg, na: (t, 0)),
            pl.BlockSpec(memory_space=pl.ANY),
            expert(D_MODEL, D_EXPERT), expert(D_MODEL, D_EXPERT), expert(D_EXPERT, D_MODEL),
        ],
        out_specs=pl.BlockSpec(memory_space=pl.ANY),
        scratch_shapes=[pltpu.VMEM((2, tile_rows, LANES), F32), pltpu.VMEM((MOE_TM, D_MODEL), BF16),
                        pltpu.VMEM((MOE_TM, D_MODEL), F32), pltpu.VMEM((tile_rows, LANES), F32),
                        pltpu.SemaphoreType.DMA((2,)), pltpu.SemaphoreType.DMA((1,))],
    )
    return pl.pallas_call(
        _moe_kernel,
        grid_spec=grid_spec,
        out_shape=jax.ShapeDtypeStruct((x1.shape[0] + tile_rows, LANES), F32),
        compiler_params=_cparams(("arbitrary", "arbitrary")),
        name="moe",
    )(tg, n_active, src, src, dst, gate, x1, w_eg, w_eu, w_ed)


def _combine_kernel(x1_ref, y_ref, g_ref, b_ref, o_ref):
    o_ref[...] = _ln(DEEPNORM_ALPHA * _load_token_tiles(x1_ref) + _load_token_tiles(y_ref), g_ref[...], b_ref[...])


def _combine(x1, y, g2, b2, row0, rows, tm):
    b0 = row0 // tm
    tiles = pl.BlockSpec((tm * SUBLANES, LANES), lambda i: (i + b0, 0))
    full = pl.BlockSpec((1, D_MODEL), lambda i: (0, 0))
    return pl.pallas_call(
        _combine_kernel,
        grid=(rows // tm,),
        in_specs=[tiles, tiles, full, full],
        out_specs=pl.BlockSpec((tm, D_MODEL), lambda i: (i, 0)),
        out_shape=jax.ShapeDtypeStruct((rows, D_MODEL), F32),
        compiler_params=_cparams(("parallel",)),
        name="combine",
    )(x1, y, g2, b2)


def _later_or_same(n):
    r = lax.broadcasted_iota(jnp.int32, (n, n), 0)
    c = lax.broadcasted_iota(jnp.int32, (n, n), 1)
    return (r >= c).astype(BF16)


def kernel(x_prompt, x_sample, mem_prompt, cache_sba_k, cache_sba_v, cache_conv, cache_mem_k, cache_mem_v,
           ln0_g, ln0_b, w_in, w_dw, b_dw, lnc_g, lnc_b, w_cpw, w_mk, w_mv, w_out, ln1_g, ln1_b,
           w_rg, b_rg, w_re, b_re, w_eg, w_eu, w_ed, ln2_g, ln2_b):
    bp, tp, _ = x_prompt.shape
    bs, ts, _ = x_sample.shape
    past = cache_sba_k.shape[2]
    l = 0
    r2 = lambda a: a.reshape(1, -1)

    w_in_b = w_in[l].astype(BF16)
    w_out_b = w_out[l].astype(BF16)
    w_cpw_b = w_cpw[l].astype(BF16)
    wr = jnp.zeros((D_MODEL, LANES), F32)
    wr = wr.at[:, 0:N_EXPERTS].set(w_re[l]).at[:, N_EXPERTS:N_EXPERTS + N_GROUPS].set(w_rg[l])
    wr0 = wr.astype(BF16)
    wr1 = (wr - wr0.astype(F32)).astype(BF16)
    wr3 = jnp.stack([wr0, wr1])
    br = jnp.zeros((1, LANES), F32)
    br = br.at[0, 0:N_EXPERTS].set(b_re[l]).at[0, N_EXPERTS:N_EXPERTS + N_GROUPS].set(b_rg[l])
    g0, b0 = r2(ln0_g), r2(ln0_b)

    tq = 256
    tk_s = 256
    u_p = _later_or_same(tq)
    u_s = _later_or_same(tk_s)
    u_n = _later_or_same(ts)

    xp2d = x_prompt.reshape(bp * tp, D_MODEL)
    xs2d = x_sample.reshape(bs * ts, D_MODEL)
    wkt = w_in[l][:, SBA_W:2 * SBA_W].T.astype(BF16)
    wvt = w_in[l][:, 2 * SBA_W:3 * SBA_W].T.astype(BF16)
    qp, ktbp, vtbp, ktp, vtp, glup, mqp = _in_proj_t(xp2d, g0, b0, w_in_b, wkt, wvt, bp, tp, 512, tq)
    qs, kbs, vbs, ks, vs, glus, mqs = _in_proj(xs2d, g0, b0, w_in_b, 512)

    sba_p = _sba_prompt(qp, ktbp, vtbp, u_p, bp, tp, tq, 4)
    kc = jnp.transpose(cache_sba_k[l], (0, 2, 3, 1)).reshape(bs, SBA_W, past)
    vc = jnp.transpose(cache_sba_v[l], (0, 2, 3, 1)).reshape(bs, SBA_W, past)
    sba_s = _sba_sample(qs, kbs, vbs, kc, vc, u_s, u_n, bs, ts, tk_s, 4)

    conv_w = (w_dw[l], r2(b_dw[l]), r2(lnc_g[l]), r2(lnc_b[l]), w_cpw_b)
    h0_p = jnp.zeros((bp, HALO, CONV_W), F32)
    h0_s = jnp.pad(cache_conv[l], ((0, 0), (HALO - (CONV_K - 1), 0), (0, 0)))
    conv_p = _conv_module(h0_p, glup, *conv_w, bp, tp, 256)
    conv_s = _conv_module(h0_s, glus, *conv_w, bs, ts, ts)

    mkt, mvt = _mem_kv(mem_prompt.reshape(bp * N_MEM, D_MODEL), w_mk[l].T.astype(BF16),
                       w_mv[l].T.astype(BF16), bp)
    mem_p = _mem_attn(mqp, mkt, mvt, bp, tp, 512)
    mem_s = _mem_attn(mqs, jnp.transpose(cache_mem_k[l], (0, 2, 3, 1)).reshape(bs, MEM_W, N_MEM),
                      jnp.transpose(cache_mem_v[l], (0, 2, 3, 1)).reshape(bs, MEM_W, N_MEM), bs, ts, ts)

    post_w = (g0, b0, w_out_b, r2(ln1_g[l]), r2(ln1_b[l]), wr3, br)
    x1, route = _post(xp2d, xs2d, sba_p, sba_s, conv_p, conv_s, mem_p, mem_s, *post_w, 512)
    f = _moe(x1, _moe_plan(route, route.shape[0]), w_eg[l], w_eu[l], w_ed[l])
    g2, b2 = r2(ln2_g[l]), r2(ln2_b[l])
    yp = _combine(x1, f, g2, b2, 0, bp * tp, 512)
    ys = _combine(x1, f, g2, b2, bp * tp, bs * ts, 512)

    hd = (SBA_HEADS, HEAD_DIM)
    glup3 = glup.reshape(bp, tp, CONV_W)
    glus3 = glus.reshape(bs, ts, CONV_W)
    conv_tail_s = jnp.concatenate([cache_conv[l], glus3], axis=1)[:, -(CONV_K - 1):]
    return (
        yp.reshape(bp, tp, D_MODEL),
        ys.reshape(bs, ts, D_MODEL),
        jnp.transpose(ktp.reshape(bp, *hd, tp), (0, 3, 1, 2))[None],
        jnp.transpose(vtp.reshape(bp, *hd, tp), (0, 3, 1, 2))[None],
        glup3[:, -(CONV_K - 1):][None],
        jnp.transpose(mkt.reshape(bp, MEM_HEADS, HEAD_DIM, N_MEM), (0, 3, 1, 2))[None],
        jnp.transpose(mvt.reshape(bp, MEM_HEADS, HEAD_DIM, N_MEM), (0, 3, 1, 2))[None],
        ks.reshape(1, bs, ts, *hd),
        vs.reshape(1, bs, ts, *hd),
        conv_tail_s[None],
    )
```

```python
import functools

import jax
import jax.numpy as jnp
from jax import lax
from jax.experimental import pallas as pl
from jax.experimental.pallas import tpu as pltpu

F32 = jnp.float32
BF16 = jnp.bfloat16

D_MODEL = 1024
HEAD_DIM = 64
SBA_HEADS = 8
SBA_W = SBA_HEADS * HEAD_DIM
CONV_W = 256
CONV_K = 31
MEM_HEADS = 4
MEM_W = MEM_HEADS * HEAD_DIM
N_MEM = 256
N_GROUPS = 4
EXPERTS_PER_GROUP = 8
N_EXPERTS = N_GROUPS * EXPERTS_PER_GROUP
D_EXPERT = 256
DEPTH = 1
DEEPNORM_ALPHA = (2 * DEPTH) ** 0.25
QK_SCALE = HEAD_DIM ** -0.5
LN_EPS = 1e-5
LOG2E = 1.4426950408889634

LANES = 128
SUBLANES = 8
HALO = 32
MOE_TM = 1024
MOE_UNROLL = 8
SP_LINEAR = 100.0
VMEM_LIMIT = 48 * 1024 * 1024


def _cparams(sem):
    return pltpu.CompilerParams(dimension_semantics=sem, vmem_limit_bytes=VMEM_LIMIT)


def _ln(x, g, b):
    mu = jnp.mean(x, axis=-1, keepdims=True)
    xc = x - mu
    var = jnp.mean(xc * xc, axis=-1, keepdims=True)
    return xc * lax.rsqrt(var + LN_EPS) * g + b


def _dot(a, b):
    return jnp.dot(a, b, preferred_element_type=F32)


def _dot_nt(a, b):
    return lax.dot_general(a, b, (((1,), (1,)), ((), ())), preferred_element_type=F32)


def _in_proj_kernel(x_ref, g_ref, b_ref, w_ref, q_ref, kb_ref, vb_ref, k_ref, v_ref, glu_ref, mq_ref):
    xn = _ln(x_ref[...], g_ref[...], b_ref[...]).astype(BF16)
    q = _dot(xn, w_ref[:, 0:SBA_W])
    q_ref[...] = (q * (QK_SCALE * LOG2E)).astype(BF16)
    k = _dot(xn, w_ref[:, SBA_W:2 * SBA_W])
    k_ref[...] = k
    kb_ref[...] = k.astype(BF16)
    v = _dot(xn, w_ref[:, 2 * SBA_W:3 * SBA_W])
    v_ref[...] = v
    vb_ref[...] = v.astype(BF16)
    c0 = 3 * SBA_W
    cv = _dot(xn, w_ref[:, c0:c0 + CONV_W])
    cg = _dot(xn, w_ref[:, c0 + CONV_W:c0 + 2 * CONV_W])
    glu_ref[...] = cv * jax.nn.sigmoid(cg)
    mq = _dot(xn, w_ref[:, c0 + 2 * CONV_W:c0 + 2 * CONV_W + MEM_W])
    mq_ref[...] = (mq * QK_SCALE).astype(BF16)


def _in_proj_t_kernel(x_ref, g_ref, b_ref, w_ref, wkt_ref, wvt_ref,
                      q_ref, ktb_ref, vtb_ref, kt_ref, vt_ref, glu_ref, mq_ref, *, tq):
    xn = _ln(x_ref[...], g_ref[...], b_ref[...]).astype(BF16)
    tm = xn.shape[0]
    q = _dot(xn, w_ref[:, 0:SBA_W])
    q_ref[...] = (q * (QK_SCALE * LOG2E)).astype(BF16)
    for wt_ref, t_ref, tb_ref in ((wkt_ref, kt_ref, ktb_ref), (wvt_ref, vt_ref, vtb_ref)):
        t = _dot_nt(wt_ref[...], xn)
        t_ref[0] = t
        for i in range(tm // tq):
            tb_ref[0, i] = t[:, i * tq:(i + 1) * tq].astype(BF16)
    c0 = 3 * SBA_W
    cv = _dot(xn, w_ref[:, c0:c0 + CONV_W])
    cg = _dot(xn, w_ref[:, c0 + CONV_W:c0 + 2 * CONV_W])
    glu_ref[...] = cv * jax.nn.sigmoid(cg)
    mq = _dot(xn, w_ref[:, c0 + 2 * CONV_W:c0 + 2 * CONV_W + MEM_W])
    mq_ref[...] = (mq * QK_SCALE).astype(BF16)


def _in_proj_t(x2d, g, b, w_bf16, wkt, wvt, batch, seq, tm, tq):
    n = x2d.shape[0]
    in_w = w_bf16.shape[1]
    nt = seq // tm
    row = lambda w: pl.BlockSpec((tm, w), lambda bi, i: (bi * nt + i, 0))
    full = lambda s: pl.BlockSpec(s, lambda bi, i: (0, 0))
    tr = pl.BlockSpec((1, SBA_W, tm), lambda bi, i: (bi, 0, i))
    trb = pl.BlockSpec((1, tm // tq, SBA_W, tq), lambda bi, i: (bi, i, 0, 0))
    return pl.pallas_call(
        functools.partial(_in_proj_t_kernel, tq=tq),
        grid=(batch, nt),
        in_specs=[row(D_MODEL), full((1, D_MODEL)), full((1, D_MODEL)), full((D_MODEL, in_w)),
                  full((SBA_W, D_MODEL)), full((SBA_W, D_MODEL))],
        out_specs=[row(SBA_W), trb, trb, tr, tr, row(CONV_W), row(MEM_W)],
        out_shape=[
            jax.ShapeDtypeStruct((n, SBA_W), BF16),
            jax.ShapeDtypeStruct((batch, seq // tq, SBA_W, tq), BF16),
            jax.ShapeDtypeStruct((batch, seq // tq, SBA_W, tq), BF16),
            jax.ShapeDtypeStruct((batch, SBA_W, seq), F32),
            jax.ShapeDtypeStruct((batch, SBA_W, seq), F32),
            jax.ShapeDtypeStruct((n, CONV_W), F32),
            jax.ShapeDtypeStruct((n, MEM_W), BF16),
        ],
        compiler_params=_cparams(("parallel", "parallel")),
        name="in_proj_t",
    )(x2d, g, b, w_bf16, wkt, wvt)


def _in_proj(x2d, g, b, w_bf16, tm):
    n = x2d.shape[0]
    in_w = w_bf16.shape[1]
    row = lambda w: pl.BlockSpec((tm, w), lambda i: (i, 0))
    full = lambda s: pl.BlockSpec(s, lambda i: (0, 0))
    return pl.pallas_call(
        _in_proj_kernel,
        grid=(n // tm,),
        in_specs=[row(D_MODEL), full((1, D_MODEL)), full((1, D_MODEL)), full((D_MODEL, in_w))],
        out_specs=[row(SBA_W), row(SBA_W), row(SBA_W), row(SBA_W), row(SBA_W), row(CONV_W), row(MEM_W)],
        out_shape=[
            jax.ShapeDtypeStruct((n, SBA_W), BF16),
            jax.ShapeDtypeStruct((n, SBA_W), BF16),
            jax.ShapeDtypeStruct((n, SBA_W), BF16),
            jax.ShapeDtypeStruct((n, SBA_W), F32),
            jax.ShapeDtypeStruct((n, SBA_W), F32),
            jax.ShapeDtypeStruct((n, CONV_W), F32),
            jax.ShapeDtypeStruct((n, MEM_W), BF16),
        ],
        compiler_params=_cparams(("parallel",)),
        name="in_proj",
    )(x2d, g, b, w_bf16)


def _head_masks(shape):
    lane = lax.broadcasted_iota(jnp.int32, shape, 1)
    return lane < HEAD_DIM


def _stack_heads(q, low):
    zero = jnp.zeros_like(q)
    return jnp.concatenate([jnp.where(low, q, zero), jnp.where(low, zero, q)], axis=0)


def _sba_tile(z, u, mask):
    sp = jnp.where(z > SP_LINEAR, z, jnp.log(1.0 + jnp.exp2(z)) * LOG2E)
    if mask is not None:
        sp = jnp.where(mask, sp, 0.0)
    w = jnp.exp2(z - _dot(sp.astype(BF16), u))
    if mask is not None:
        w = jnp.where(mask, w, 0.0)
    return w.astype(BF16), jnp.sum(sp, axis=1, keepdims=True)


def _sba_prompt_kernel(q_ref, k_ref, v_ref, u_ref, o_ref, *, tq, hp):
    qi = pl.program_id(2)
    low = _head_masks((tq, LANES))
    u = u_ref[...]
    row = lax.broadcasted_iota(jnp.int32, (2 * tq, tq), 0) & (tq - 1)
    col = lax.broadcasted_iota(jnp.int32, (2 * tq, tq), 1)
    qs = [_stack_heads(q_ref[:, p * LANES:(p + 1) * LANES], low) for p in range(hp)]

    def blk(j, carry, mask):
        out = []
        for p in range(hp):
            c, acc = carry[p]
            z = _dot(qs[p], k_ref[0, j, p * LANES:(p + 1) * LANES, :])
            w, rs = _sba_tile(z, u, mask)
            acc = acc + _dot_nt(w, v_ref[0, j, p * LANES:(p + 1) * LANES, :]) * jnp.exp2(-c)
            out.append((c + rs, acc))
        return tuple(out)

    carry = tuple((jnp.zeros((2 * tq, 1), F32), jnp.zeros((2 * tq, LANES), F32)) for _ in range(hp))
    carry = blk(qi, carry, col < row)
    carry = lax.fori_loop(0, qi, lambda t, cr: blk(qi - 1 - t, cr, None), carry)
    for p in range(hp):
        acc = carry[p][1]
        o_ref[:, p * LANES:(p + 1) * LANES] = jnp.where(low, acc[0:tq], acc[tq:2 * tq]).astype(o_ref.dtype)


def _sba_prompt(q, kt, vt, u, batch, seq, tq, hp):
    n = q.shape[0]
    nq = seq // tq
    groups = SBA_W // (LANES * hp)
    kv = pl.BlockSpec((1, nq, hp * LANES, tq), lambda b, p, i: (b, 0, p, 0))
    return pl.pallas_call(
        functools.partial(_sba_prompt_kernel, tq=tq, hp=hp),
        grid=(batch, groups, nq),
        in_specs=[
            pl.BlockSpec((tq, hp * LANES), lambda b, p, i: (b * nq + i, p)),
            kv, kv,
            pl.BlockSpec((tq, tq), lambda b, p, i: (0, 0)),
        ],
        out_specs=pl.BlockSpec((tq, hp * LANES), lambda b, p, i: (b * nq + i, p)),
        out_shape=jax.ShapeDtypeStruct((n, SBA_W), BF16),
        compiler_params=_cparams(("parallel", "parallel", "arbitrary")),
        name="sba_prompt",
    )(q, kt, vt, u)


def _sba_sample_kernel(q_ref, kn_ref, vn_ref, kc_ref, vc_ref, u_ref, un_ref, o_ref, qs_ref, c_ref, acc_ref,
                       *, tn, tk, nsub):
    j = pl.program_id(1)
    nj = pl.num_programs(1)
    low = _head_masks((tn, LANES))
    pairs = SBA_W // LANES
    rows = 2 * tn

    def step(logits, weighted, u, mask):
        z = jnp.concatenate([logits(qs_ref[p], p) for p in range(pairs)], axis=0)
        w, rs = _sba_tile(z, u, mask)
        pv = jnp.concatenate([weighted(w[p * rows:(p + 1) * rows], p) for p in range(pairs)], axis=0)
        c = c_ref[...]
        acc_ref[...] += pv * jnp.exp2(-c)
        c_ref[...] = c + rs

    def lanes(p):
        return slice(p * LANES, (p + 1) * LANES)

    @pl.when(j == 0)
    def _():
        for p in range(pairs):
            qs_ref[p] = _stack_heads(q_ref[:, lanes(p)], low)
        c_ref[...] = jnp.zeros_like(c_ref)
        acc_ref[...] = jnp.zeros_like(acc_ref)
        row = lax.broadcasted_iota(jnp.int32, (pairs * rows, tn), 0) & (tn - 1)
        col = lax.broadcasted_iota(jnp.int32, (pairs * rows, tn), 1)
        step(lambda q, p: _dot_nt(q, kn_ref[:, lanes(p)]), lambda w, p: _dot(w, vn_ref[:, lanes(p)]),
             un_ref[...], col < row)

    for s in reversed(range(nsub)):
        keys = slice(s * tk, (s + 1) * tk)
        step(lambda q, p: _dot(q, kc_ref[0, lanes(p), keys].astype(BF16)),
             lambda w, p: _dot_nt(w, vc_ref[0, lanes(p), keys].astype(BF16)),
             u_ref[...], None)

    @pl.when(j == nj - 1)
    def _():
        for p in range(pairs):
            a0 = acc_ref[p * rows:p * rows + tn, :]
            a1 = acc_ref[p * rows + tn:(p + 1) * rows, :]
            o_ref[:, lanes(p)] = jnp.where(low, a0, a1).astype(o_ref.dtype)


def _sba_sample(q, kn, vn, kc, vc, u, un, batch, tn, tk, nsub):
    n = q.shape[0]
    past = kc.shape[2]
    tkb = tk * nsub
    nj = past // tkb
    rows = SBA_HEADS * tn
    cache = pl.BlockSpec((1, SBA_W, tkb), lambda b, j: (b, 0, nj - 1 - j))
    return pl.pallas_call(
        functools.partial(_sba_sample_kernel, tn=tn, tk=tk, nsub=nsub),
        grid=(batch, nj),
        in_specs=[
            pl.BlockSpec((tn, SBA_W), lambda b, j: (b, 0)),
            pl.BlockSpec((tn, SBA_W), lambda b, j: (b, 0)),
            pl.BlockSpec((tn, SBA_W), lambda b, j: (b, 0)),
            cache, cache,
            pl.BlockSpec((tk, tk), lambda b, j: (0, 0)),
            pl.BlockSpec((tn, tn), lambda b, j: (0, 0)),
        ],
        out_specs=pl.BlockSpec((tn, SBA_W), lambda b, j: (b, 0)),
        out_shape=jax.ShapeDtypeStruct((n, SBA_W), BF16),
        scratch_shapes=[pltpu.VMEM((SBA_W // LANES, 2 * tn, LANES), BF16),
                        pltpu.VMEM((rows, 1), F32), pltpu.VMEM((rows, LANES), F32)],
        compiler_params=_cparams(("parallel", "arbitrary")),
        name="sba_sample",
    )(q, kn, vn, kc, vc, u, un)


def _conv_kernel(h0_ref, glu_ref, wdw_ref, bdw_ref, g_ref, b_ref, wpw_ref, o_ref, buf_ref, *, tt, rc):
    ti = pl.program_id(1)
    rows = HALO + tt

    @pl.when(ti == 0)
    def _():
        buf_ref[0, 0:HALO, :] = h0_ref[0]

    @pl.when(ti > 0)
    def _():
        buf_ref[0, 0:HALO, :] = buf_ref[0, tt:tt + HALO, :]

    buf_ref[0, HALO:rows, :] = glu_ref[...]
    buf_ref[0, rows:rows + SUBLANES, :] = jnp.zeros((SUBLANES, CONV_W), F32)
    for s in range(1, SUBLANES):
        buf_ref[s, 0:rows, :] = buf_ref[0, pl.ds(s, rows), :]
    base = HALO - (CONV_K - 1)
    wdw = wdw_ref[...]
    for r0 in range(0, tt, rc):
        acc = jnp.zeros((rc, CONV_W), F32)
        for kk in range(CONV_K):
            off = base + r0 + kk
            acc = acc + buf_ref[off % SUBLANES, pl.ds(off - off % SUBLANES, rc), :] * wdw[kk:kk + 1, :]
        u = _ln(acc + bdw_ref[...], g_ref[...], b_ref[...])
        s = u * jax.nn.sigmoid(u)
        o_ref[pl.ds(r0, rc), :] = _dot(s.astype(BF16), wpw_ref[...]).astype(o_ref.dtype)


def _conv_module(h0, glu, w_dw, b_dw, g, b, w_pw_bf16, batch, seq, tt):
    n = glu.shape[0]
    nt = seq // tt
    rc = min(tt, 64)
    full = lambda s: pl.BlockSpec(s, lambda bi, ti: (0,) * len(s))
    return pl.pallas_call(
        functools.partial(_conv_kernel, tt=tt, rc=rc),
        grid=(batch, nt),
        in_specs=[
            pl.BlockSpec((1, HALO, CONV_W), lambda bi, ti: (bi, 0, 0)),
            pl.BlockSpec((tt, CONV_W), lambda bi, ti: (bi * nt + ti, 0)),
            full((CONV_K, CONV_W)), full((1, CONV_W)), full((1, CONV_W)), full((1, CONV_W)),
            full((CONV_W, CONV_W)),
        ],
        out_specs=pl.BlockSpec((tt, CONV_W), lambda bi, ti: (bi * nt + ti, 0)),
        out_shape=jax.ShapeDtypeStruct((n, CONV_W), BF16),
        scratch_shapes=[pltpu.VMEM((SUBLANES, HALO + tt + SUBLANES, CONV_W), F32)],
        compiler_params=_cparams(("parallel", "arbitrary")),
        name="conv_module",
    )(h0, glu, w_dw, b_dw, g, b, w_pw_bf16)


def _mem_kv_kernel(m_ref, wkt_ref, wvt_ref, k_ref, v_ref):
    m = m_ref[...].astype(BF16)
    k_ref[0] = _dot_nt(wkt_ref[...], m)
    v_ref[0] = _dot_nt(wvt_ref[...], m)


def _mem_kv(mem2d, wkt, wvt, batch):
    full = pl.BlockSpec((MEM_W, D_MODEL), lambda i: (0, 0))
    out = pl.BlockSpec((1, MEM_W, N_MEM), lambda i: (i, 0, 0))
    return pl.pallas_call(
        _mem_kv_kernel,
        grid=(batch,),
        in_specs=[pl.BlockSpec((N_MEM, D_MODEL), lambda i: (i, 0)), full, full],
        out_specs=[out, out],
        out_shape=[jax.ShapeDtypeStruct((batch, MEM_W, N_MEM), F32)] * 2,
        compiler_params=_cparams(("parallel",)),
        name="mem_kv",
    )(mem2d, wkt, wvt)


def _mem_attn_kernel(q_ref, k_ref, v_ref, o_ref):
    q = q_ref[...]
    tq = q.shape[0]
    low = _head_masks((tq, LANES))
    for p in range(MEM_W // LANES):
        qs = _stack_heads(q[:, p * LANES:(p + 1) * LANES], low)
        kt = k_ref[0, p * LANES:(p + 1) * LANES, :].astype(BF16)
        vt = v_ref[0, p * LANES:(p + 1) * LANES, :].astype(BF16)
        s = _dot(qs, kt)
        e = jnp.exp(s - jnp.max(s, axis=-1, keepdims=True))
        o = _dot_nt(e.astype(BF16), vt) / jnp.sum(e, axis=-1, keepdims=True)
        o_ref[:, p * LANES:(p + 1) * LANES] = jnp.where(low, o[0:tq], o[tq:2 * tq]).astype(o_ref.dtype)


def _mem_attn(mq, mkt, mvt, batch, seq, tq):
    n = mq.shape[0]
    nq = seq // tq
    kv = pl.BlockSpec((1, MEM_W, N_MEM), lambda b, i: (b, 0, 0))
    return pl.pallas_call(
        _mem_attn_kernel,
        grid=(batch, nq),
        in_specs=[pl.BlockSpec((tq, MEM_W), lambda b, i: (b * nq + i, 0)), kv, kv],
        out_specs=pl.BlockSpec((tq, MEM_W), lambda b, i: (b * nq + i, 0)),
        out_shape=jax.ShapeDtypeStruct((n, MEM_W), BF16),
        compiler_params=_cparams(("parallel", "parallel")),
        name="mem_attn",
    )(mq, mkt, mvt)


def _store_token_tiles(ref, x):
    rows = x.shape[0]
    for c in range(SUBLANES):
        ref[pl.ds(c, rows, stride=SUBLANES), :] = x[:, c * LANES:(c + 1) * LANES]


def _load_token_tiles(ref):
    rows = ref.shape[0] // SUBLANES
    return jnp.concatenate([ref[pl.ds(c, rows, stride=SUBLANES), :] for c in range(SUBLANES)], axis=1)


def _split2(x):
    a = x.astype(BF16)
    b = (x - a.astype(F32)).astype(BF16)
    return a, b


def _post_body(x_ref, sba_ref, conv_ref, mem_ref, g0_ref, b0_ref, wo_ref, g1_ref, b1_ref,
               wr_ref, br_ref, x1_ref, route_ref):
    xn = _ln(x_ref[...], g0_ref[...], b0_ref[...])
    mix = _dot(sba_ref[...], wo_ref[0:SBA_W, :])
    mix = mix + _dot(conv_ref[...], wo_ref[SBA_W:SBA_W + CONV_W, :])
    mix = mix + _dot(mem_ref[...], wo_ref[SBA_W + CONV_W:SBA_W + CONV_W + MEM_W, :])
    x1 = _ln(DEEPNORM_ALPHA * xn + mix, g1_ref[...], b1_ref[...])
    _store_token_tiles(x1_ref, x1)

    a0, a1 = _split2(x1)
    w0, w1 = wr_ref[0], wr_ref[1]
    logits = _dot(a0, w0) + (_dot(a0, w1) + _dot(a1, w0)) + br_ref[...]
    tm = logits.shape[0]
    lane = lax.broadcasted_iota(jnp.int32, (tm, LANES), 1).astype(F32)
    neg = jnp.float32(-jnp.inf)
    big = jnp.float32(LANES)
    is_g = jnp.logical_and(lane >= N_EXPERTS, lane < N_EXPERTS + N_GROUPS)
    gl = jnp.where(is_g, logits, neg)
    gmax = jnp.max(gl, axis=-1, keepdims=True)
    g_idx = jnp.min(jnp.where(gl == gmax, lane, big), axis=-1, keepdims=True) - N_EXPERTS
    g_w = 1.0 / jnp.sum(jnp.exp(gl - gmax), axis=-1, keepdims=True)
    in_grp = jnp.logical_and(lane >= g_idx * EXPERTS_PER_GROUP, lane < (g_idx + 1.0) * EXPERTS_PER_GROUP)
    el = jnp.where(in_grp, logits, neg)
    v1 = jnp.max(el, axis=-1, keepdims=True)
    i1 = jnp.min(jnp.where(el == v1, lane, big), axis=-1, keepdims=True)
    el2 = jnp.where(lane == i1, neg, el)
    v2 = jnp.max(el2, axis=-1, keepdims=True)
    i2 = jnp.min(jnp.where(el2 == v2, lane, big), axis=-1, keepdims=True)
    e2 = jnp.exp(v2 - v1)
    p1 = 1.0 / (1.0 + e2)
    p2 = e2 / (1.0 + e2)
    route_ref[...] = jnp.where(lane == 0.0, i1, jnp.where(lane == 1.0, i2, jnp.where(
        lane == 2.0, p1 * g_w, jnp.where(lane == 3.0, p2 * g_w, 0.0))))


def _post_kernel(xp_ref, xs_ref, sbap_ref, sbas_ref, convp_ref, convs_ref, memp_ref, mems_ref,
                 g0_ref, b0_ref, wo_ref, g1_ref, b1_ref, wr_ref, br_ref, x1_ref, route_ref, *, prompt_tiles):
    i = pl.program_id(0)
    shared = (g0_ref, b0_ref, wo_ref, g1_ref, b1_ref, wr_ref, br_ref, x1_ref, route_ref)

    @pl.when(i < prompt_tiles)
    def _():
        _post_body(xp_ref, sbap_ref, convp_ref, memp_ref, *shared)

    @pl.when(i >= prompt_tiles)
    def _():
        _post_body(xs_ref, sbas_ref, convs_ref, mems_ref, *shared)


def _post(xp, xs, sbap, sbas, convp, convs, memp, mems, g0, b0, wo_bf16, g1, b1, wr3, br, tm):
    pt = xp.shape[0] // tm
    st = xs.shape[0] // tm
    n = xp.shape[0] + xs.shape[0]
    prow = lambda w: pl.BlockSpec((tm, w), lambda i: (jnp.minimum(i, pt - 1), 0))
    srow = lambda w: pl.BlockSpec((tm, w), lambda i: (jnp.maximum(i - pt, 0), 0))
    full = lambda s: pl.BlockSpec(s, lambda i: (0,) * len(s))
    widths = (D_MODEL, SBA_W, CONV_W, MEM_W)
    return pl.pallas_call(
        functools.partial(_post_kernel, prompt_tiles=pt),
        grid=(pt + st,),
        in_specs=[spec(w) for w in widths for spec in (prow, srow)] + [
            full((1, D_MODEL)), full((1, D_MODEL)), full((D_MODEL, D_MODEL)),
            full((1, D_MODEL)), full((1, D_MODEL)),
            full((2, D_MODEL, LANES)), full((1, LANES))],
        out_specs=[pl.BlockSpec((tm * SUBLANES, LANES), lambda i: (i, 0)),
                   pl.BlockSpec((tm, LANES), lambda i: (i, 0))],
        out_shape=[jax.ShapeDtypeStruct((n * SUBLANES, LANES), F32),
                   jax.ShapeDtypeStruct((n, LANES), F32)],
        compiler_params=_cparams(("arbitrary",)),
        name="post",
    )(xp, xs, sbap, sbas, convp, convs, memp, mems, g0, b0, wo_bf16, g1, b1, wr3, br)


def _moe_plan(route, n):
    tiles = n // MOE_TM + N_GROUPS
    e1 = route[:, 0].astype(jnp.int32)
    e2 = route[:, 1].astype(jnp.int32)
    g = e1 // EXPERTS_PER_GROUP
    _, tok_s, l1_s, l2_s, w1_s, w2_s = lax.sort(
        (g, jnp.arange(n, dtype=jnp.int32), e1 % EXPERTS_PER_GROUP, e2 % EXPERTS_PER_GROUP,
         route[:, 2], route[:, 3]), num_keys=1, is_stable=True)
    counts = jnp.sum((g[:, None] == jnp.arange(N_GROUPS, dtype=jnp.int32)[None, :]).astype(jnp.int32), axis=0)
    ntile = (counts + MOE_TM - 1) // MOE_TM
    tile_end = jnp.cumsum(ntile)
    tile_start = tile_end - ntile
    first = jnp.cumsum(counts) - counts
    n_active = tile_end[-1]
    t = jnp.arange(tiles, dtype=jnp.int32)
    tg = jnp.minimum(jnp.sum((t[:, None] >= tile_end[None, :]).astype(jnp.int32), axis=1), N_GROUPS - 1)
    done = (t - tile_start[tg]) * MOE_TM
    n_valid = jnp.where(t < n_active, jnp.clip(counts[tg] - done, 0, MOE_TM), 0)
    r = jnp.arange(MOE_TM, dtype=jnp.int32)
    valid = r[None, :] < n_valid[:, None]
    start = jnp.clip(first[tg] + done, 0, n)

    def runs(a):
        a = jnp.concatenate([a, jnp.zeros((MOE_TM,), a.dtype)])
        return jax.vmap(lambda s0: lax.dynamic_slice(a, (s0,), (MOE_TM,)))(start)

    tok = runs(tok_s)
    src = jnp.where(valid, tok, 0)
    dst = jnp.where(valid, tok, n + r[None, :])
    ids = jnp.arange(EXPERTS_PER_GROUP, dtype=jnp.int32)
    gate = (jnp.where(runs(l1_s)[..., None] == ids, runs(w1_s)[..., None], 0.0)
            + jnp.where(runs(l2_s)[..., None] == ids, runs(w2_s)[..., None], 0.0))
    gate = jnp.where(valid[..., None], gate, 0.0).reshape(tiles * MOE_TM, EXPERTS_PER_GROUP)
    as_rows = lambda a: (a * SUBLANES).astype(jnp.int32).reshape(tiles, 1, MOE_TM)
    return tg.astype(jnp.int32), n_active.reshape(1).astype(jnp.int32), as_rows(src), as_rows(dst), gate


def _moe_kernel(tg_ref, na_ref, src0_ref, src1_ref, dst_ref, gate_ref, x_hbm, wg_ref, wu_ref, wd_ref,
                y_hbm, xg_ref, xb_ref, acc_ref, yo_ref, gsem, ssem):
    t = pl.program_id(0)
    j = pl.program_id(1)
    n_active = na_ref[0]
    tile_rows = MOE_TM * SUBLANES

    def token(ref, row0):
        return ref.at[pl.ds(pl.multiple_of(row0, SUBLANES), SUBLANES)]

    def start_gather(src_ref, slot):
        def body(i, carry):
            for k in range(MOE_UNROLL):
                r = i * MOE_UNROLL + k
                pltpu.make_async_copy(token(x_hbm, src_ref[0, 0, r]), token(xg_ref.at[slot], r * SUBLANES),
                                      gsem.at[slot]).start(priority=k % 2)
            return carry
        lax.fori_loop(0, MOE_TM // MOE_UNROLL, body, 0)

    def wait_gather(slot):
        pltpu.make_async_copy(x_hbm.at[pl.ds(0, tile_rows)], xg_ref.at[slot], gsem.at[slot]).wait()

    def wait_scatter():
        pltpu.make_async_copy(yo_ref, y_hbm.at[pl.ds(0, tile_rows)], ssem.at[0]).wait()

    @pl.when(t < n_active)
    def _():
        slot = lax.rem(t, 2)

        @pl.when(j == 0)
        def _():
            @pl.when(t == 0)
            def _():
                start_gather(src0_ref, 0)
                yo_ref[...] = jnp.zeros_like(yo_ref)
                sink = pltpu.make_async_copy(yo_ref, y_hbm.at[pl.ds(y_hbm.shape[0] - tile_rows, tile_rows)],
                                             ssem.at[0])
                sink.start()
                sink.wait()

            @pl.when(t + 1 < n_active)
            def _():
                start_gather(src1_ref, 1 - slot)

            wait_gather(slot)
            xb_ref[...] = _load_token_tiles(xg_ref.at[slot]).astype(BF16)
            acc_ref[...] = jnp.zeros_like(acc_ref)

        x = xb_ref[...]
        a = _dot(x, wg_ref[0])
        u = _dot(x, wu_ref[0])
        gate = gate_ref[...]
        lane = lax.broadcasted_iota(jnp.int32, gate.shape, 1)
        ge = jnp.sum(jnp.where(lane == j, gate, 0.0), axis=1, keepdims=True)
        hid = (a * jax.nn.sigmoid(a)) * u * ge
        acc_ref[...] += _dot(hid.astype(BF16), wd_ref[0])

        @pl.when(j == EXPERTS_PER_GROUP - 1)
        def _():
            @pl.when(t > 0)
            def _():
                wait_scatter()

            _store_token_tiles(yo_ref, acc_ref[...])

            def body(i, carry):
                for k in range(MOE_UNROLL):
                    r = i * MOE_UNROLL + k
                    pltpu.make_async_copy(token(yo_ref, r * SUBLANES), token(y_hbm, dst_ref[0, 0, r]),
                                          ssem.at[0]).start(priority=k % 2)
                return carry
            lax.fori_loop(0, MOE_TM // MOE_UNROLL, body, 0)

            @pl.when(t == n_active - 1)
            def _():
                wait_scatter()


def _moe(x1, plan, w_eg, w_eu, w_ed):
    tg, n_active, src, dst, gate = plan
    tiles = src.shape[0]
    tile_rows = MOE_TM * SUBLANES
    idx = lambda f: pl.BlockSpec((1, 1, MOE_TM), f, memory_space=pltpu.SMEM)
    expert = lambda a, b: pl.BlockSpec((1, a, b), lambda t, j, tg, na: (tg[t] * EXPERTS_PER_GROUP + j, 0, 0))
    grid_spec = pltpu.PrefetchScalarGridSpec(
        num_scalar_prefetch=2,
        grid=(tiles, EXPERTS_PER_GROUP),
        in_specs=[
            idx(lambda t, j, tg, na: (t, 0, 0)),
            idx(lambda t, j, tg, na: (jnp.minimum(t + 1, tiles - 1), 0, 0)),
            idx(lambda t, j, tg, na: (t, 0, 0)),
            pl.BlockSpec((MOE_TM, EXPERTS_PER_GROUP), lambda t, j, tg, na: (t, 0)),
            pl.BlockSpec(memory_space=pl.ANY),
            expert(D_MODEL, D_EXPERT), expert(D_MODEL, D_EXPERT), expert(D_EXPERT, D_MODEL),
        ],
        out_specs=pl.BlockSpec(memory_space=pl.ANY),
        scratch_shapes=[pltpu.VMEM((2, tile_rows, LANES), F32), pltpu.VMEM((MOE_TM, D_MODEL), BF16),
                        pltpu.VMEM((MOE_TM, D_MODEL), F32), pltpu.VMEM((tile_rows, LANES), F32),
                        pltpu.SemaphoreType.DMA((2,)), pltpu.SemaphoreType.DMA((1,))],
    )
    return pl.pallas_call(
        _moe_kernel,
        grid_spec=grid_spec,
        out_shape=jax.ShapeDtypeStruct((x1.shape[0] + tile_rows, LANES), F32),
        compiler_params=_cparams(("arbitrary", "arbitrary")),
        name="moe",
    )(tg, n_active, src, src, dst, gate, x1, w_eg, w_eu, w_ed)


def _combine_kernel(x1_ref, y_ref, g_ref, b_ref, o_ref):
    o_ref[...] = _ln(DEEPNORM_ALPHA * _load_token_tiles(x1_ref) + _load_token_tiles(y_ref), g_ref[...], b_ref[...])


def _combine(x1, y, g2, b2, row0, rows, tm):
    b0 = row0 // tm
    tiles = pl.BlockSpec((tm * SUBLANES, LANES), lambda i: (i + b0, 0))
    full = pl.BlockSpec((1, D_MODEL), lambda i: (0, 0))
    return pl.pallas_call(
        _combine_kernel,
        grid=(rows // tm,),
        in_specs=[tiles, tiles, full, full],
        out_specs=pl.BlockSpec((tm, D_MODEL), lambda i: (i, 0)),
        out_shape=jax.ShapeDtypeStruct((rows, D_MODEL), F32),
        compiler_params=_cparams(("parallel",)),
        name="combine",
    )(x1, y, g2, b2)


def _later_or_same(n):
    r = lax.broadcasted_iota(jnp.int32, (n, n), 0)
    c = lax.broadcasted_iota(jnp.int32, (n, n), 1)
    return (r >= c).astype(BF16)


def kernel(x_prompt, x_sample, mem_prompt, cache_sba_k, cache_sba_v, cache_conv, cache_mem_k, cache_mem_v,
           ln0_g, ln0_b, w_in, w_dw, b_dw, lnc_g, lnc_b, w_cpw, w_mk, w_mv, w_out, ln1_g, ln1_b,
           w_rg, b_rg, w_re, b_re, w_eg, w_eu, w_ed, ln2_g, ln2_b):
    bp, tp, _ = x_prompt.shape
    bs, ts, _ = x_sample.shape
    past = cache_sba_k.shape[2]
    l = 0
    r2 = lambda a: a.reshape(1, -1)

    w_in_b = w_in[l].astype(BF16)
    w_out_b = w_out[l].astype(BF16)
    w_cpw_b = w_cpw[l].astype(BF16)
    wr = jnp.zeros((D_MODEL, LANES), F32)
    wr = wr.at[:, 0:N_EXPERTS].set(w_re[l]).at[:, N_EXPERTS:N_EXPERTS + N_GROUPS].set(w_rg[l])
    wr0 = wr.astype(BF16)
    wr1 = (wr - wr0.astype(F32)).astype(BF16)
    wr3 = jnp.stack([wr0, wr1])
    br = jnp.zeros((1, LANES), F32)
    br = br.at[0, 0:N_EXPERTS].set(b_re[l]).at[0, N_EXPERTS:N_EXPERTS + N_GROUPS].set(b_rg[l])
    g0, b0 = r2(ln0_g), r2(ln0_b)

    tq = 256
    tk_s = 256
    u_p = _later_or_same(tq)
    u_s = _later_or_same(tk_s)
    u_n = _later_or_same(ts)

    xp2d = x_prompt.reshape(bp * tp, D_MODEL)
    xs2d = x_sample.reshape(bs * ts, D_MODEL)
    wkt = w_in[l][:, SBA_W:2 * SBA_W].T.astype(BF16)
    wvt = w_in[l][:, 2 * SBA_W:3 * SBA_W].T.astype(BF16)
    qp, ktbp, vtbp, ktp, vtp, glup, mqp = _in_proj_t(xp2d, g0, b0, w_in_b, wkt, wvt, bp, tp, 512, tq)
    qs, kbs, vbs, ks, vs, glus, mqs = _in_proj(xs2d, g0, b0, w_in_b, 512)

    sba_p = _sba_prompt(qp, ktbp, vtbp, u_p, bp, tp, tq, 4)
    kc = jnp.transpose(cache_sba_k[l], (0, 2, 3, 1)).reshape(bs, SBA_W, past)
    vc = jnp.transpose(cache_sba_v[l], (0, 2, 3, 1)).reshape(bs, SBA_W, past)
    sba_s = _sba_sample(qs, kbs, vbs, kc, vc, u_s, u_n, bs, ts, tk_s, 4)

    conv_w = (w_dw[l], r2(b_dw[l]), r2(lnc_g[l]), r2(lnc_b[l]), w_cpw_b)
    h0_p = jnp.zeros((bp, HALO, CONV_W), F32)
    h0_s = jnp.pad(cache_conv[l], ((0, 0), (HALO - (CONV_K - 1), 0), (0, 0)))
    conv_p = _conv_module(h0_p, glup, *conv_w, bp, tp, 256)
    conv_s = _conv_module(h0_s, glus, *conv_w, bs, ts, ts)

    mkt, mvt = _mem_kv(mem_prompt.reshape(bp * N_MEM, D_MODEL), w_mk[l].T.astype(BF16),
                       w_mv[l].T.astype(BF16), bp)
    mem_p = _mem_attn(mqp, mkt, mvt, bp, tp, 512)
    mem_s = _mem_attn(mqs, jnp.transpose(cache_mem_k[l], (0, 2, 3, 1)).reshape(bs, MEM_W, N_MEM),
                      jnp.transpose(cache_mem_v[l], (0, 2, 3, 1)).reshape(bs, MEM_W, N_MEM), bs, ts, ts)

    post_w = (g0, b0, w_out_b, r2(ln1_g[l]), r2(ln1_b[l]), wr3, br)
    x1, route = _post(xp2d, xs2d, sba_p, sba_s, conv_p, conv_s, mem_p, mem_s, *post_w, 512)
    f = _moe(x1, _moe_plan(route, route.shape[0]), w_eg[l].astype(BF16), w_eu[l].astype(BF16), w_ed[l].astype(BF16))
    g2, b2 = r2(ln2_g[l]), r2(ln2_b[l])
    yp = _combine(x1, f, g2, b2, 0, bp * tp, 512)
    ys = _combine(x1, f, g2, b2, bp * tp, bs * ts, 512)

    hd = (SBA_HEADS, HEAD_DIM)
    glup3 = glup.reshape(bp, tp, CONV_W)
    glus3 = glus.reshape(bs, ts, CONV_W)
    conv_tail_s = jnp.concatenate([cache_conv[l], glus3], axis=1)[:, -(CONV_K - 1):]
    return (
        yp.reshape(bp, tp, D_MODEL),
        ys.reshape(bs, ts, D_MODEL),
        jnp.transpose(ktp.reshape(bp, *hd, tp), (0, 3, 1, 2))[None],
        jnp.transpose(vtp.reshape(bp, *hd, tp), (0, 3, 1, 2))[None],
        glup3[:, -(CONV_K - 1):][None],
        jnp.transpose(mkt.reshape(bp, MEM_HEADS, HEAD_DIM, N_MEM), (0, 3, 1, 2))[None],
        jnp.transpose(mvt.reshape(bp, MEM_HEADS, HEAD_DIM, N_MEM), (0, 3, 1, 2))[None],
        ks.reshape(1, bs, ts, *hd),
        vs.reshape(1, bs, ts, *hd),
        conv_tail_s[None],
    )
```

```python
import functools

import jax
import jax.numpy as jnp
from jax import lax
from jax.experimental import pallas as pl
from jax.experimental.pallas import tpu as pltpu

F32 = jnp.float32
BF16 = jnp.bfloat16

D_MODEL = 1024
HEAD_DIM = 64
SBA_HEADS = 8
SBA_W = SBA_HEADS * HEAD_DIM
CONV_W = 256
CONV_K = 31
MEM_HEADS = 4
MEM_W = MEM_HEADS * HEAD_DIM
N_MEM = 256
N_GROUPS = 4
EXPERTS_PER_GROUP = 8
N_EXPERTS = N_GROUPS * EXPERTS_PER_GROUP
D_EXPERT = 256
DEPTH = 1
DEEPNORM_ALPHA = (2 * DEPTH) ** 0.25
QK_SCALE = HEAD_DIM ** -0.5
LN_EPS = 1e-5
LOG2E = 1.4426950408889634

LANES = 128
SUBLANES = 8
HALO = 32
MOE_TM = 1024
MOE_UNROLL = 8
SP_LINEAR = 100.0
VMEM_LIMIT = 48 * 1024 * 1024


def _cparams(sem):
    return pltpu.CompilerParams(dimension_semantics=sem, vmem_limit_bytes=VMEM_LIMIT)


def _ln(x, g, b):
    mu = jnp.mean(x, axis=-1, keepdims=True)
    xc = x - mu
    var = jnp.mean(xc * xc, axis=-1, keepdims=True)
    return xc * lax.rsqrt(var + LN_EPS) * g + b


def _dot(a, b):
    return jnp.dot(a, b, preferred_element_type=F32)


def _dot_nt(a, b):
    return lax.dot_general(a, b, (((1,), (1,)), ((), ())), preferred_element_type=F32)


def _in_proj_kernel(x_ref, g_ref, b_ref, w_ref, q_ref, kb_ref, vb_ref, k_ref, v_ref, glu_ref, mq_ref):
    xn = _ln(x_ref[...], g_ref[...], b_ref[...]).astype(BF16)
    q = _dot(xn, w_ref[:, 0:SBA_W])
    q_ref[...] = (q * (QK_SCALE * LOG2E)).astype(BF16)
    k = _dot(xn, w_ref[:, SBA_W:2 * SBA_W])
    k_ref[...] = k
    kb_ref[...] = k.astype(BF16)
    v = _dot(xn, w_ref[:, 2 * SBA_W:3 * SBA_W])
    v_ref[...] = v
    vb_ref[...] = v.astype(BF16)
    c0 = 3 * SBA_W
    cv = _dot(xn, w_ref[:, c0:c0 + CONV_W])
    cg = _dot(xn, w_ref[:, c0 + CONV_W:c0 + 2 * CONV_W])
    glu_ref[...] = cv * jax.nn.sigmoid(cg)
    mq = _dot(xn, w_ref[:, c0 + 2 * CONV_W:c0 + 2 * CONV_W + MEM_W])
    mq_ref[...] = (mq * QK_SCALE).astype(BF16)


def _in_proj_t_kernel(x_ref, g_ref, b_ref, w_ref, wkt_ref, wvt_ref,
                      q_ref, ktb_ref, vtb_ref, kt_ref, vt_ref, glu_ref, mq_ref, *, tq):
    xn = _ln(x_ref[...], g_ref[...], b_ref[...]).astype(BF16)
    tm = xn.shape[0]
    q = _dot(xn, w_ref[:, 0:SBA_W])
    q_ref[...] = (q * (QK_SCALE * LOG2E)).astype(BF16)
    for wt_ref, t_ref, tb_ref in ((wkt_ref, kt_ref, ktb_ref), (wvt_ref, vt_ref, vtb_ref)):
        t = _dot_nt(wt_ref[...], xn)
        t_ref[0] = t
        for i in range(tm // tq):
            tb_ref[0, i] = t[:, i * tq:(i + 1) * tq].astype(BF16)
    c0 = 3 * SBA_W
    cv = _dot(xn, w_ref[:, c0:c0 + CONV_W])
    cg = _dot(xn, w_ref[:, c0 + CONV_W:c0 + 2 * CONV_W])
    glu_ref[...] = cv * jax.nn.sigmoid(cg)
    mq = _dot(xn, w_ref[:, c0 + 2 * CONV_W:c0 + 2 * CONV_W + MEM_W])
    mq_ref[...] = (mq * QK_SCALE).astype(BF16)


def _in_proj_t(x2d, g, b, w_bf16, wkt, wvt, batch, seq, tm, tq):
    n = x2d.shape[0]
    in_w = w_bf16.shape[1]
    nt = seq // tm
    row = lambda w: pl.BlockSpec((tm, w), lambda bi, i: (bi * nt + i, 0))
    full = lambda s: pl.BlockSpec(s, lambda bi, i: (0, 0))
    tr = pl.BlockSpec((1, SBA_W, tm), lambda bi, i: (bi, 0, i))
    trb = pl.BlockSpec((1, tm // tq, SBA_W, tq), lambda bi, i: (bi, i, 0, 0))
    return pl.pallas_call(
        functools.partial(_in_proj_t_kernel, tq=tq),
        grid=(batch, nt),
        in_specs=[row(D_MODEL), full((1, D_MODEL)), full((1, D_MODEL)), full((D_MODEL, in_w)),
                  full((SBA_W, D_MODEL)), full((SBA_W, D_MODEL))],
        out_specs=[row(SBA_W), trb, trb, tr, tr, row(CONV_W), row(MEM_W)],
        out_shape=[
            jax.ShapeDtypeStruct((n, SBA_W), BF16),
            jax.ShapeDtypeStruct((batch, seq // tq, SBA_W, tq), BF16),
            jax.ShapeDtypeStruct((batch, seq // tq, SBA_W, tq), BF16),
            jax.ShapeDtypeStruct((batch, SBA_W, seq), F32),
            jax.ShapeDtypeStruct((batch, SBA_W, seq), F32),
            jax.ShapeDtypeStruct((n, CONV_W), F32),
            jax.ShapeDtypeStruct((n, MEM_W), BF16),
        ],
        compiler_params=_cparams(("parallel", "parallel")),
        name="in_proj_t",
    )(x2d, g, b, w_bf16, wkt, wvt)


def _in_proj(x2d, g, b, w_bf16, tm):
    n = x2d.shape[0]
    in_w = w_bf16.shape[1]
    row = lambda w: pl.BlockSpec((tm, w), lambda i: (i, 0))
    full = lambda s: pl.BlockSpec(s, lambda i: (0, 0))
    return pl.pallas_call(
        _in_proj_kernel,
        grid=(n // tm,),
        in_specs=[row(D_MODEL), full((1, D_MODEL)), full((1, D_MODEL)), full((D_MODEL, in_w))],
        out_specs=[row(SBA_W), row(SBA_W), row(SBA_W), row(SBA_W), row(SBA_W), row(CONV_W), row(MEM_W)],
        out_shape=[
            jax.ShapeDtypeStruct((n, SBA_W), BF16),
            jax.ShapeDtypeStruct((n, SBA_W), BF16),
            jax.ShapeDtypeStruct((n, SBA_W), BF16),
            jax.ShapeDtypeStruct((n, SBA_W), F32),
            jax.ShapeDtypeStruct((n, SBA_W), F32),
            jax.ShapeDtypeStruct((n, CONV_W), F32),
            jax.ShapeDtypeStruct((n, MEM_W), BF16),
        ],
        compiler_params=_cparams(("parallel",)),
        name="in_proj",
    )(x2d, g, b, w_bf16)


def _head_masks(shape):
    lane = lax.broadcasted_iota(jnp.int32, shape, 1)
    return lane < HEAD_DIM


def _stack_heads(q, low):
    zero = jnp.zeros_like(q)
    return jnp.concatenate([jnp.where(low, q, zero), jnp.where(low, zero, q)], axis=0)


def _sba_tile(z, u, mask):
    sp = jnp.where(z > SP_LINEAR, z, jnp.log(1.0 + jnp.exp2(z)) * LOG2E)
    if mask is not None:
        sp = jnp.where(mask, sp, 0.0)
    w = jnp.exp2(z - _dot(sp.astype(BF16), u))
    if mask is not None:
        w = jnp.where(mask, w, 0.0)
    return w.astype(BF16), jnp.sum(sp, axis=1, keepdims=True)


def _sba_prompt_kernel(q_ref, k_ref, v_ref, u_ref, o_ref, *, tq, hp):
    qi = pl.program_id(2)
    low = _head_masks((tq, LANES))
    u = u_ref[...]
    row = lax.broadcasted_iota(jnp.int32, (2 * tq, tq), 0) & (tq - 1)
    col = lax.broadcasted_iota(jnp.int32, (2 * tq, tq), 1)
    qs = [_stack_heads(q_ref[:, p * LANES:(p + 1) * LANES], low) for p in range(hp)]

    def blk(j, carry, mask):
        out = []
        for p in range(hp):
            c, acc = carry[p]
            z = _dot(qs[p], k_ref[0, j, p * LANES:(p + 1) * LANES, :])
            w, rs = _sba_tile(z, u, mask)
            acc = acc + _dot_nt(w, v_ref[0, j, p * LANES:(p + 1) * LANES, :]) * jnp.exp2(-c)
            out.append((c + rs, acc))
        return tuple(out)

    carry = tuple((jnp.zeros((2 * tq, 1), F32), jnp.zeros((2 * tq, LANES), F32)) for _ in range(hp))
    carry = blk(qi, carry, col < row)
    carry = lax.fori_loop(0, qi, lambda t, cr: blk(qi - 1 - t, cr, None), carry)
    for p in range(hp):
        acc = carry[p][1]
        o_ref[:, p * LANES:(p + 1) * LANES] = jnp.where(low, acc[0:tq], acc[tq:2 * tq]).astype(o_ref.dtype)


def _sba_prompt(q, kt, vt, u, batch, seq, tq, hp):
    n = q.shape[0]
    nq = seq // tq
    groups = SBA_W // (LANES * hp)
    kv = pl.BlockSpec((1, nq, hp * LANES, tq), lambda b, p, i: (b, 0, p, 0))
    return pl.pallas_call(
        functools.partial(_sba_prompt_kernel, tq=tq, hp=hp),
        grid=(batch, groups, nq),
        in_specs=[
            pl.BlockSpec((tq, hp * LANES), lambda b, p, i: (b * nq + i, p)),
            kv, kv,
            pl.BlockSpec((tq, tq), lambda b, p, i: (0, 0)),
        ],
        out_specs=pl.BlockSpec((tq, hp * LANES), lambda b, p, i: (b * nq + i, p)),
        out_shape=jax.ShapeDtypeStruct((n, SBA_W), BF16),
        compiler_params=_cparams(("parallel", "parallel", "arbitrary")),
        name="sba_prompt",
    )(q, kt, vt, u)


def _sba_sample_kernel(q_ref, kn_ref, vn_ref, kc_ref, vc_ref, u_ref, un_ref, o_ref, qs_ref, c_ref, acc_ref,
                       *, tn, tk, nsub):
    j = pl.program_id(1)
    nj = pl.num_programs(1)
    low = _head_masks((tn, LANES))
    pairs = SBA_W // LANES
    rows = 2 * tn

    def step(logits, weighted, u, mask):
        z = jnp.concatenate([logits(qs_ref[p], p) for p in range(pairs)], axis=0)
        w, rs = _sba_tile(z, u, mask)
        pv = jnp.concatenate([weighted(w[p * rows:(p + 1) * rows], p) for p in range(pairs)], axis=0)
        c = c_ref[...]
        acc_ref[...] += pv * jnp.exp2(-c)
        c_ref[...] = c + rs

    def lanes(p):
        return slice(p * LANES, (p + 1) * LANES)

    @pl.when(j == 0)
    def _():
        for p in range(pairs):
            qs_ref[p] = _stack_heads(q_ref[:, lanes(p)], low)
        c_ref[...] = jnp.zeros_like(c_ref)
        acc_ref[...] = jnp.zeros_like(acc_ref)
        row = lax.broadcasted_iota(jnp.int32, (pairs * rows, tn), 0) & (tn - 1)
        col = lax.broadcasted_iota(jnp.int32, (pairs * rows, tn), 1)
        step(lambda q, p: _dot_nt(q, kn_ref[:, lanes(p)]), lambda w, p: _dot(w, vn_ref[:, lanes(p)]),
             un_ref[...], col < row)

    for s in reversed(range(nsub)):
        keys = slice(s * tk, (s + 1) * tk)
        step(lambda q, p: _dot(q, kc_ref[0, lanes(p), keys].astype(BF16)),
             lambda w, p: _dot_nt(w, vc_ref[0, lanes(p), keys].astype(BF16)),
             u_ref[...], None)

    @pl.when(j == nj - 1)
    def _():
        for p in range(pairs):
            a0 = acc_ref[p * rows:p * rows + tn, :]
            a1 = acc_ref[p * rows + tn:(p + 1) * rows, :]
            o_ref[:, lanes(p)] = jnp.where(low, a0, a1).astype(o_ref.dtype)


def _sba_sample(q, kn, vn, kc, vc, u, un, batch, tn, tk, nsub):
    n = q.shape[0]
    past = kc.shape[2]
    tkb = tk * nsub
    nj = past // tkb
    rows = SBA_HEADS * tn
    cache = pl.BlockSpec((1, SBA_W, tkb), lambda b, j: (b, 0, nj - 1 - j))
    return pl.pallas_call(
        functools.partial(_sba_sample_kernel, tn=tn, tk=tk, nsub=nsub),
        grid=(batch, nj),
        in_specs=[
            pl.BlockSpec((tn, SBA_W), lambda b, j: (b, 0)),
            pl.BlockSpec((tn, SBA_W), lambda b, j: (b, 0)),
            pl.BlockSpec((tn, SBA_W), lambda b, j: (b, 0)),
            cache, cache,
            pl.BlockSpec((tk, tk), lambda b, j: (0, 0)),
            pl.BlockSpec((tn, tn), lambda b, j: (0, 0)),
        ],
        out_specs=pl.BlockSpec((tn, SBA_W), lambda b, j: (b, 0)),
        out_shape=jax.ShapeDtypeStruct((n, SBA_W), BF16),
        scratch_shapes=[pltpu.VMEM((SBA_W // LANES, 2 * tn, LANES), BF16),
                        pltpu.VMEM((rows, 1), F32), pltpu.VMEM((rows, LANES), F32)],
        compiler_params=_cparams(("parallel", "arbitrary")),
        name="sba_sample",
    )(q, kn, vn, kc, vc, u, un)


def _conv_kernel(h0_ref, glu_ref, wdw_ref, bdw_ref, g_ref, b_ref, wpw_ref, o_ref, buf_ref, *, tt, rc):
    ti = pl.program_id(1)
    rows = HALO + tt

    @pl.when(ti == 0)
    def _():
        buf_ref[0, 0:HALO, :] = h0_ref[0]

    @pl.when(ti > 0)
    def _():
        buf_ref[0, 0:HALO, :] = buf_ref[0, tt:tt + HALO, :]

    buf_ref[0, HALO:rows, :] = glu_ref[...]
    buf_ref[0, rows:rows + SUBLANES, :] = jnp.zeros((SUBLANES, CONV_W), F32)
    for s in range(1, SUBLANES):
        buf_ref[s, 0:rows, :] = buf_ref[0, pl.ds(s, rows), :]
    base = HALO - (CONV_K - 1)
    wdw = wdw_ref[...]
    for r0 in range(0, tt, rc):
        acc = jnp.zeros((rc, CONV_W), F32)
        for kk in range(CONV_K):
            off = base + r0 + kk
            acc = acc + buf_ref[off % SUBLANES, pl.ds(off - off % SUBLANES, rc), :] * wdw[kk:kk + 1, :]
        u = _ln(acc + bdw_ref[...], g_ref[...], b_ref[...])
        s = u * jax.nn.sigmoid(u)
        o_ref[pl.ds(r0, rc), :] = _dot(s.astype(BF16), wpw_ref[...]).astype(o_ref.dtype)


def _conv_module(h0, glu, w_dw, b_dw, g, b, w_pw_bf16, batch, seq, tt):
    n = glu.shape[0]
    nt = seq // tt
    rc = min(tt, 64)
    full = lambda s: pl.BlockSpec(s, lambda bi, ti: (0,) * len(s))
    return pl.pallas_call(
        functools.partial(_conv_kernel, tt=tt, rc=rc),
        grid=(batch, nt),
        in_specs=[
            pl.BlockSpec((1, HALO, CONV_W), lambda bi, ti: (bi, 0, 0)),
            pl.BlockSpec((tt, CONV_W), lambda bi, ti: (bi * nt + ti, 0)),
            full((CONV_K, CONV_W)), full((1, CONV_W)), full((1, CONV_W)), full((1, CONV_W)),
            full((CONV_W, CONV_W)),
        ],
        out_specs=pl.BlockSpec((tt, CONV_W), lambda bi, ti: (bi * nt + ti, 0)),
        out_shape=jax.ShapeDtypeStruct((n, CONV_W), BF16),
        scratch_shapes=[pltpu.VMEM((SUBLANES, HALO + tt + SUBLANES, CONV_W), F32)],
        compiler_params=_cparams(("parallel", "arbitrary")),
        name="conv_module",
    )(h0, glu, w_dw, b_dw, g, b, w_pw_bf16)


def _mem_kv_kernel(m_ref, wkt_ref, wvt_ref, k_ref, v_ref):
    m = m_ref[...].astype(BF16)
    k_ref[0] = _dot_nt(wkt_ref[...], m)
    v_ref[0] = _dot_nt(wvt_ref[...], m)


def _mem_kv(mem2d, wkt, wvt, batch):
    full = pl.BlockSpec((MEM_W, D_MODEL), lambda i: (0, 0))
    out = pl.BlockSpec((1, MEM_W, N_MEM), lambda i: (i, 0, 0))
    return pl.pallas_call(
        _mem_kv_kernel,
        grid=(batch,),
        in_specs=[pl.BlockSpec((N_MEM, D_MODEL), lambda i: (i, 0)), full, full],
        out_specs=[out, out],
        out_shape=[jax.ShapeDtypeStruct((batch, MEM_W, N_MEM), F32)] * 2,
        compiler_params=_cparams(("parallel",)),
        name="mem_kv",
    )(mem2d, wkt, wvt)


def _mem_attn_kernel(q_ref, k_ref, v_ref, o_ref):
    q = q_ref[...]
    tq = q.shape[0]
    low = _head_masks((tq, LANES))
    for p in range(MEM_W // LANES):
        qs = _stack_heads(q[:, p * LANES:(p + 1) * LANES], low)
        kt = k_ref[0, p * LANES:(p + 1) * LANES, :].astype(BF16)
        vt = v_ref[0, p * LANES:(p + 1) * LANES, :].astype(BF16)
        s = _dot(qs, kt)
        e = jnp.exp(s - jnp.max(s, axis=-1, keepdims=True))
        o = _dot_nt(e.astype(BF16), vt) / jnp.sum(e, axis=-1, keepdims=True)
        o_ref[:, p * LANES:(p + 1) * LANES] = jnp.where(low, o[0:tq], o[tq:2 * tq]).astype(o_ref.dtype)


def _mem_attn(mq, mkt, mvt, batch, seq, tq):
    n = mq.shape[0]
    nq = seq // tq
    kv = pl.BlockSpec((1, MEM_W, N_MEM), lambda b, i: (b, 0, 0))
    return pl.pallas_call(
        _mem_attn_kernel,
        grid=(batch, nq),
        in_specs=[pl.BlockSpec((tq, MEM_W), lambda b, i: (b * nq + i, 0)), kv, kv],
        out_specs=pl.BlockSpec((tq, MEM_W), lambda b, i: (b * nq + i, 0)),
        out_shape=jax.ShapeDtypeStruct((n, MEM_W), BF16),
        compiler_params=_cparams(("parallel", "parallel")),
        name="mem_attn",
    )(mq, mkt, mvt)


def _store_token_tiles(ref, x):
    rows = x.shape[0]
    for c in range(SUBLANES):
        ref[pl.ds(c, rows, stride=SUBLANES), :] = x[:, c * LANES:(c + 1) * LANES]


def _load_token_tiles(ref):
    rows = ref.shape[0] // SUBLANES
    return jnp.concatenate([ref[pl.ds(c, rows, stride=SUBLANES), :] for c in range(SUBLANES)], axis=1)


def _split2(x):
    a = x.astype(BF16)
    b = (x - a.astype(F32)).astype(BF16)
    return a, b


def _post_body(x_ref, sba_ref, conv_ref, mem_ref, g0_ref, b0_ref, wo_ref, g1_ref, b1_ref,
               wr_ref, br_ref, x1_ref, route_ref):
    xn = _ln(x_ref[...], g0_ref[...], b0_ref[...])
    mix = _dot(sba_ref[...], wo_ref[0:SBA_W, :])
    mix = mix + _dot(conv_ref[...], wo_ref[SBA_W:SBA_W + CONV_W, :])
    mix = mix + _dot(mem_ref[...], wo_ref[SBA_W + CONV_W:SBA_W + CONV_W + MEM_W, :])
    x1 = _ln(DEEPNORM_ALPHA * xn + mix, g1_ref[...], b1_ref[...])
    _store_token_tiles(x1_ref, x1)

    a0, a1 = _split2(x1)
    w0, w1 = wr_ref[0], wr_ref[1]
    logits = _dot(a0, w0) + (_dot(a0, w1) + _dot(a1, w0)) + br_ref[...]
    tm = logits.shape[0]
    lane = lax.broadcasted_iota(jnp.int32, (tm, LANES), 1).astype(F32)
    neg = jnp.float32(-jnp.inf)
    big = jnp.float32(LANES)
    is_g = jnp.logical_and(lane >= N_EXPERTS, lane < N_EXPERTS + N_GROUPS)
    gl = jnp.where(is_g, logits, neg)
    gmax = jnp.max(gl, axis=-1, keepdims=True)
    g_idx = jnp.min(jnp.where(gl == gmax, lane, big), axis=-1, keepdims=True) - N_EXPERTS
    g_w = 1.0 / jnp.sum(jnp.exp(gl - gmax), axis=-1, keepdims=True)
    in_grp = jnp.logical_and(lane >= g_idx * EXPERTS_PER_GROUP, lane < (g_idx + 1.0) * EXPERTS_PER_GROUP)
    el = jnp.where(in_grp, logits, neg)
    v1 = jnp.max(el, axis=-1, keepdims=True)
    i1 = jnp.min(jnp.where(el == v1, lane, big), axis=-1, keepdims=True)
    el2 = jnp.where(lane == i1, neg, el)
    v2 = jnp.max(el2, axis=-1, keepdims=True)
    i2 = jnp.min(jnp.where(el2 == v2, lane, big), axis=-1, keepdims=True)
    e2 = jnp.exp(v2 - v1)
    p1 = 1.0 / (1.0 + e2)
    p2 = e2 / (1.0 + e2)
    route_ref[...] = jnp.where(lane == 0.0, i1, jnp.where(lane == 1.0, i2, jnp.where(
        lane == 2.0, p1 * g_w, jnp.where(lane == 3.0, p2 * g_w, 0.0))))


def _post_kernel(xp_ref, xs_ref, sbap_ref, sbas_ref, convp_ref, convs_ref, memp_ref, mems_ref,
                 g0_ref, b0_ref, wo_ref, g1_ref, b1_ref, wr_ref, br_ref, x1_ref, route_ref, *, prompt_tiles):
    i = pl.program_id(0)
    shared = (g0_ref, b0_ref, wo_ref, g1_ref, b1_ref, wr_ref, br_ref, x1_ref, route_ref)

    @pl.when(i < prompt_tiles)
    def _():
        _post_body(xp_ref, sbap_ref, convp_ref, memp_ref, *shared)

    @pl.when(i >= prompt_tiles)
    def _():
        _post_body(xs_ref, sbas_ref, convs_ref, mems_ref, *shared)


def _post(xp, xs, sbap, sbas, convp, convs, memp, mems, g0, b0, wo_bf16, g1, b1, wr3, br, tm):
    pt = xp.shape[0] // tm
    st = xs.shape[0] // tm
    n = xp.shape[0] + xs.shape[0]
    prow = lambda w: pl.BlockSpec((tm, w), lambda i: (jnp.minimum(i, pt - 1), 0))
    srow = lambda w: pl.BlockSpec((tm, w), lambda i: (jnp.maximum(i - pt, 0), 0))
    full = lambda s: pl.BlockSpec(s, lambda i: (0,) * len(s))
    widths = (D_MODEL, SBA_W, CONV_W, MEM_W)
    return pl.pallas_call(
        functools.partial(_post_kernel, prompt_tiles=pt),
        grid=(pt + st,),
        in_specs=[spec(w) for w in widths for spec in (prow, srow)] + [
            full((1, D_MODEL)), full((1, D_MODEL)), full((D_MODEL, D_MODEL)),
            full((1, D_MODEL)), full((1, D_MODEL)),
            full((2, D_MODEL, LANES)), full((1, LANES))],
        out_specs=[pl.BlockSpec((tm * SUBLANES, LANES), lambda i: (i, 0)),
                   pl.BlockSpec((tm, LANES), lambda i: (i, 0))],
        out_shape=[jax.ShapeDtypeStruct((n * SUBLANES, LANES), F32),
                   jax.ShapeDtypeStruct((n, LANES), F32)],
        compiler_params=_cparams(("arbitrary",)),
        name="post",
    )(xp, xs, sbap, sbas, convp, convs, memp, mems, g0, b0, wo_bf16, g1, b1, wr3, br)


def _moe_plan(route, n):
    tiles = n // MOE_TM + N_GROUPS
    e1 = route[:, 0].astype(jnp.int32)
    e2 = route[:, 1].astype(jnp.int32)
    g = e1 // EXPERTS_PER_GROUP
    _, tok_s, l1_s, l2_s, w1_s, w2_s = lax.sort(
        (g, jnp.arange(n, dtype=jnp.int32), e1 % EXPERTS_PER_GROUP, e2 % EXPERTS_PER_GROUP,
         route[:, 2], route[:, 3]), num_keys=1, is_stable=True)
    counts = jnp.sum((g[:, None] == jnp.arange(N_GROUPS, dtype=jnp.int32)[None, :]).astype(jnp.int32), axis=0)
    ntile = (counts + MOE_TM - 1) // MOE_TM
    tile_end = jnp.cumsum(ntile)
    tile_start = tile_end - ntile
    first = jnp.cumsum(counts) - counts
    n_active = tile_end[-1]
    t = jnp.arange(tiles, dtype=jnp.int32)
    tg = jnp.minimum(jnp.sum((t[:, None] >= tile_end[None, :]).astype(jnp.int32), axis=1), N_GROUPS - 1)
    done = (t - tile_start[tg]) * MOE_TM
    n_valid = jnp.where(t < n_active, jnp.clip(counts[tg] - done, 0, MOE_TM), 0)
    r = jnp.arange(MOE_TM, dtype=jnp.int32)
    valid = r[None, :] < n_valid[:, None]
    shift = tile_start * MOE_TM - first
    max_shift = N_GROUPS * MOE_TM
    row_group = jnp.repeat(tg, MOE_TM)

    def runs(a):
        ext = jnp.concatenate([jnp.zeros((max_shift,), a.dtype), a, jnp.zeros((tiles * MOE_TM - n,), a.dtype)])
        out = jnp.zeros((tiles * MOE_TM,), a.dtype)
        for gi in range(N_GROUPS):
            moved = lax.dynamic_slice(ext, (max_shift - shift[gi],), (tiles * MOE_TM,))
            out = jnp.where(row_group == gi, moved, out)
        return out.reshape(tiles, MOE_TM)

    tok = runs(tok_s)
    src = jnp.where(valid, tok, 0)
    dst = jnp.where(valid, tok, n + r[None, :])
    ids = jnp.arange(EXPERTS_PER_GROUP, dtype=jnp.int32)
    gate = (jnp.where(runs(l1_s)[..., None] == ids, runs(w1_s)[..., None], 0.0)
            + jnp.where(runs(l2_s)[..., None] == ids, runs(w2_s)[..., None], 0.0))
    gate = jnp.where(valid[..., None], gate, 0.0).reshape(tiles * MOE_TM, EXPERTS_PER_GROUP)
    as_rows = lambda a: (a * SUBLANES).astype(jnp.int32).reshape(tiles, 1, MOE_TM)
    return tg.astype(jnp.int32), n_active.reshape(1).astype(jnp.int32), as_rows(src), as_rows(dst), gate


def _moe_kernel(tg_ref, na_ref, src0_ref, src1_ref, dst_ref, gate_ref, x_hbm, wg_ref, wu_ref, wd_ref,
                y_hbm, xg_ref, xb_ref, acc_ref, yo_ref, gsem, ssem):
    t = pl.program_id(0)
    j = pl.program_id(1)
    n_active = na_ref[0]
    tile_rows = MOE_TM * SUBLANES

    def token(ref, row0):
        return ref.at[pl.ds(pl.multiple_of(row0, SUBLANES), SUBLANES)]

    def start_gather(src_ref, slot):
        def body(i, carry):
            for k in range(MOE_UNROLL):
                r = i * MOE_UNROLL + k
                pltpu.make_async_copy(token(x_hbm, src_ref[0, 0, r]), token(xg_ref.at[slot], r * SUBLANES),
                                      gsem.at[slot]).start(priority=k % 2)
            return carry
        lax.fori_loop(0, MOE_TM // MOE_UNROLL, body, 0)

    def wait_gather(slot):
        pltpu.make_async_copy(x_hbm.at[pl.ds(0, tile_rows)], xg_ref.at[slot], gsem.at[slot]).wait()

    def wait_scatter():
        pltpu.make_async_copy(yo_ref, y_hbm.at[pl.ds(0, tile_rows)], ssem.at[0]).wait()

    @pl.when(t < n_active)
    def _():
        slot = lax.rem(t, 2)

        @pl.when(j == 0)
        def _():
            @pl.when(t == 0)
            def _():
                start_gather(src0_ref, 0)
                yo_ref[...] = jnp.zeros_like(yo_ref)
                sink = pltpu.make_async_copy(yo_ref, y_hbm.at[pl.ds(y_hbm.shape[0] - tile_rows, tile_rows)],
                                             ssem.at[0])
                sink.start()
                sink.wait()

            @pl.when(t + 1 < n_active)
            def _():
                start_gather(src1_ref, 1 - slot)

            wait_gather(slot)
            xb_ref[...] = _load_token_tiles(xg_ref.at[slot]).astype(BF16)
            acc_ref[...] = jnp.zeros_like(acc_ref)

        x = xb_ref[...]
        a = _dot(x, wg_ref[0].astype(BF16))
        u = _dot(x, wu_ref[0].astype(BF16))
        gate = gate_ref[...]
        lane = lax.broadcasted_iota(jnp.int32, gate.shape, 1)
        ge = jnp.sum(jnp.where(lane == j, gate, 0.0), axis=1, keepdims=True)
        hid = (a * jax.nn.sigmoid(a)) * u * ge
        acc_ref[...] += _dot(hid.astype(BF16), wd_ref[0].astype(BF16))

        @pl.when(j == EXPERTS_PER_GROUP - 1)
        def _():
            @pl.when(t > 0)
            def _():
                wait_scatter()

            _store_token_tiles(yo_ref, acc_ref[...])

            def body(i, carry):
                for k in range(MOE_UNROLL):
                    r = i * MOE_UNROLL + k
                    pltpu.make_async_copy(token(yo_ref, r * SUBLANES), token(y_hbm, dst_ref[0, 0, r]),
                                          ssem.at[0]).start(priority=k % 2)
                return carry
            lax.fori_loop(0, MOE_TM // MOE_UNROLL, body, 0)

            @pl.when(t == n_active - 1)
            def _():
                wait_scatter()


def _moe(x1, plan, w_eg, w_eu, w_ed):
    tg, n_active, src, dst, gate = plan
    tiles = src.shape[0]
    tile_rows = MOE_TM * SUBLANES
    idx = lambda f: pl.BlockSpec((1, 1, MOE_TM), f, memory_space=pltpu.SMEM)
    expert = lambda a, b: pl.BlockSpec((1, a, b), lambda t, j, tg, na: (tg[t] * EXPERTS_PER_GROUP + j, 0, 0))
    grid_spec = pltpu.PrefetchScalarGridSpec(
        num_scalar_prefetch=2,
        grid=(tiles, EXPERTS_PER_GROUP),
        in_specs=[
            idx(lambda t, j, tg, na: (t, 0, 0)),
            idx(lambda t, j, tg, na: (jnp.minimum(t + 1, tiles - 1), 0, 0)),
            idx(lambda t, j, tg, na: (t, 0, 0)),
            pl.BlockSpec((MOE_TM, EXPERTS_PER_GROUP), lambda t, j, tg, na: (t, 0)),
            pl.BlockSpec(memory_space=pl.ANY),
            expert(D_MODEL, D_EXPERT), expert(D_MODEL, D_EXPERT), expert(D_EXPERT, D_MODEL),
        ],
        out_specs=pl.BlockSpec(memory_space=pl.ANY),
        scratch_shapes=[pltpu.VMEM((2, tile_rows, LANES), F32), pltpu.VMEM((MOE_TM, D_MODEL), BF16),
                        pltpu.VMEM((MOE_TM, D_MODEL), F32), pltpu.VMEM((tile_rows, LANES), F32),
                        pltpu.SemaphoreType.DMA((2,)), pltpu.SemaphoreType.DMA((1,))],
    )
    return pl.pallas_call(
        _moe_kernel,
        grid_spec=grid_spec,
        out_shape=jax.ShapeDtypeStruct((x1.shape[0] + tile_rows, LANES), F32),
        compiler_params=_cparams(("arbitrary", "arbitrary")),
        name="moe",
    )(tg, n_active, src, src, dst, gate, x1, w_eg, w_eu, w_ed)


def _combine_kernel(x1_ref, y_ref, g_ref, b_ref, o_ref):
    o_ref[...] = _ln(DEEPNORM_ALPHA * _load_token_tiles(x1_ref) + _load_token_tiles(y_ref), g_ref[...], b_ref[...])


def _combine(x1, y, g2, b2, row0, rows, tm):
    b0 = row0 // tm
    tiles = pl.BlockSpec((tm * SUBLANES, LANES), lambda i: (i + b0, 0))
    full = pl.BlockSpec((1, D_MODEL), lambda i: (0, 0))
    return pl.pallas_call(
        _combine_kernel,
        grid=(rows // tm,),
        in_specs=[tiles, tiles, full, full],
        out_specs=pl.BlockSpec((tm, D_MODEL), lambda i: (i, 0)),
        out_shape=jax.ShapeDtypeStruct((rows, D_MODEL), F32),
        compiler_params=_cparams(("parallel",)),
        name="combine",
    )(x1, y, g2, b2)


def _later_or_same(n):
    r = lax.broadcasted_iota(jnp.int32, (n, n), 0)
    c = lax.broadcasted_iota(jnp.int32, (n, n), 1)
    return (r >= c).astype(BF16)


def kernel(x_prompt, x_sample, mem_prompt, cache_sba_k, cache_sba_v, cache_conv, cache_mem_k, cache_mem_v,
           ln0_g, ln0_b, w_in, w_dw, b_dw, lnc_g, lnc_b, w_cpw, w_mk, w_mv, w_out, ln1_g, ln1_b,
           w_rg, b_rg, w_re, b_re, w_eg, w_eu, w_ed, ln2_g, ln2_b):
    bp, tp, _ = x_prompt.shape
    bs, ts, _ = x_sample.shape
    past = cache_sba_k.shape[2]
    l = 0
    r2 = lambda a: a.reshape(1, -1)

    w_in_b = w_in[l].astype(BF16)
    w_out_b = w_out[l].astype(BF16)
    w_cpw_b = w_cpw[l].astype(BF16)
    wr = jnp.zeros((D_MODEL, LANES), F32)
    wr = wr.at[:, 0:N_EXPERTS].set(w_re[l]).at[:, N_EXPERTS:N_EXPERTS + N_GROUPS].set(w_rg[l])
    wr0 = wr.astype(BF16)
    wr1 = (wr - wr0.astype(F32)).astype(BF16)
    wr3 = jnp.stack([wr0, wr1])
    br = jnp.zeros((1, LANES), F32)
    br = br.at[0, 0:N_EXPERTS].set(b_re[l]).at[0, N_EXPERTS:N_EXPERTS + N_GROUPS].set(b_rg[l])
    g0, b0 = r2(ln0_g), r2(ln0_b)

    tq = 256
    tk_s = 256
    u_p = _later_or_same(tq)
    u_s = _later_or_same(tk_s)
    u_n = _later_or_same(ts)

    xp2d = x_prompt.reshape(bp * tp, D_MODEL)
    xs2d = x_sample.reshape(bs * ts, D_MODEL)
    wkt = w_in[l][:, SBA_W:2 * SBA_W].T.astype(BF16)
    wvt = w_in[l][:, 2 * SBA_W:3 * SBA_W].T.astype(BF16)
    qp, ktbp, vtbp, ktp, vtp, glup, mqp = _in_proj_t(xp2d, g0, b0, w_in_b, wkt, wvt, bp, tp, 512, tq)
    qs, kbs, vbs, ks, vs, glus, mqs = _in_proj(xs2d, g0, b0, w_in_b, 512)

    sba_p = _sba_prompt(qp, ktbp, vtbp, u_p, bp, tp, tq, 4)
    kc = jnp.transpose(cache_sba_k[l], (0, 2, 3, 1)).reshape(bs, SBA_W, past)
    vc = jnp.transpose(cache_sba_v[l], (0, 2, 3, 1)).reshape(bs, SBA_W, past)
    sba_s = _sba_sample(qs, kbs, vbs, kc, vc, u_s, u_n, bs, ts, tk_s, 8)

    conv_w = (w_dw[l], r2(b_dw[l]), r2(lnc_g[l]), r2(lnc_b[l]), w_cpw_b)
    h0_p = jnp.zeros((bp, HALO, CONV_W), F32)
    h0_s = jnp.pad(cache_conv[l], ((0, 0), (HALO - (CONV_K - 1), 0), (0, 0)))
    conv_p = _conv_module(h0_p, glup, *conv_w, bp, tp, 256)
    conv_s = _conv_module(h0_s, glus, *conv_w, bs, ts, ts)

    mkt, mvt = _mem_kv(mem_prompt.reshape(bp * N_MEM, D_MODEL), w_mk[l].T.astype(BF16),
                       w_mv[l].T.astype(BF16), bp)
    mem_p = _mem_attn(mqp, mkt, mvt, bp, tp, 512)
    mem_s = _mem_attn(mqs, jnp.transpose(cache_mem_k[l], (0, 2, 3, 1)).reshape(bs, MEM_W, N_MEM),
                      jnp.transpose(cache_mem_v[l], (0, 2, 3, 1)).reshape(bs, MEM_W, N_MEM), bs, ts, ts)

    post_w = (g0, b0, w_out_b, r2(ln1_g[l]), r2(ln1_b[l]), wr3, br)
    x1, route = _post(xp2d, xs2d, sba_p, sba_s, conv_p, conv_s, mem_p, mem_s, *post_w, 512)
    f = _moe(x1, _moe_plan(route, route.shape[0]), w_eg[l], w_eu[l], w_ed[l])
    g2, b2 = r2(ln2_g[l]), r2(ln2_b[l])
    yp = _combine(x1, f, g2, b2, 0, bp * tp, 512)
    ys = _combine(x1, f, g2, b2, bp * tp, bs * ts, 512)

    hd = (SBA_HEADS, HEAD_DIM)
    glup3 = glup.reshape(bp, tp, CONV_W)
    glus3 = glus.reshape(bs, ts, CONV_W)
    conv_tail_s = jnp.concatenate([cache_conv[l], glus3], axis=1)[:, -(CONV_K - 1):]
    return (
        yp.reshape(bp, tp, D_MODEL),
        ys.reshape(bs, ts, D_MODEL),
        jnp.transpose(ktp.reshape(bp, *hd, tp), (0, 3, 1, 2))[None],
        jnp.transpose(vtp.reshape(bp, *hd, tp), (0, 3, 1, 2))[None],
        glup3[:, -(CONV_K - 1):][None],
        jnp.transpose(mkt.reshape(bp, MEM_HEADS, HEAD_DIM, N_MEM), (0, 3, 1, 2))[None],
        jnp.transpose(mvt.reshape(bp, MEM_HEADS, HEAD_DIM, N_MEM), (0, 3, 1, 2))[None],
        ks.reshape(1, bs, ts, *hd),
        vs.reshape(1, bs, ts, *hd),
        conv_tail_s[None],
    )
```

```python
import functools

import jax
import jax.numpy as jnp
from jax import lax
from jax.experimental import pallas as pl
from jax.experimental.pallas import tpu as pltpu

F32 = jnp.float32
BF16 = jnp.bfloat16

D_MODEL = 1024
HEAD_DIM = 64
SBA_HEADS = 8
SBA_W = SBA_HEADS * HEAD_DIM
CONV_W = 256
CONV_K = 31
MEM_HEADS = 4
MEM_W = MEM_HEADS * HEAD_DIM
N_MEM = 256
N_GROUPS = 4
EXPERTS_PER_GROUP = 8
N_EXPERTS = N_GROUPS * EXPERTS_PER_GROUP
D_EXPERT = 256
DEPTH = 1
DEEPNORM_ALPHA = (2 * DEPTH) ** 0.25
QK_SCALE = HEAD_DIM ** -0.5
LN_EPS = 1e-5
LOG2E = 1.4426950408889634

LANES = 128
SUBLANES = 8
HALO = 32
MOE_TM = 1024
MOE_UNROLL = 8
SP_LINEAR = 100.0
VMEM_LIMIT = 48 * 1024 * 1024


def _cparams(sem):
    return pltpu.CompilerParams(dimension_semantics=sem, vmem_limit_bytes=VMEM_LIMIT)


def _ln(x, g, b):
    mu = jnp.mean(x, axis=-1, keepdims=True)
    xc = x - mu
    var = jnp.mean(xc * xc, axis=-1, keepdims=True)
    return xc * lax.rsqrt(var + LN_EPS) * g + b


def _dot(a, b):
    return jnp.dot(a, b, preferred_element_type=F32)


def _dot_nt(a, b):
    return lax.dot_general(a, b, (((1,), (1,)), ((), ())), preferred_element_type=F32)


def _in_proj_kernel(x_ref, g_ref, b_ref, w_ref, q_ref, kb_ref, vb_ref, k_ref, v_ref, glu_ref, mq_ref):
    xn = _ln(x_ref[...], g_ref[...], b_ref[...]).astype(BF16)
    q = _dot(xn, w_ref[:, 0:SBA_W])
    q_ref[...] = (q * (QK_SCALE * LOG2E)).astype(BF16)
    k = _dot(xn, w_ref[:, SBA_W:2 * SBA_W])
    k_ref[...] = k
    kb_ref[...] = k.astype(BF16)
    v = _dot(xn, w_ref[:, 2 * SBA_W:3 * SBA_W])
    v_ref[...] = v
    vb_ref[...] = v.astype(BF16)
    c0 = 3 * SBA_W
    cv = _dot(xn, w_ref[:, c0:c0 + CONV_W])
    cg = _dot(xn, w_ref[:, c0 + CONV_W:c0 + 2 * CONV_W])
    glu_ref[...] = cv * jax.nn.sigmoid(cg)
    mq = _dot(xn, w_ref[:, c0 + 2 * CONV_W:c0 + 2 * CONV_W + MEM_W])
    mq_ref[...] = (mq * QK_SCALE).astype(BF16)


def _in_proj_t_kernel(x_ref, g_ref, b_ref, w_ref, wkt_ref, wvt_ref,
                      q_ref, ktb_ref, vtb_ref, kt_ref, vt_ref, glu_ref, mq_ref, *, tq):
    xn = _ln(x_ref[...], g_ref[...], b_ref[...]).astype(BF16)
    tm = xn.shape[0]
    q = _dot(xn, w_ref[:, 0:SBA_W])
    q_ref[...] = (q * (QK_SCALE * LOG2E)).astype(BF16)
    for wt_ref, t_ref, tb_ref in ((wkt_ref, kt_ref, ktb_ref), (wvt_ref, vt_ref, vtb_ref)):
        t = _dot_nt(wt_ref[...], xn)
        t_ref[0] = t
        for i in range(tm // tq):
            tb_ref[0, i] = t[:, i * tq:(i + 1) * tq].astype(BF16)
    c0 = 3 * SBA_W
    cv = _dot(xn, w_ref[:, c0:c0 + CONV_W])
    cg = _dot(xn, w_ref[:, c0 + CONV_W:c0 + 2 * CONV_W])
    glu_ref[...] = cv * jax.nn.sigmoid(cg)
    mq = _dot(xn, w_ref[:, c0 + 2 * CONV_W:c0 + 2 * CONV_W + MEM_W])
    mq_ref[...] = (mq * QK_SCALE).astype(BF16)


def _in_proj_t(x2d, g, b, w_bf16, wkt, wvt, batch, seq, tm, tq):
    n = x2d.shape[0]
    in_w = w_bf16.shape[1]
    nt = seq // tm
    row = lambda w: pl.BlockSpec((tm, w), lambda bi, i: (bi * nt + i, 0))
    full = lambda s: pl.BlockSpec(s, lambda bi, i: (0, 0))
    tr = pl.BlockSpec((1, SBA_W, tm), lambda bi, i: (bi, 0, i))
    trb = pl.BlockSpec((1, tm // tq, SBA_W, tq), lambda bi, i: (bi, i, 0, 0))
    return pl.pallas_call(
        functools.partial(_in_proj_t_kernel, tq=tq),
        grid=(batch, nt),
        in_specs=[row(D_MODEL), full((1, D_MODEL)), full((1, D_MODEL)), full((D_MODEL, in_w)),
                  full((SBA_W, D_MODEL)), full((SBA_W, D_MODEL))],
        out_specs=[row(SBA_W), trb, trb, tr, tr, row(CONV_W), row(MEM_W)],
        out_shape=[
            jax.ShapeDtypeStruct((n, SBA_W), BF16),
            jax.ShapeDtypeStruct((batch, seq // tq, SBA_W, tq), BF16),
            jax.ShapeDtypeStruct((batch, seq // tq, SBA_W, tq), BF16),
            jax.ShapeDtypeStruct((batch, SBA_W, seq), F32),
            jax.ShapeDtypeStruct((batch, SBA_W, seq), F32),
            jax.ShapeDtypeStruct((n, CONV_W), F32),
            jax.ShapeDtypeStruct((n, MEM_W), BF16),
        ],
        compiler_params=_cparams(("parallel", "parallel")),
        name="in_proj_t",
    )(x2d, g, b, w_bf16, wkt, wvt)


def _in_proj(x2d, g, b, w_bf16, tm):
    n = x2d.shape[0]
    in_w = w_bf16.shape[1]
    row = lambda w: pl.BlockSpec((tm, w), lambda i: (i, 0))
    full = lambda s: pl.BlockSpec(s, lambda i: (0, 0))
    return pl.pallas_call(
        _in_proj_kernel,
        grid=(n // tm,),
        in_specs=[row(D_MODEL), full((1, D_MODEL)), full((1, D_MODEL)), full((D_MODEL, in_w))],
        out_specs=[row(SBA_W), row(SBA_W), row(SBA_W), row(SBA_W), row(SBA_W), row(CONV_W), row(MEM_W)],
        out_shape=[
            jax.ShapeDtypeStruct((n, SBA_W), BF16),
            jax.ShapeDtypeStruct((n, SBA_W), BF16),
            jax.ShapeDtypeStruct((n, SBA_W), BF16),
            jax.ShapeDtypeStruct((n, SBA_W), F32),
            jax.ShapeDtypeStruct((n, SBA_W), F32),
            jax.ShapeDtypeStruct((n, CONV_W), F32),
            jax.ShapeDtypeStruct((n, MEM_W), BF16),
        ],
        compiler_params=_cparams(("parallel",)),
        name="in_proj",
    )(x2d, g, b, w_bf16)


def _head_masks(shape):
    lane = lax.broadcasted_iota(jnp.int32, shape, 1)
    return lane < HEAD_DIM


def _stack_heads(q, low):
    zero = jnp.zeros_like(q)
    return jnp.concatenate([jnp.where(low, q, zero), jnp.where(low, zero, q)], axis=0)


def _sba_tile(z, log2_weight, mask):
    sp = jnp.where(z > SP_LINEAR, z, jnp.log(1.0 + jnp.exp2(z)) * LOG2E)
    if mask is not None:
        sp = jnp.where(mask, sp, 0.0)
    w = jnp.exp2(log2_weight(sp.astype(BF16)))
    if mask is not None:
        w = jnp.where(mask, w, 0.0)
    return w.astype(BF16), jnp.sum(sp, axis=1, keepdims=True)


def _sba_prompt_kernel(q_ref, k_ref, v_ref, u_ref, o_ref, *, tq, hp):
    qi = pl.program_id(2)
    low = _head_masks((tq, LANES))
    u = u_ref[...]
    row = lax.broadcasted_iota(jnp.int32, (2 * tq, tq), 0) & (tq - 1)
    col = lax.broadcasted_iota(jnp.int32, (2 * tq, tq), 1)
    qs = [_stack_heads(q_ref[:, p * LANES:(p + 1) * LANES], low) for p in range(hp)]

    def blk(j, carry, mask):
        out = []
        for p in range(hp):
            c, acc = carry[p]
            z = _dot(qs[p], k_ref[0, j, p * LANES:(p + 1) * LANES, :])
            w, rs = _sba_tile(z, lambda sp: z - _dot(sp, u), mask)
            acc = acc + _dot_nt(w, v_ref[0, j, p * LANES:(p + 1) * LANES, :]) * jnp.exp2(-c)
            out.append((c + rs, acc))
        return tuple(out)

    carry = tuple((jnp.zeros((2 * tq, 1), F32), jnp.zeros((2 * tq, LANES), F32)) for _ in range(hp))
    carry = blk(qi, carry, col < row)
    carry = lax.fori_loop(0, qi // 2,
                          lambda t, cr: blk(qi - 2 - 2 * t, blk(qi - 1 - 2 * t, cr, None), None), carry)
    carry = lax.cond(qi % 2 == 1, lambda cr: blk(0, cr, None), lambda cr: cr, carry)
    for p in range(hp):
        acc = carry[p][1]
        o_ref[:, p * LANES:(p + 1) * LANES] = jnp.where(low, acc[0:tq], acc[tq:2 * tq]).astype(o_ref.dtype)


def _sba_prompt(q, kt, vt, u, batch, seq, tq, hp):
    n = q.shape[0]
    nq = seq // tq
    groups = SBA_W // (LANES * hp)
    kv = pl.BlockSpec((1, nq, hp * LANES, tq), lambda b, p, i: (b, 0, p, 0))
    return pl.pallas_call(
        functools.partial(_sba_prompt_kernel, tq=tq, hp=hp),
        grid=(batch, groups, nq),
        in_specs=[
            pl.BlockSpec((tq, hp * LANES), lambda b, p, i: (b * nq + i, p)),
            kv, kv,
            pl.BlockSpec((tq, tq), lambda b, p, i: (0, 0)),
        ],
        out_specs=pl.BlockSpec((tq, hp * LANES), lambda b, p, i: (b * nq + i, p)),
        out_shape=jax.ShapeDtypeStruct((n, SBA_W), BF16),
        compiler_params=_cparams(("parallel", "parallel", "arbitrary")),
        name="sba_prompt",
    )(q, kt, vt, u)


def _sba_sample_kernel(q_ref, kn_ref, vn_ref, kc_ref, vc_ref, u_ref, un_ref, o_ref, qs_ref, c_ref, acc_ref,
                       *, tn, tk, nsub):
    j = pl.program_id(1)
    nj = pl.num_programs(1)
    low = _head_masks((tn, LANES))
    pairs = SBA_W // LANES
    rows = 2 * tn

    def step(logits, weighted, u, mask):
        z = jnp.concatenate([logits(qs_ref[p], p) for p in range(pairs)], axis=0)
        w, rs = _sba_tile(z, lambda sp: z - _dot(sp, u), mask)
        pv = jnp.concatenate([weighted(w[p * rows:(p + 1) * rows], p) for p in range(pairs)], axis=0)
        c = c_ref[...]
        acc_ref[...] += pv * jnp.exp2(-c)
        c_ref[...] = c + rs

    def lanes(p):
        return slice(p * LANES, (p + 1) * LANES)

    @pl.when(j == 0)
    def _():
        for p in range(pairs):
            qs_ref[p] = _stack_heads(q_ref[:, lanes(p)], low)
        c_ref[...] = jnp.zeros_like(c_ref)
        acc_ref[...] = jnp.zeros_like(acc_ref)
        row = lax.broadcasted_iota(jnp.int32, (pairs * rows, tn), 0) & (tn - 1)
        col = lax.broadcasted_iota(jnp.int32, (pairs * rows, tn), 1)
        step(lambda q, p: _dot_nt(q, kn_ref[:, lanes(p)]), lambda w, p: _dot(w, vn_ref[:, lanes(p)]),
             un_ref[...], col < row)

    for s in reversed(range(nsub)):
        keys = slice(s * tk, (s + 1) * tk)
        step(lambda q, p: _dot(q, kc_ref[0, lanes(p), keys].astype(BF16)),
             lambda w, p: _dot_nt(w, vc_ref[0, lanes(p), keys].astype(BF16)),
             u_ref[...], None)

    @pl.when(j == nj - 1)
    def _():
        for p in range(pairs):
            a0 = acc_ref[p * rows:p * rows + tn, :]
            a1 = acc_ref[p * rows + tn:(p + 1) * rows, :]
            o_ref[:, lanes(p)] = jnp.where(low, a0, a1).astype(o_ref.dtype)


def _sba_sample(q, kn, vn, kc, vc, u, un, batch, tn, tk, nsub):
    n = q.shape[0]
    past = kc.shape[2]
    tkb = tk * nsub
    nj = past // tkb
    rows = SBA_HEADS * tn
    cache = pl.BlockSpec((1, SBA_W, tkb), lambda b, j: (b, 0, nj - 1 - j))
    return pl.pallas_call(
        functools.partial(_sba_sample_kernel, tn=tn, tk=tk, nsub=nsub),
        grid=(batch, nj),
        in_specs=[
            pl.BlockSpec((tn, SBA_W), lambda b, j: (b, 0)),
            pl.BlockSpec((tn, SBA_W), lambda b, j: (b, 0)),
            pl.BlockSpec((tn, SBA_W), lambda b, j: (b, 0)),
            cache, cache,
            pl.BlockSpec((tk, tk), lambda b, j: (0, 0)),
            pl.BlockSpec((tn, tn), lambda b, j: (0, 0)),
        ],
        out_specs=pl.BlockSpec((tn, SBA_W), lambda b, j: (b, 0)),
        out_shape=jax.ShapeDtypeStruct((n, SBA_W), BF16),
        scratch_shapes=[pltpu.VMEM((SBA_W // LANES, 2 * tn, LANES), BF16),
                        pltpu.VMEM((rows, 1), F32), pltpu.VMEM((rows, LANES), F32)],
        compiler_params=_cparams(("parallel", "arbitrary")),
        name="sba_sample",
    )(q, kn, vn, kc, vc, u, un)


def _conv_kernel(h0_ref, glu_ref, wdw_ref, bdw_ref, g_ref, b_ref, wpw_ref, o_ref, buf_ref, *, tt, rc):
    ti = pl.program_id(1)
    rows = HALO + tt

    @pl.when(ti == 0)
    def _():
        buf_ref[0, 0:HALO, :] = h0_ref[0]

    @pl.when(ti > 0)
    def _():
        buf_ref[0, 0:HALO, :] = buf_ref[0, tt:tt + HALO, :]

    buf_ref[0, HALO:rows, :] = glu_ref[...]
    buf_ref[0, rows:rows + SUBLANES, :] = jnp.zeros((SUBLANES, CONV_W), F32)
    for s in range(1, SUBLANES):
        buf_ref[s, 0:rows, :] = buf_ref[0, pl.ds(s, rows), :]
    base = HALO - (CONV_K - 1)
    wdw = wdw_ref[...]
    for r0 in range(0, tt, rc):
        acc = jnp.zeros((rc, CONV_W), F32)
        for kk in range(CONV_K):
            off = base + r0 + kk
            acc = acc + buf_ref[off % SUBLANES, pl.ds(off - off % SUBLANES, rc), :] * wdw[kk:kk + 1, :]
        u = _ln(acc + bdw_ref[...], g_ref[...], b_ref[...])
        s = u * jax.nn.sigmoid(u)
        o_ref[pl.ds(r0, rc), :] = _dot(s.astype(BF16), wpw_ref[...]).astype(o_ref.dtype)


def _conv_module(h0, glu, w_dw, b_dw, g, b, w_pw_bf16, batch, seq, tt):
    n = glu.shape[0]
    nt = seq // tt
    rc = min(tt, 64)
    full = lambda s: pl.BlockSpec(s, lambda bi, ti: (0,) * len(s))
    return pl.pallas_call(
        functools.partial(_conv_kernel, tt=tt, rc=rc),
        grid=(batch, nt),
        in_specs=[
            pl.BlockSpec((1, HALO, CONV_W), lambda bi, ti: (bi, 0, 0)),
            pl.BlockSpec((tt, CONV_W), lambda bi, ti: (bi * nt + ti, 0)),
            full((CONV_K, CONV_W)), full((1, CONV_W)), full((1, CONV_W)), full((1, CONV_W)),
            full((CONV_W, CONV_W)),
        ],
        out_specs=pl.BlockSpec((tt, CONV_W), lambda bi, ti: (bi * nt + ti, 0)),
        out_shape=jax.ShapeDtypeStruct((n, CONV_W), BF16),
        scratch_shapes=[pltpu.VMEM((SUBLANES, HALO + tt + SUBLANES, CONV_W), F32)],
        compiler_params=_cparams(("parallel", "arbitrary")),
        name="conv_module",
    )(h0, glu, w_dw, b_dw, g, b, w_pw_bf16)


def _mem_kv_kernel(m_ref, wkt_ref, wvt_ref, k_ref, v_ref):
    m = m_ref[...].astype(BF16)
    k_ref[0] = _dot_nt(wkt_ref[...], m)
    v_ref[0] = _dot_nt(wvt_ref[...], m)


def _mem_kv(mem2d, wkt, wvt, batch):
    full = pl.BlockSpec((MEM_W, D_MODEL), lambda i: (0, 0))
    out = pl.BlockSpec((1, MEM_W, N_MEM), lambda i: (i, 0, 0))
    return pl.pallas_call(
        _mem_kv_kernel,
        grid=(batch,),
        in_specs=[pl.BlockSpec((N_MEM, D_MODEL), lambda i: (i, 0)), full, full],
        out_specs=[out, out],
        out_shape=[jax.ShapeDtypeStruct((batch, MEM_W, N_MEM), F32)] * 2,
        compiler_params=_cparams(("parallel",)),
        name="mem_kv",
    )(mem2d, wkt, wvt)


def _mem_attn_kernel(q_ref, k_ref, v_ref, o_ref):
    q = q_ref[...]
    tq = q.shape[0]
    low = _head_masks((tq, LANES))
    for p in range(MEM_W // LANES):
        qs = _stack_heads(q[:, p * LANES:(p + 1) * LANES], low)
        kt = k_ref[0, p * LANES:(p + 1) * LANES, :].astype(BF16)
        vt = v_ref[0, p * LANES:(p + 1) * LANES, :].astype(BF16)
        s = _dot(qs, kt)
        e = jnp.exp(s - jnp.max(s, axis=-1, keepdims=True))
        o = _dot_nt(e.astype(BF16), vt) / jnp.sum(e, axis=-1, keepdims=True)
        o_ref[:, p * LANES:(p + 1) * LANES] = jnp.where(low, o[0:tq], o[tq:2 * tq]).astype(o_ref.dtype)


def _mem_attn(mq, mkt, mvt, batch, seq, tq):
    n = mq.shape[0]
    nq = seq // tq
    kv = pl.BlockSpec((1, MEM_W, N_MEM), lambda b, i: (b, 0, 0))
    return pl.pallas_call(
        _mem_attn_kernel,
        grid=(batch, nq),
        in_specs=[pl.BlockSpec((tq, MEM_W), lambda b, i: (b * nq + i, 0)), kv, kv],
        out_specs=pl.BlockSpec((tq, MEM_W), lambda b, i: (b * nq + i, 0)),
        out_shape=jax.ShapeDtypeStruct((n, MEM_W), BF16),
        compiler_params=_cparams(("parallel", "parallel")),
        name="mem_attn",
    )(mq, mkt, mvt)


def _store_token_tiles(ref, x):
    rows = x.shape[0]
    for c in range(SUBLANES):
        ref[pl.ds(c, rows, stride=SUBLANES), :] = x[:, c * LANES:(c + 1) * LANES]


def _load_token_tiles(ref):
    rows = ref.shape[0] // SUBLANES
    return jnp.concatenate([ref[pl.ds(c, rows, stride=SUBLANES), :] for c in range(SUBLANES)], axis=1)


def _split2(x):
    a = x.astype(BF16)
    b = (x - a.astype(F32)).astype(BF16)
    return a, b


def _post_body(x_ref, sba_ref, conv_ref, mem_ref, g0_ref, b0_ref, wo_ref, g1_ref, b1_ref,
               wr_ref, br_ref, x1_ref, route_ref):
    xn = _ln(x_ref[...], g0_ref[...], b0_ref[...])
    mix = _dot(sba_ref[...], wo_ref[0:SBA_W, :])
    mix = mix + _dot(conv_ref[...], wo_ref[SBA_W:SBA_W + CONV_W, :])
    mix = mix + _dot(mem_ref[...], wo_ref[SBA_W + CONV_W:SBA_W + CONV_W + MEM_W, :])
    x1 = _ln(DEEPNORM_ALPHA * xn + mix, g1_ref[...], b1_ref[...])
    _store_token_tiles(x1_ref, x1)

    a0, a1 = _split2(x1)
    hi = _dot(a0, wr_ref[...])
    logits = hi[:, 0:LANES] + (hi[:, LANES:2 * LANES] + _dot(a1, wr_ref[:, 0:LANES])) + br_ref[...]
    tm = logits.shape[0]
    lane = lax.broadcasted_iota(jnp.int32, (tm, LANES), 1).astype(F32)
    neg = jnp.float32(-jnp.inf)
    big = jnp.float32(LANES)
    is_g = jnp.logical_and(lane >= N_EXPERTS, lane < N_EXPERTS + N_GROUPS)
    gl = jnp.where(is_g, logits, neg)
    gmax = jnp.max(gl, axis=-1, keepdims=True)
    g_idx = jnp.min(jnp.where(gl == gmax, lane, big), axis=-1, keepdims=True) - N_EXPERTS
    g_w = 1.0 / jnp.sum(jnp.exp(gl - gmax), axis=-1, keepdims=True)
    in_grp = jnp.logical_and(lane >= g_idx * EXPERTS_PER_GROUP, lane < (g_idx + 1.0) * EXPERTS_PER_GROUP)
    el = jnp.where(in_grp, logits, neg)
    v1 = jnp.max(el, axis=-1, keepdims=True)
    i1 = jnp.min(jnp.where(el == v1, lane, big), axis=-1, keepdims=True)
    el2 = jnp.where(lane == i1, neg, el)
    v2 = jnp.max(el2, axis=-1, keepdims=True)
    i2 = jnp.min(jnp.where(el2 == v2, lane, big), axis=-1, keepdims=True)
    e2 = jnp.exp(v2 - v1)
    p1 = 1.0 / (1.0 + e2)
    p2 = e2 / (1.0 + e2)
    route_ref[...] = jnp.where(lane == 0.0, i1, jnp.where(lane == 1.0, i2, jnp.where(
        lane == 2.0, p1 * g_w, jnp.where(lane == 3.0, p2 * g_w, 0.0))))


def _post_kernel(xp_ref, xs_ref, sbap_ref, sbas_ref, convp_ref, convs_ref, memp_ref, mems_ref,
                 g0_ref, b0_ref, wo_ref, g1_ref, b1_ref, wr_ref, br_ref, x1_ref, route_ref, *, prompt_tiles):
    i = pl.program_id(0)
    shared = (g0_ref, b0_ref, wo_ref, g1_ref, b1_ref, wr_ref, br_ref, x1_ref, route_ref)

    @pl.when(i < prompt_tiles)
    def _():
        _post_body(xp_ref, sbap_ref, convp_ref, memp_ref, *shared)

    @pl.when(i >= prompt_tiles)
    def _():
        _post_body(xs_ref, sbas_ref, convs_ref, mems_ref, *shared)


def _post(xp, xs, sbap, sbas, convp, convs, memp, mems, g0, b0, wo_bf16, g1, b1, wr3, br, tm):
    pt = xp.shape[0] // tm
    st = xs.shape[0] // tm
    n = xp.shape[0] + xs.shape[0]
    prow = lambda w: pl.BlockSpec((tm, w), lambda i: (jnp.minimum(i, pt - 1), 0))
    srow = lambda w: pl.BlockSpec((tm, w), lambda i: (jnp.maximum(i - pt, 0), 0))
    full = lambda s: pl.BlockSpec(s, lambda i: (0,) * len(s))
    widths = (D_MODEL, SBA_W, CONV_W, MEM_W)
    return pl.pallas_call(
        functools.partial(_post_kernel, prompt_tiles=pt),
        grid=(pt + st,),
        in_specs=[spec(w) for w in widths for spec in (prow, srow)] + [
            full((1, D_MODEL)), full((1, D_MODEL)), full((D_MODEL, D_MODEL)),
            full((1, D_MODEL)), full((1, D_MODEL)),
            full((D_MODEL, 2 * LANES)), full((1, LANES))],
        out_specs=[pl.BlockSpec((tm * SUBLANES, LANES), lambda i: (i, 0)),
                   pl.BlockSpec((tm, LANES), lambda i: (i, 0))],
        out_shape=[jax.ShapeDtypeStruct((n * SUBLANES, LANES), F32),
                   jax.ShapeDtypeStruct((n, LANES), F32)],
        compiler_params=_cparams(("arbitrary",)),
        name="post",
    )(xp, xs, sbap, sbas, convp, convs, memp, mems, g0, b0, wo_bf16, g1, b1, wr3, br)


def _moe_plan(route, n):
    tiles = n // MOE_TM + N_GROUPS
    e1 = route[:, 0].astype(jnp.int32)
    e2 = route[:, 1].astype(jnp.int32)
    g = e1 // EXPERTS_PER_GROUP
    _, tok_s, l1_s, l2_s, w1_s, w2_s = lax.sort(
        (g, jnp.arange(n, dtype=jnp.int32), e1 % EXPERTS_PER_GROUP, e2 % EXPERTS_PER_GROUP,
         route[:, 2], route[:, 3]), num_keys=1, is_stable=True)
    counts = jnp.sum((g[:, None] == jnp.arange(N_GROUPS, dtype=jnp.int32)[None, :]).astype(jnp.int32), axis=0)
    ntile = (counts + MOE_TM - 1) // MOE_TM
    tile_end = jnp.cumsum(ntile)
    tile_start = tile_end - ntile
    first = jnp.cumsum(counts) - counts
    n_active = tile_end[-1]
    t = jnp.arange(tiles, dtype=jnp.int32)
    tg = jnp.minimum(jnp.sum((t[:, None] >= tile_end[None, :]).astype(jnp.int32), axis=1), N_GROUPS - 1)
    done = (t - tile_start[tg]) * MOE_TM
    n_valid = jnp.where(t < n_active, jnp.clip(counts[tg] - done, 0, MOE_TM), 0)
    r = jnp.arange(MOE_TM, dtype=jnp.int32)
    valid = r[None, :] < n_valid[:, None]
    shift = tile_start * MOE_TM - first
    max_shift = N_GROUPS * MOE_TM
    row_group = jnp.repeat(tg, MOE_TM)

    def runs(a):
        ext = jnp.concatenate([jnp.zeros((max_shift,), a.dtype), a, jnp.zeros((tiles * MOE_TM - n,), a.dtype)])
        out = jnp.zeros((tiles * MOE_TM,), a.dtype)
        for gi in range(N_GROUPS):
            moved = lax.dynamic_slice(ext, (max_shift - shift[gi],), (tiles * MOE_TM,))
            out = jnp.where(row_group == gi, moved, out)
        return out.reshape(tiles, MOE_TM)

    tok = runs(tok_s)
    src = jnp.where(valid, tok, 0)
    dst = jnp.where(valid, tok, n + r[None, :])
    ids = jnp.arange(EXPERTS_PER_GROUP, dtype=jnp.int32)
    gate = (jnp.where(runs(l1_s)[..., None] == ids, runs(w1_s)[..., None], 0.0)
            + jnp.where(runs(l2_s)[..., None] == ids, runs(w2_s)[..., None], 0.0))
    gate = jnp.where(valid[..., None], gate, 0.0).reshape(tiles * MOE_TM, EXPERTS_PER_GROUP)
    as_rows = lambda a: (a * SUBLANES).astype(jnp.int32).reshape(tiles, 1, MOE_TM)
    return tg.astype(jnp.int32), n_active.reshape(1).astype(jnp.int32), as_rows(src), as_rows(dst), gate


def _moe_kernel(tg_ref, na_ref, src0_ref, src1_ref, dst_ref, gate_ref, x_hbm, wg_ref, wu_ref, wd_ref,
                y_hbm, xg_ref, xb_ref, acc_ref, yo_ref, gsem, ssem):
    t = pl.program_id(0)
    j = pl.program_id(1)
    n_active = na_ref[0]
    tile_rows = MOE_TM * SUBLANES

    def token(ref, row0):
        return ref.at[pl.ds(pl.multiple_of(row0, SUBLANES), SUBLANES)]

    def start_gather(src_ref, slot):
        def body(i, carry):
            for k in range(MOE_UNROLL):
                r = i * MOE_UNROLL + k
                pltpu.make_async_copy(token(x_hbm, src_ref[0, 0, r]), token(xg_ref.at[slot], r * SUBLANES),
                                      gsem.at[slot]).start(priority=k % 2)
            return carry
        lax.fori_loop(0, MOE_TM // MOE_UNROLL, body, 0)

    def wait_gather(slot):
        pltpu.make_async_copy(x_hbm.at[pl.ds(0, tile_rows)], xg_ref.at[slot], gsem.at[slot]).wait()

    def wait_scatter():
        pltpu.make_async_copy(yo_ref, y_hbm.at[pl.ds(0, tile_rows)], ssem.at[0]).wait()

    @pl.when(t < n_active)
    def _():
        slot = lax.rem(t, 2)

        @pl.when(j == 0)
        def _():
            @pl.when(t == 0)
            def _():
                start_gather(src0_ref, 0)
                yo_ref[...] = jnp.zeros_like(yo_ref)
                sink = pltpu.make_async_copy(yo_ref, y_hbm.at[pl.ds(y_hbm.shape[0] - tile_rows, tile_rows)],
                                             ssem.at[0])
                sink.start()
                sink.wait()

            @pl.when(t + 1 < n_active)
            def _():
                start_gather(src1_ref, 1 - slot)

            wait_gather(slot)
            xb_ref[...] = _load_token_tiles(xg_ref.at[slot]).astype(BF16)
            acc_ref[...] = jnp.zeros_like(acc_ref)

        x = xb_ref[...]
        a = _dot(x, wg_ref[0].astype(BF16))
        u = _dot(x, wu_ref[0].astype(BF16))
        gate = gate_ref[...]
        lane = lax.broadcasted_iota(jnp.int32, gate.shape, 1)
        ge = jnp.sum(jnp.where(lane == j, gate, 0.0), axis=1, keepdims=True)
        hid = (a * jax.nn.sigmoid(a)) * u * ge
        acc_ref[...] += _dot(hid.astype(BF16), wd_ref[0].astype(BF16))

        @pl.when(j == EXPERTS_PER_GROUP - 1)
        def _():
            @pl.when(t > 0)
            def _():
                wait_scatter()

            _store_token_tiles(yo_ref, acc_ref[...])

            def body(i, carry):
                for k in range(MOE_UNROLL):
                    r = i * MOE_UNROLL + k
                    pltpu.make_async_copy(token(yo_ref, r * SUBLANES), token(y_hbm, dst_ref[0, 0, r]),
                                          ssem.at[0]).start(priority=k % 2)
                return carry
            lax.fori_loop(0, MOE_TM // MOE_UNROLL, body, 0)

            @pl.when(t == n_active - 1)
            def _():
                wait_scatter()


def _moe(x1, plan, w_eg, w_eu, w_ed):
    tg, n_active, src, dst, gate = plan
    tiles = src.shape[0]
    tile_rows = MOE_TM * SUBLANES
    idx = lambda f: pl.BlockSpec((1, 1, MOE_TM), f, memory_space=pltpu.SMEM)
    expert = lambda a, b: pl.BlockSpec((1, a, b), lambda t, j, tg, na: (tg[t] * EXPERTS_PER_GROUP + j, 0, 0))
    grid_spec = pltpu.PrefetchScalarGridSpec(
        num_scalar_prefetch=2,
        grid=(tiles, EXPERTS_PER_GROUP),
        in_specs=[
            idx(lambda t, j, tg, na: (t, 0, 0)),
            idx(lambda t, j, tg, na: (jnp.minimum(t + 1, tiles - 1), 0, 0)),
            idx(lambda t, j, tg, na: (t, 0, 0)),
            pl.BlockSpec((MOE_TM, EXPERTS_PER_GROUP), lambda t, j, tg, na: (t, 0)),
            pl.BlockSpec(memory_space=pl.ANY),
            expert(D_MODEL, D_EXPERT), expert(D_MODEL, D_EXPERT), expert(D_EXPERT, D_MODEL),
        ],
        out_specs=pl.BlockSpec(memory_space=pl.ANY),
        scratch_shapes=[pltpu.VMEM((2, tile_rows, LANES), F32), pltpu.VMEM((MOE_TM, D_MODEL), BF16),
                        pltpu.VMEM((MOE_TM, D_MODEL), F32), pltpu.VMEM((tile_rows, LANES), F32),
                        pltpu.SemaphoreType.DMA((2,)), pltpu.SemaphoreType.DMA((1,))],
    )
    return pl.pallas_call(
        _moe_kernel,
        grid_spec=grid_spec,
        out_shape=jax.ShapeDtypeStruct((x1.shape[0] + tile_rows, LANES), F32),
        compiler_params=_cparams(("arbitrary", "arbitrary")),
        name="moe",
    )(tg, n_active, src, src, dst, gate, x1, w_eg, w_eu, w_ed)


def _combine_kernel(x1_ref, y_ref, g_ref, b_ref, o_ref):
    o_ref[...] = _ln(DEEPNORM_ALPHA * _load_token_tiles(x1_ref) + _load_token_tiles(y_ref), g_ref[...], b_ref[...])


def _combine(x1, y, g2, b2, row0, rows, tm):
    b0 = row0 // tm
    tiles = pl.BlockSpec((tm * SUBLANES, LANES), lambda i: (i + b0, 0))
    full = pl.BlockSpec((1, D_MODEL), lambda i: (0, 0))
    return pl.pallas_call(
        _combine_kernel,
        grid=(rows // tm,),
        in_specs=[tiles, tiles, full, full],
        out_specs=pl.BlockSpec((tm, D_MODEL), lambda i: (i, 0)),
        out_shape=jax.ShapeDtypeStruct((rows, D_MODEL), F32),
        compiler_params=_cparams(("parallel",)),
        name="combine",
    )(x1, y, g2, b2)


def _later_or_same(n):
    r = lax.broadcasted_iota(jnp.int32, (n, n), 0)
    c = lax.broadcasted_iota(jnp.int32, (n, n), 1)
    return (r >= c).astype(BF16)


def kernel(x_prompt, x_sample, mem_prompt, cache_sba_k, cache_sba_v, cache_conv, cache_mem_k, cache_mem_v,
           ln0_g, ln0_b, w_in, w_dw, b_dw, lnc_g, lnc_b, w_cpw, w_mk, w_mv, w_out, ln1_g, ln1_b,
           w_rg, b_rg, w_re, b_re, w_eg, w_eu, w_ed, ln2_g, ln2_b):
    bp, tp, _ = x_prompt.shape
    bs, ts, _ = x_sample.shape
    past = cache_sba_k.shape[2]
    l = 0
    r2 = lambda a: a.reshape(1, -1)

    w_in_b = w_in[l].astype(BF16)
    w_out_b = w_out[l].astype(BF16)
    w_cpw_b = w_cpw[l].astype(BF16)
    wr = jnp.zeros((D_MODEL, LANES), F32)
    wr = wr.at[:, 0:N_EXPERTS].set(w_re[l]).at[:, N_EXPERTS:N_EXPERTS + N_GROUPS].set(w_rg[l])
    wr0 = wr.astype(BF16)
    wr1 = (wr - wr0.astype(F32)).astype(BF16)
    wr3 = jnp.concatenate([wr0, wr1], axis=1)
    br = jnp.zeros((1, LANES), F32)
    br = br.at[0, 0:N_EXPERTS].set(b_re[l]).at[0, N_EXPERTS:N_EXPERTS + N_GROUPS].set(b_rg[l])
    g0, b0 = r2(ln0_g), r2(ln0_b)

    tq = 256
    tk_s = 256
    u_p = _later_or_same(tq)
    u_s = _later_or_same(tk_s)
    u_n = _later_or_same(ts)

    xp2d = x_prompt.reshape(bp * tp, D_MODEL)
    xs2d = x_sample.reshape(bs * ts, D_MODEL)
    wkt = w_in[l][:, SBA_W:2 * SBA_W].T.astype(BF16)
    wvt = w_in[l][:, 2 * SBA_W:3 * SBA_W].T.astype(BF16)
    qp, ktbp, vtbp, ktp, vtp, glup, mqp = _in_proj_t(xp2d, g0, b0, w_in_b, wkt, wvt, bp, tp, 512, tq)
    qs, kbs, vbs, ks, vs, glus, mqs = _in_proj(xs2d, g0, b0, w_in_b, 512)

    sba_p = _sba_prompt(qp, ktbp, vtbp, u_p, bp, tp, tq, 4)
    kc = jnp.transpose(cache_sba_k[l], (0, 2, 3, 1)).reshape(bs, SBA_W, past)
    vc = jnp.transpose(cache_sba_v[l], (0, 2, 3, 1)).reshape(bs, SBA_W, past)
    sba_s = _sba_sample(qs, kbs, vbs, kc, vc, u_s, u_n, bs, ts, tk_s, 8)

    conv_w = (w_dw[l], r2(b_dw[l]), r2(lnc_g[l]), r2(lnc_b[l]), w_cpw_b)
    h0_p = jnp.zeros((bp, HALO, CONV_W), F32)
    h0_s = jnp.pad(cache_conv[l], ((0, 0), (HALO - (CONV_K - 1), 0), (0, 0)))
    conv_p = _conv_module(h0_p, glup, *conv_w, bp, tp, 256)
    conv_s = _conv_module(h0_s, glus, *conv_w, bs, ts, ts)

    mkt, mvt = _mem_kv(mem_prompt.reshape(bp * N_MEM, D_MODEL), w_mk[l].T.astype(BF16),
                       w_mv[l].T.astype(BF16), bp)
    mem_p = _mem_attn(mqp, mkt, mvt, bp, tp, 512)
    mem_s = _mem_attn(mqs, jnp.transpose(cache_mem_k[l], (0, 2, 3, 1)).reshape(bs, MEM_W, N_MEM),
                      jnp.transpose(cache_mem_v[l], (0, 2, 3, 1)).reshape(bs, MEM_W, N_MEM), bs, ts, ts)

    post_w = (g0, b0, w_out_b, r2(ln1_g[l]), r2(ln1_b[l]), wr3, br)
    x1, route = _post(xp2d, xs2d, sba_p, sba_s, conv_p, conv_s, mem_p, mem_s, *post_w, 512)
    f = _moe(x1, _moe_plan(route, route.shape[0]), w_eg[l], w_eu[l], w_ed[l])
    g2, b2 = r2(ln2_g[l]), r2(ln2_b[l])
    yp = _combine(x1, f, g2, b2, 0, bp * tp, 512)
    ys = _combine(x1, f, g2, b2, bp * tp, bs * ts, 512)

    hd = (SBA_HEADS, HEAD_DIM)
    glup3 = glup.reshape(bp, tp, CONV_W)
    glus3 = glus.reshape(bs, ts, CONV_W)
    conv_tail_s = jnp.concatenate([cache_conv[l], glus3], axis=1)[:, -(CONV_K - 1):]
    return (
        yp.reshape(bp, tp, D_MODEL),
        ys.reshape(bs, ts, D_MODEL),
        jnp.transpose(ktp.reshape(bp, *hd, tp), (0, 3, 1, 2))[None],
        jnp.transpose(vtp.reshape(bp, *hd, tp), (0, 3, 1, 2))[None],
        glup3[:, -(CONV_K - 1):][None],
        jnp.transpose(mkt.reshape(bp, MEM_HEADS, HEAD_DIM, N_MEM), (0, 3, 1, 2))[None],
        jnp.transpose(mvt.reshape(bp, MEM_HEADS, HEAD_DIM, N_MEM), (0, 3, 1, 2))[None],
        ks.reshape(1, bs, ts, *hd),
        vs.reshape(1, bs, ts, *hd),
        conv_tail_s[None],
    )
```

```python
import functools

import jax
import jax.numpy as jnp
from jax import lax
from jax.experimental import pallas as pl
from jax.experimental.pallas import tpu as pltpu

F32 = jnp.float32
BF16 = jnp.bfloat16

D_MODEL = 1024
HEAD_DIM = 64
SBA_HEADS = 8
SBA_W = SBA_HEADS * HEAD_DIM
CONV_W = 256
CONV_K = 31
MEM_HEADS = 4
MEM_W = MEM_HEADS * HEAD_DIM
N_MEM = 256
N_GROUPS = 4
EXPERTS_PER_GROUP = 8
N_EXPERTS = N_GROUPS * EXPERTS_PER_GROUP
D_EXPERT = 256
DEPTH = 1
DEEPNORM_ALPHA = (2 * DEPTH) ** 0.25
QK_SCALE = HEAD_DIM ** -0.5
LN_EPS = 1e-5
LOG2E = 1.4426950408889634

LANES = 128
SUBLANES = 8
HALO = 32
MOE_TM = 1024
MOE_UNROLL = 8
SP_LINEAR = 100.0
VMEM_LIMIT = 48 * 1024 * 1024


def _cparams(sem):
    return pltpu.CompilerParams(dimension_semantics=sem, vmem_limit_bytes=VMEM_LIMIT)


def _ln(x, g, b):
    mu = jnp.mean(x, axis=-1, keepdims=True)
    xc = x - mu
    var = jnp.mean(xc * xc, axis=-1, keepdims=True)
    return xc * lax.rsqrt(var + LN_EPS) * g + b


def _dot(a, b):
    return jnp.dot(a, b, preferred_element_type=F32)


def _dot_nt(a, b):
    return lax.dot_general(a, b, (((1,), (1,)), ((), ())), preferred_element_type=F32)


def _in_proj_kernel(x_ref, g_ref, b_ref, w_ref, q_ref, kb_ref, vb_ref, k_ref, v_ref, glu_ref, mq_ref):
    xn = _ln(x_ref[...], g_ref[...], b_ref[...]).astype(BF16)
    q = _dot(xn, w_ref[:, 0:SBA_W])
    q_ref[...] = (q * (QK_SCALE * LOG2E)).astype(BF16)
    k = _dot(xn, w_ref[:, SBA_W:2 * SBA_W])
    k_ref[...] = k
    kb_ref[...] = k.astype(BF16)
    v = _dot(xn, w_ref[:, 2 * SBA_W:3 * SBA_W])
    v_ref[...] = v
    vb_ref[...] = v.astype(BF16)
    c0 = 3 * SBA_W
    cv = _dot(xn, w_ref[:, c0:c0 + CONV_W])
    cg = _dot(xn, w_ref[:, c0 + CONV_W:c0 + 2 * CONV_W])
    glu_ref[...] = cv * jax.nn.sigmoid(cg)
    mq = _dot(xn, w_ref[:, c0 + 2 * CONV_W:c0 + 2 * CONV_W + MEM_W])
    mq_ref[...] = (mq * QK_SCALE).astype(BF16)


def _in_proj_t_kernel(x_ref, g_ref, b_ref, w_ref, wkt_ref, wvt_ref,
                      q_ref, ktb_ref, vtb_ref, kt_ref, vt_ref, glu_ref, mq_ref, *, tq):
    xn = _ln(x_ref[...], g_ref[...], b_ref[...]).astype(BF16)
    tm = xn.shape[0]
    q = _dot(xn, w_ref[:, 0:SBA_W])
    q_ref[...] = (q * (QK_SCALE * LOG2E)).astype(BF16)
    for wt_ref, t_ref, tb_ref in ((wkt_ref, kt_ref, ktb_ref), (wvt_ref, vt_ref, vtb_ref)):
        t = _dot_nt(wt_ref[...], xn)
        t_ref[0] = t
        for i in range(tm // tq):
            tb_ref[0, i] = t[:, i * tq:(i + 1) * tq].astype(BF16)
    c0 = 3 * SBA_W
    cv = _dot(xn, w_ref[:, c0:c0 + CONV_W])
    cg = _dot(xn, w_ref[:, c0 + CONV_W:c0 + 2 * CONV_W])
    glu_ref[...] = cv * jax.nn.sigmoid(cg)
    mq = _dot(xn, w_ref[:, c0 + 2 * CONV_W:c0 + 2 * CONV_W + MEM_W])
    mq_ref[...] = (mq * QK_SCALE).astype(BF16)


def _in_proj_t(x2d, g, b, w_bf16, wkt, wvt, batch, seq, tm, tq):
    n = x2d.shape[0]
    in_w = w_bf16.shape[1]
    nt = seq // tm
    row = lambda w: pl.BlockSpec((tm, w), lambda bi, i: (bi * nt + i, 0))
    full = lambda s: pl.BlockSpec(s, lambda bi, i: (0, 0))
    tr = pl.BlockSpec((1, SBA_W, tm), lambda bi, i: (bi, 0, i))
    trb = pl.BlockSpec((1, tm // tq, SBA_W, tq), lambda bi, i: (bi, i, 0, 0))
    return pl.pallas_call(
        functools.partial(_in_proj_t_kernel, tq=tq),
        grid=(batch, nt),
        in_specs=[row(D_MODEL), full((1, D_MODEL)), full((1, D_MODEL)), full((D_MODEL, in_w)),
                  full((SBA_W, D_MODEL)), full((SBA_W, D_MODEL))],
        out_specs=[row(SBA_W), trb, trb, tr, tr, row(CONV_W), row(MEM_W)],
        out_shape=[
            jax.ShapeDtypeStruct((n, SBA_W), BF16),
            jax.ShapeDtypeStruct((batch, seq // tq, SBA_W, tq), BF16),
            jax.ShapeDtypeStruct((batch, seq // tq, SBA_W, tq), BF16),
            jax.ShapeDtypeStruct((batch, SBA_W, seq), F32),
            jax.ShapeDtypeStruct((batch, SBA_W, seq), F32),
            jax.ShapeDtypeStruct((n, CONV_W), F32),
            jax.ShapeDtypeStruct((n, MEM_W), BF16),
        ],
        compiler_params=_cparams(("parallel", "parallel")),
        name="in_proj_t",
    )(x2d, g, b, w_bf16, wkt, wvt)


def _in_proj(x2d, g, b, w_bf16, tm):
    n = x2d.shape[0]
    in_w = w_bf16.shape[1]
    row = lambda w: pl.BlockSpec((tm, w), lambda i: (i, 0))
    full = lambda s: pl.BlockSpec(s, lambda i: (0, 0))
    return pl.pallas_call(
        _in_proj_kernel,
        grid=(n // tm,),
        in_specs=[row(D_MODEL), full((1, D_MODEL)), full((1, D_MODEL)), full((D_MODEL, in_w))],
        out_specs=[row(SBA_W), row(SBA_W), row(SBA_W), row(SBA_W), row(SBA_W), row(CONV_W), row(MEM_W)],
        out_shape=[
            jax.ShapeDtypeStruct((n, SBA_W), BF16),
            jax.ShapeDtypeStruct((n, SBA_W), BF16),
            jax.ShapeDtypeStruct((n, SBA_W), BF16),
            jax.ShapeDtypeStruct((n, SBA_W), F32),
            jax.ShapeDtypeStruct((n, SBA_W), F32),
            jax.ShapeDtypeStruct((n, CONV_W), F32),
            jax.ShapeDtypeStruct((n, MEM_W), BF16),
        ],
        compiler_params=_cparams(("parallel",)),
        name="in_proj",
    )(x2d, g, b, w_bf16)


def _head_masks(shape):
    lane = lax.broadcasted_iota(jnp.int32, shape, 1)
    return lane < HEAD_DIM


def _stack_heads(q, low):
    zero = jnp.zeros_like(q)
    return jnp.concatenate([jnp.where(low, q, zero), jnp.where(low, zero, q)], axis=0)


def _sba_tile(z, log2_weight, mask):
    sp = jnp.where(z > SP_LINEAR, z, jnp.log(1.0 + jnp.exp2(z)) * LOG2E)
    if mask is not None:
        sp = jnp.where(mask, sp, 0.0)
    w = jnp.exp2(log2_weight(sp.astype(BF16)))
    if mask is not None:
        w = jnp.where(mask, w, 0.0)
    return w.astype(BF16), jnp.sum(sp, axis=1, keepdims=True)


def _sba_prompt_kernel(q_ref, k_ref, v_ref, u_ref, o_ref, *, tq, hp):
    qi = pl.program_id(2)
    low = _head_masks((tq, LANES))
    u = u_ref[...]
    row = lax.broadcasted_iota(jnp.int32, (2 * tq, tq), 0) & (tq - 1)
    col = lax.broadcasted_iota(jnp.int32, (2 * tq, tq), 1)
    qs = [_stack_heads(q_ref[:, p * LANES:(p + 1) * LANES], low) for p in range(hp)]

    def blk(j, carry, mask):
        out = []
        for p in range(hp):
            c, acc = carry[p]
            z = _dot(qs[p], k_ref[0, j, p * LANES:(p + 1) * LANES, :])
            w, rs = _sba_tile(z, lambda sp: (z - c) - _dot(sp, u), mask)
            acc = acc + _dot_nt(w, v_ref[0, j, p * LANES:(p + 1) * LANES, :])
            out.append((c + rs, acc))
        return tuple(out)

    carry = tuple((jnp.zeros((2 * tq, 1), F32), jnp.zeros((2 * tq, LANES), F32)) for _ in range(hp))
    carry = blk(qi, carry, col < row)
    carry = lax.fori_loop(0, qi // 2,
                          lambda t, cr: blk(qi - 2 - 2 * t, blk(qi - 1 - 2 * t, cr, None), None), carry)
    carry = lax.cond(qi % 2 == 1, lambda cr: blk(0, cr, None), lambda cr: cr, carry)
    for p in range(hp):
        acc = carry[p][1]
        o_ref[:, p * LANES:(p + 1) * LANES] = jnp.where(low, acc[0:tq], acc[tq:2 * tq]).astype(o_ref.dtype)


def _sba_prompt(q, kt, vt, u, batch, seq, tq, hp):
    n = q.shape[0]
    nq = seq // tq
    groups = SBA_W // (LANES * hp)
    kv = pl.BlockSpec((1, nq, hp * LANES, tq), lambda b, p, i: (b, 0, p, 0))
    return pl.pallas_call(
        functools.partial(_sba_prompt_kernel, tq=tq, hp=hp),
        grid=(batch, groups, nq),
        in_specs=[
            pl.BlockSpec((tq, hp * LANES), lambda b, p, i: (b * nq + i, p)),
            kv, kv,
            pl.BlockSpec((tq, tq), lambda b, p, i: (0, 0)),
        ],
        out_specs=pl.BlockSpec((tq, hp * LANES), lambda b, p, i: (b * nq + i, p)),
        out_shape=jax.ShapeDtypeStruct((n, SBA_W), BF16),
        compiler_params=_cparams(("parallel", "parallel", "arbitrary")),
        name="sba_prompt",
    )(q, kt, vt, u)


def _sba_sample_kernel(q_ref, kn_ref, vn_ref, kc_ref, vc_ref, u_ref, un_ref, o_ref, qs_ref, c_ref, acc_ref,
                       *, tn, tk, nsub):
    j = pl.program_id(1)
    nj = pl.num_programs(1)
    low = _head_masks((tn, LANES))
    pairs = SBA_W // LANES
    rows = 2 * tn

    def step(logits, weighted, u, mask):
        z = jnp.concatenate([logits(qs_ref[p], p) for p in range(pairs)], axis=0)
        c = c_ref[...]
        w, rs = _sba_tile(z, lambda sp: (z - c) - _dot(sp, u), mask)
        acc_ref[...] += jnp.concatenate([weighted(w[p * rows:(p + 1) * rows], p) for p in range(pairs)], axis=0)
        c_ref[...] = c + rs

    def lanes(p):
        return slice(p * LANES, (p + 1) * LANES)

    @pl.when(j == 0)
    def _():
        for p in range(pairs):
            qs_ref[p] = _stack_heads(q_ref[:, lanes(p)], low)
        c_ref[...] = jnp.zeros_like(c_ref)
        acc_ref[...] = jnp.zeros_like(acc_ref)
        row = lax.broadcasted_iota(jnp.int32, (pairs * rows, tn), 0) & (tn - 1)
        col = lax.broadcasted_iota(jnp.int32, (pairs * rows, tn), 1)
        step(lambda q, p: _dot_nt(q, kn_ref[:, lanes(p)]), lambda w, p: _dot(w, vn_ref[:, lanes(p)]),
             un_ref[...], col < row)

    for s in reversed(range(nsub)):
        keys = slice(s * tk, (s + 1) * tk)
        step(lambda q, p: _dot(q, kc_ref[0, lanes(p), keys].astype(BF16)),
             lambda w, p: _dot_nt(w, vc_ref[0, lanes(p), keys].astype(BF16)),
             u_ref[...], None)

    @pl.when(j == nj - 1)
    def _():
        for p in range(pairs):
            a0 = acc_ref[p * rows:p * rows + tn, :]
            a1 = acc_ref[p * rows + tn:(p + 1) * rows, :]
            o_ref[:, lanes(p)] = jnp.where(low, a0, a1).astype(o_ref.dtype)


def _sba_sample(q, kn, vn, kc, vc, u, un, batch, tn, tk, nsub):
    n = q.shape[0]
    past = kc.shape[2]
    tkb = tk * nsub
    nj = past // tkb
    rows = SBA_HEADS * tn
    cache = pl.BlockSpec((1, SBA_W, tkb), lambda b, j: (b, 0, nj - 1 - j))
    return pl.pallas_call(
        functools.partial(_sba_sample_kernel, tn=tn, tk=tk, nsub=nsub),
        grid=(batch, nj),
        in_specs=[
            pl.BlockSpec((tn, SBA_W), lambda b, j: (b, 0)),
            pl.BlockSpec((tn, SBA_W), lambda b, j: (b, 0)),
            pl.BlockSpec((tn, SBA_W), lambda b, j: (b, 0)),
            cache, cache,
            pl.BlockSpec((tk, tk), lambda b, j: (0, 0)),
            pl.BlockSpec((tn, tn), lambda b, j: (0, 0)),
        ],
        out_specs=pl.BlockSpec((tn, SBA_W), lambda b, j: (b, 0)),
        out_shape=jax.ShapeDtypeStruct((n, SBA_W), BF16),
        scratch_shapes=[pltpu.VMEM((SBA_W // LANES, 2 * tn, LANES), BF16),
                        pltpu.VMEM((rows, 1), F32), pltpu.VMEM((rows, LANES), F32)],
        compiler_params=_cparams(("parallel", "arbitrary")),
        name="sba_sample",
    )(q, kn, vn, kc, vc, u, un)


def _conv_kernel(h0_ref, glu_ref, wdw_ref, bdw_ref, g_ref, b_ref, wpw_ref, o_ref, buf_ref, *, tt, rc):
    ti = pl.program_id(1)
    rows = HALO + tt

    @pl.when(ti == 0)
    def _():
        buf_ref[0, 0:HALO, :] = h0_ref[0]

    @pl.when(ti > 0)
    def _():
        buf_ref[0, 0:HALO, :] = buf_ref[0, tt:tt + HALO, :]

    buf_ref[0, HALO:rows, :] = glu_ref[...]
    buf_ref[0, rows:rows + SUBLANES, :] = jnp.zeros((SUBLANES, CONV_W), F32)
    for s in range(1, SUBLANES):
        buf_ref[s, 0:rows, :] = buf_ref[0, pl.ds(s, rows), :]
    base = HALO - (CONV_K - 1)
    wdw = wdw_ref[...]
    for r0 in range(0, tt, rc):
        acc = jnp.zeros((rc, CONV_W), F32)
        for kk in range(CONV_K):
            off = base + r0 + kk
            acc = acc + buf_ref[off % SUBLANES, pl.ds(off - off % SUBLANES, rc), :] * wdw[kk:kk + 1, :]
        u = _ln(acc + bdw_ref[...], g_ref[...], b_ref[...])
        s = u * jax.nn.sigmoid(u)
        o_ref[pl.ds(r0, rc), :] = _dot(s.astype(BF16), wpw_ref[...]).astype(o_ref.dtype)


def _conv_module(h0, glu, w_dw, b_dw, g, b, w_pw_bf16, batch, seq, tt):
    n = glu.shape[0]
    nt = seq // tt
    rc = min(tt, 64)
    full = lambda s: pl.BlockSpec(s, lambda bi, ti: (0,) * len(s))
    return pl.pallas_call(
        functools.partial(_conv_kernel, tt=tt, rc=rc),
        grid=(batch, nt),
        in_specs=[
            pl.BlockSpec((1, HALO, CONV_W), lambda bi, ti: (bi, 0, 0)),
            pl.BlockSpec((tt, CONV_W), lambda bi, ti: (bi * nt + ti, 0)),
            full((CONV_K, CONV_W)), full((1, CONV_W)), full((1, CONV_W)), full((1, CONV_W)),
            full((CONV_W, CONV_W)),
        ],
        out_specs=pl.BlockSpec((tt, CONV_W), lambda bi, ti: (bi * nt + ti, 0)),
        out_shape=jax.ShapeDtypeStruct((n, CONV_W), BF16),
        scratch_shapes=[pltpu.VMEM((SUBLANES, HALO + tt + SUBLANES, CONV_W), F32)],
        compiler_params=_cparams(("parallel", "arbitrary")),
        name="conv_module",
    )(h0, glu, w_dw, b_dw, g, b, w_pw_bf16)


def _mem_kv_kernel(m_ref, wkt_ref, wvt_ref, k_ref, v_ref):
    m = m_ref[...].astype(BF16)
    k_ref[0] = _dot_nt(wkt_ref[...], m)
    v_ref[0] = _dot_nt(wvt_ref[...], m)


def _mem_kv(mem2d, wkt, wvt, batch):
    full = pl.BlockSpec((MEM_W, D_MODEL), lambda i: (0, 0))
    out = pl.BlockSpec((1, MEM_W, N_MEM), lambda i: (i, 0, 0))
    return pl.pallas_call(
        _mem_kv_kernel,
        grid=(batch,),
        in_specs=[pl.BlockSpec((N_MEM, D_MODEL), lambda i: (i, 0)), full, full],
        out_specs=[out, out],
        out_shape=[jax.ShapeDtypeStruct((batch, MEM_W, N_MEM), F32)] * 2,
        compiler_params=_cparams(("parallel",)),
        name="mem_kv",
    )(mem2d, wkt, wvt)


def _mem_attn_kernel(q_ref, k_ref, v_ref, o_ref):
    q = q_ref[...]
    tq = q.shape[0]
    low = _head_masks((tq, LANES))
    for p in range(MEM_W // LANES):
        qs = _stack_heads(q[:, p * LANES:(p + 1) * LANES], low)
        kt = k_ref[0, p * LANES:(p + 1) * LANES, :].astype(BF16)
        vt = v_ref[0, p * LANES:(p + 1) * LANES, :].astype(BF16)
        s = _dot(qs, kt)
        e = jnp.exp(s - jnp.max(s, axis=-1, keepdims=True))
        o = _dot_nt(e.astype(BF16), vt) / jnp.sum(e, axis=-1, keepdims=True)
        o_ref[:, p * LANES:(p + 1) * LANES] = jnp.where(low, o[0:tq], o[tq:2 * tq]).astype(o_ref.dtype)


def _mem_attn(mq, mkt, mvt, batch, seq, tq):
    n = mq.shape[0]
    nq = seq // tq
    kv = pl.BlockSpec((1, MEM_W, N_MEM), lambda b, i: (b, 0, 0))
    return pl.pallas_call(
        _mem_attn_kernel,
        grid=(batch, nq),
        in_specs=[pl.BlockSpec((tq, MEM_W), lambda b, i: (b * nq + i, 0)), kv, kv],
        out_specs=pl.BlockSpec((tq, MEM_W), lambda b, i: (b * nq + i, 0)),
        out_shape=jax.ShapeDtypeStruct((n, MEM_W), BF16),
        compiler_params=_cparams(("parallel", "parallel")),
        name="mem_attn",
    )(mq, mkt, mvt)


def _store_token_tiles(ref, x):
    rows = x.shape[0]
    for c in range(SUBLANES):
        ref[pl.ds(c, rows, stride=SUBLANES), :] = x[:, c * LANES:(c + 1) * LANES]


def _load_token_tiles(ref):
    rows = ref.shape[0] // SUBLANES
    return jnp.concatenate([ref[pl.ds(c, rows, stride=SUBLANES), :] for c in range(SUBLANES)], axis=1)


def _split2(x):
    a = x.astype(BF16)
    b = (x - a.astype(F32)).astype(BF16)
    return a, b


def _post_body(x_ref, sba_ref, conv_ref, mem_ref, g0_ref, b0_ref, wo_ref, g1_ref, b1_ref,
               wr_ref, br_ref, x1_ref, route_ref):
    xn = _ln(x_ref[...], g0_ref[...], b0_ref[...])
    mix = _dot(sba_ref[...], wo_ref[0:SBA_W, :])
    mix = mix + _dot(conv_ref[...], wo_ref[SBA_W:SBA_W + CONV_W, :])
    mix = mix + _dot(mem_ref[...], wo_ref[SBA_W + CONV_W:SBA_W + CONV_W + MEM_W, :])
    x1 = _ln(DEEPNORM_ALPHA * xn + mix, g1_ref[...], b1_ref[...])
    _store_token_tiles(x1_ref, x1)

    a0, a1 = _split2(x1)
    hi = _dot(a0, wr_ref[...])
    logits = hi[:, 0:LANES] + (hi[:, LANES:2 * LANES] + _dot(a1, wr_ref[:, 0:LANES])) + br_ref[...]
    tm = logits.shape[0]
    lane = lax.broadcasted_iota(jnp.int32, (tm, LANES), 1).astype(F32)
    neg = jnp.float32(-jnp.inf)
    big = jnp.float32(LANES)
    is_g = jnp.logical_and(lane >= N_EXPERTS, lane < N_EXPERTS + N_GROUPS)
    gl = jnp.where(is_g, logits, neg)
    gmax = jnp.max(gl, axis=-1, keepdims=True)
    g_idx = jnp.min(jnp.where(gl == gmax, lane, big), axis=-1, keepdims=True) - N_EXPERTS
    g_w = 1.0 / jnp.sum(jnp.exp(gl - gmax), axis=-1, keepdims=True)
    in_grp = jnp.logical_and(lane >= g_idx * EXPERTS_PER_GROUP, lane < (g_idx + 1.0) * EXPERTS_PER_GROUP)
    el = jnp.where(in_grp, logits, neg)
    v1 = jnp.max(el, axis=-1, keepdims=True)
    i1 = jnp.min(jnp.where(el == v1, lane, big), axis=-1, keepdims=True)
    el2 = jnp.where(lane == i1, neg, el)
    v2 = jnp.max(el2, axis=-1, keepdims=True)
    i2 = jnp.min(jnp.where(el2 == v2, lane, big), axis=-1, keepdims=True)
    e2 = jnp.exp(v2 - v1)
    p1 = 1.0 / (1.0 + e2)
    p2 = e2 / (1.0 + e2)
    route_ref[...] = jnp.where(lane == 0.0, i1, jnp.where(lane == 1.0, i2, jnp.where(
        lane == 2.0, p1 * g_w, jnp.where(lane == 3.0, p2 * g_w, 0.0))))


def _post_kernel(xp_ref, xs_ref, sbap_ref, sbas_ref, convp_ref, convs_ref, memp_ref, mems_ref,
                 g0_ref, b0_ref, wo_ref, g1_ref, b1_ref, wr_ref, br_ref, x1_ref, route_ref, *, prompt_tiles):
    i = pl.program_id(0)
    shared = (g0_ref, b0_ref, wo_ref, g1_ref, b1_ref, wr_ref, br_ref, x1_ref, route_ref)

    @pl.when(i < prompt_tiles)
    def _():
        _post_body(xp_ref, sbap_ref, convp_ref, memp_ref, *shared)

    @pl.when(i >= prompt_tiles)
    def _():
        _post_body(xs_ref, sbas_ref, convs_ref, mems_ref, *shared)


def _post(xp, xs, sbap, sbas, convp, convs, memp, mems, g0, b0, wo_bf16, g1, b1, wr3, br, tm):
    pt = xp.shape[0] // tm
    st = xs.shape[0] // tm
    n = xp.shape[0] + xs.shape[0]
    prow = lambda w: pl.BlockSpec((tm, w), lambda i: (jnp.minimum(i, pt - 1), 0))
    srow = lambda w: pl.BlockSpec((tm, w), lambda i: (jnp.maximum(i - pt, 0), 0))
    full = lambda s: pl.BlockSpec(s, lambda i: (0,) * len(s))
    widths = (D_MODEL, SBA_W, CONV_W, MEM_W)
    return pl.pallas_call(
        functools.partial(_post_kernel, prompt_tiles=pt),
        grid=(pt + st,),
        in_specs=[spec(w) for w in widths for spec in (prow, srow)] + [
            full((1, D_MODEL)), full((1, D_MODEL)), full((D_MODEL, D_MODEL)),
            full((1, D_MODEL)), full((1, D_MODEL)),
            full((D_MODEL, 2 * LANES)), full((1, LANES))],
        out_specs=[pl.BlockSpec((tm * SUBLANES, LANES), lambda i: (i, 0)),
                   pl.BlockSpec((tm, LANES), lambda i: (i, 0))],
        out_shape=[jax.ShapeDtypeStruct((n * SUBLANES, LANES), F32),
                   jax.ShapeDtypeStruct((n, LANES), F32)],
        compiler_params=_cparams(("arbitrary",)),
        name="post",
    )(xp, xs, sbap, sbas, convp, convs, memp, mems, g0, b0, wo_bf16, g1, b1, wr3, br)


def _moe_plan(route, n):
    tiles = n // MOE_TM + N_GROUPS
    e1 = route[:, 0].astype(jnp.int32)
    e2 = route[:, 1].astype(jnp.int32)
    g = e1 // EXPERTS_PER_GROUP
    _, tok_s, l1_s, l2_s, w1_s, w2_s = lax.sort(
        (g, jnp.arange(n, dtype=jnp.int32), e1 % EXPERTS_PER_GROUP, e2 % EXPERTS_PER_GROUP,
         route[:, 2], route[:, 3]), num_keys=1, is_stable=True)
    counts = jnp.sum((g[:, None] == jnp.arange(N_GROUPS, dtype=jnp.int32)[None, :]).astype(jnp.int32), axis=0)
    ntile = (counts + MOE_TM - 1) // MOE_TM
    tile_end = jnp.cumsum(ntile)
    tile_start = tile_end - ntile
    first = jnp.cumsum(counts) - counts
    n_active = tile_end[-1]
    t = jnp.arange(tiles, dtype=jnp.int32)
    tg = jnp.minimum(jnp.sum((t[:, None] >= tile_end[None, :]).astype(jnp.int32), axis=1), N_GROUPS - 1)
    done = (t - tile_start[tg]) * MOE_TM
    n_valid = jnp.where(t < n_active, jnp.clip(counts[tg] - done, 0, MOE_TM), 0)
    r = jnp.arange(MOE_TM, dtype=jnp.int32)
    valid = r[None, :] < n_valid[:, None]
    shift = tile_start * MOE_TM - first
    max_shift = N_GROUPS * MOE_TM
    row_group = jnp.repeat(tg, MOE_TM)

    def runs(a):
        ext = jnp.concatenate([jnp.zeros((max_shift,), a.dtype), a, jnp.zeros((tiles * MOE_TM - n,), a.dtype)])
        out = jnp.zeros((tiles * MOE_TM,), a.dtype)
        for gi in range(N_GROUPS):
            moved = lax.dynamic_slice(ext, (max_shift - shift[gi],), (tiles * MOE_TM,))
            out = jnp.where(row_group == gi, moved, out)
        return out.reshape(tiles, MOE_TM)

    tok = runs(tok_s)
    src = jnp.where(valid, tok, 0)
    dst = jnp.where(valid, tok, n + r[None, :])
    ids = jnp.arange(EXPERTS_PER_GROUP, dtype=jnp.int32)
    gate = (jnp.where(runs(l1_s)[..., None] == ids, runs(w1_s)[..., None], 0.0)
            + jnp.where(runs(l2_s)[..., None] == ids, runs(w2_s)[..., None], 0.0))
    gate = jnp.where(valid[..., None], gate, 0.0).reshape(tiles * MOE_TM, EXPERTS_PER_GROUP)
    as_rows = lambda a: (a * SUBLANES).astype(jnp.int32).reshape(tiles, 1, MOE_TM)
    return tg.astype(jnp.int32), n_active.reshape(1).astype(jnp.int32), as_rows(src), as_rows(dst), gate


def _moe_kernel(tg_ref, na_ref, src0_ref, src1_ref, dst_ref, gate_ref, x_hbm, wg_ref, wu_ref, wd_ref,
                y_hbm, xg_ref, xb_ref, acc_ref, yo_ref, gsem, ssem):
    t = pl.program_id(0)
    j = pl.program_id(1)
    n_active = na_ref[0]
    tile_rows = MOE_TM * SUBLANES

    def token(ref, row0):
        return ref.at[pl.ds(pl.multiple_of(row0, SUBLANES), SUBLANES)]

    def start_gather(src_ref, slot):
        def body(i, carry):
            for k in range(MOE_UNROLL):
                r = i * MOE_UNROLL + k
                pltpu.make_async_copy(token(x_hbm, src_ref[0, 0, r]), token(xg_ref.at[slot], r * SUBLANES),
                                      gsem.at[slot]).start(priority=k % 2)
            return carry
        lax.fori_loop(0, MOE_TM // MOE_UNROLL, body, 0)

    def wait_gather(slot):
        pltpu.make_async_copy(x_hbm.at[pl.ds(0, tile_rows)], xg_ref.at[slot], gsem.at[slot]).wait()

    def wait_scatter():
        pltpu.make_async_copy(yo_ref, y_hbm.at[pl.ds(0, tile_rows)], ssem.at[0]).wait()

    @pl.when(t < n_active)
    def _():
        slot = lax.rem(t, 2)

        @pl.when(j == 0)
        def _():
            @pl.when(t == 0)
            def _():
                start_gather(src0_ref, 0)
                yo_ref[...] = jnp.zeros_like(yo_ref)
                sink = pltpu.make_async_copy(yo_ref, y_hbm.at[pl.ds(y_hbm.shape[0] - tile_rows, tile_rows)],
                                             ssem.at[0])
                sink.start()
                sink.wait()

            @pl.when(t + 1 < n_active)
            def _():
                start_gather(src1_ref, 1 - slot)

            wait_gather(slot)
            xb_ref[...] = _load_token_tiles(xg_ref.at[slot]).astype(BF16)
            acc_ref[...] = jnp.zeros_like(acc_ref)

        x = xb_ref[...]
        a = _dot(x, wg_ref[0].astype(BF16))
        u = _dot(x, wu_ref[0].astype(BF16))
        gate = gate_ref[...]
        lane = lax.broadcasted_iota(jnp.int32, gate.shape, 1)
        ge = jnp.sum(jnp.where(lane == j, gate, 0.0), axis=1, keepdims=True)
        hid = (a * jax.nn.sigmoid(a)) * u * ge
        acc_ref[...] += _dot(hid.astype(BF16), wd_ref[0].astype(BF16))

        @pl.when(j == EXPERTS_PER_GROUP - 1)
        def _():
            @pl.when(t > 0)
            def _():
                wait_scatter()

            _store_token_tiles(yo_ref, acc_ref[...])

            def body(i, carry):
                for k in range(MOE_UNROLL):
                    r = i * MOE_UNROLL + k
                    pltpu.make_async_copy(token(yo_ref, r * SUBLANES), token(y_hbm, dst_ref[0, 0, r]),
                                          ssem.at[0]).start(priority=k % 2)
                return carry
            lax.fori_loop(0, MOE_TM // MOE_UNROLL, body, 0)

            @pl.when(t == n_active - 1)
            def _():
                wait_scatter()


def _moe(x1, plan, w_eg, w_eu, w_ed):
    tg, n_active, src, dst, gate = plan
    tiles = src.shape[0]
    tile_rows = MOE_TM * SUBLANES
    idx = lambda f: pl.BlockSpec((1, 1, MOE_TM), f, memory_space=pltpu.SMEM)
    expert = lambda a, b: pl.BlockSpec((1, a, b), lambda t, j, tg, na: (tg[t] * EXPERTS_PER_GROUP + j, 0, 0))
    grid_spec = pltpu.PrefetchScalarGridSpec(
        num_scalar_prefetch=2,
        grid=(tiles, EXPERTS_PER_GROUP),
        in_specs=[
            idx(lambda t, j, tg, na: (t, 0, 0)),
            idx(lambda t, j, tg, na: (jnp.minimum(t + 1, tiles - 1), 0, 0)),
            idx(lambda t, j, tg, na: (t, 0, 0)),
            pl.BlockSpec((MOE_TM, EXPERTS_PER_GROUP), lambda t, j, tg, na: (t, 0)),
            pl.BlockSpec(memory_space=pl.ANY),
            expert(D_MODEL, D_EXPERT), expert(D_MODEL, D_EXPERT), expert(D_EXPERT, D_MODEL),
        ],
        out_specs=pl.BlockSpec(memory_space=pl.ANY),
        scratch_shapes=[pltpu.VMEM((2, tile_rows, LANES), F32), pltpu.VMEM((MOE_TM, D_MODEL), BF16),
                        pltpu.VMEM((MOE_TM, D_MODEL), F32), pltpu.VMEM((tile_rows, LANES), F32),
                        pltpu.SemaphoreType.DMA((2,)), pltpu.SemaphoreType.DMA((1,))],
    )
    return pl.pallas_call(
        _moe_kernel,
        grid_spec=grid_spec,
        out_shape=jax.ShapeDtypeStruct((x1.shape[0] + tile_rows, LANES), F32),
        compiler_params=_cparams(("arbitrary", "arbitrary")),
        name="moe",
    )(tg, n_active, src, src, dst, gate, x1, w_eg, w_eu, w_ed)


def _combine_kernel(x1_ref, y_ref, g_ref, b_ref, o_ref):
    o_ref[...] = _ln(DEEPNORM_ALPHA * _load_token_tiles(x1_ref) + _load_token_tiles(y_ref), g_ref[...], b_ref[...])


def _combine(x1, y, g2, b2, row0, rows, tm):
    b0 = row0 // tm
    tiles = pl.BlockSpec((tm * SUBLANES, LANES), lambda i: (i + b0, 0))
    full = pl.BlockSpec((1, D_MODEL), lambda i: (0, 0))
    return pl.pallas_call(
        _combine_kernel,
        grid=(rows // tm,),
        in_specs=[tiles, tiles, full, full],
        out_specs=pl.BlockSpec((tm, D_MODEL), lambda i: (i, 0)),
        out_shape=jax.ShapeDtypeStruct((rows, D_MODEL), F32),
        compiler_params=_cparams(("parallel",)),
        name="combine",
    )(x1, y, g2, b2)


def _later_or_same(n):
    r = lax.broadcasted_iota(jnp.int32, (n, n), 0)
    c = lax.broadcasted_iota(jnp.int32, (n, n), 1)
    return (r >= c).astype(BF16)


def kernel(x_prompt, x_sample, mem_prompt, cache_sba_k, cache_sba_v, cache_conv, cache_mem_k, cache_mem_v,
           ln0_g, ln0_b, w_in, w_dw, b_dw, lnc_g, lnc_b, w_cpw, w_mk, w_mv, w_out, ln1_g, ln1_b,
           w_rg, b_rg, w_re, b_re, w_eg, w_eu, w_ed, ln2_g, ln2_b):
    bp, tp, _ = x_prompt.shape
    bs, ts, _ = x_sample.shape
    past = cache_sba_k.shape[2]
    l = 0
    r2 = lambda a: a.reshape(1, -1)

    w_in_b = w_in[l].astype(BF16)
    w_out_b = w_out[l].astype(BF16)
    w_cpw_b = w_cpw[l].astype(BF16)
    wr = jnp.zeros((D_MODEL, LANES), F32)
    wr = wr.at[:, 0:N_EXPERTS].set(w_re[l]).at[:, N_EXPERTS:N_EXPERTS + N_GROUPS].set(w_rg[l])
    wr0 = wr.astype(BF16)
    wr1 = (wr - wr0.astype(F32)).astype(BF16)
    wr3 = jnp.concatenate([wr0, wr1], axis=1)
    br = jnp.zeros((1, LANES), F32)
    br = br.at[0, 0:N_EXPERTS].set(b_re[l]).at[0, N_EXPERTS:N_EXPERTS + N_GROUPS].set(b_rg[l])
    g0, b0 = r2(ln0_g), r2(ln0_b)

    tq = 256
    tk_s = 256
    u_p = _later_or_same(tq)
    u_s = _later_or_same(tk_s)
    u_n = _later_or_same(ts)

    xp2d = x_prompt.reshape(bp * tp, D_MODEL)
    xs2d = x_sample.reshape(bs * ts, D_MODEL)
    wkt = w_in[l][:, SBA_W:2 * SBA_W].T.astype(BF16)
    wvt = w_in[l][:, 2 * SBA_W:3 * SBA_W].T.astype(BF16)
    qp, ktbp, vtbp, ktp, vtp, glup, mqp = _in_proj_t(xp2d, g0, b0, w_in_b, wkt, wvt, bp, tp, 512, tq)
    qs, kbs, vbs, ks, vs, glus, mqs = _in_proj(xs2d, g0, b0, w_in_b, 512)

    sba_p = _sba_prompt(qp, ktbp, vtbp, u_p, bp, tp, tq, 4)
    kc = jnp.transpose(cache_sba_k[l], (0, 2, 3, 1)).reshape(bs, SBA_W, past)
    vc = jnp.transpose(cache_sba_v[l], (0, 2, 3, 1)).reshape(bs, SBA_W, past)
    sba_s = _sba_sample(qs, kbs, vbs, kc, vc, u_s, u_n, bs, ts, tk_s, 8)

    conv_w = (w_dw[l], r2(b_dw[l]), r2(lnc_g[l]), r2(lnc_b[l]), w_cpw_b)
    h0_p = jnp.zeros((bp, HALO, CONV_W), F32)
    h0_s = jnp.pad(cache_conv[l], ((0, 0), (HALO - (CONV_K - 1), 0), (0, 0)))
    conv_p = _conv_module(h0_p, glup, *conv_w, bp, tp, 256)
    conv_s = _conv_module(h0_s, glus, *conv_w, bs, ts, ts)

    mkt, mvt = _mem_kv(mem_prompt.reshape(bp * N_MEM, D_MODEL), w_mk[l].T.astype(BF16),
                       w_mv[l].T.astype(BF16), bp)
    mem_p = _mem_attn(mqp, mkt, mvt, bp, tp, 512)
    mem_s = _mem_attn(mqs, jnp.transpose(cache_mem_k[l], (0, 2, 3, 1)).reshape(bs, MEM_W, N_MEM),
                      jnp.transpose(cache_mem_v[l], (0, 2, 3, 1)).reshape(bs, MEM_W, N_MEM), bs, ts, ts)

    post_w = (g0, b0, w_out_b, r2(ln1_g[l]), r2(ln1_b[l]), wr3, br)
    x1, route = _post(xp2d, xs2d, sba_p, sba_s, conv_p, conv_s, mem_p, mem_s, *post_w, 512)
    f = _moe(x1, _moe_plan(route, route.shape[0]), w_eg[l], w_eu[l], w_ed[l])
    g2, b2 = r2(ln2_g[l]), r2(ln2_b[l])
    yp = _combine(x1, f, g2, b2, 0, bp * tp, 512)
    ys = _combine(x1, f, g2, b2, bp * tp, bs * ts, 512)

    hd = (SBA_HEADS, HEAD_DIM)
    glup3 = glup.reshape(bp, tp, CONV_W)
    glus3 = glus.reshape(bs, ts, CONV_W)
    conv_tail_s = jnp.concatenate([cache_conv[l], glus3], axis=1)[:, -(CONV_K - 1):]
    return (
        yp.reshape(bp, tp, D_MODEL),
        ys.reshape(bs, ts, D_MODEL),
        jnp.transpose(ktp.reshape(bp, *hd, tp), (0, 3, 1, 2))[None],
        jnp.transpose(vtp.reshape(bp, *hd, tp), (0, 3, 1, 2))[None],
        glup3[:, -(CONV_K - 1):][None],
        jnp.transpose(mkt.reshape(bp, MEM_HEADS, HEAD_DIM, N_MEM), (0, 3, 1, 2))[None],
        jnp.transpose(mvt.reshape(bp, MEM_HEADS, HEAD_DIM, N_MEM), (0, 3, 1, 2))[None],
        ks.reshape(1, bs, ts, *hd),
        vs.reshape(1, bs, ts, *hd),
        conv_tail_s[None],
    )
```

```python
import functools

import jax
import jax.numpy as jnp
from jax import lax
from jax.experimental import pallas as pl
from jax.experimental.pallas import tpu as pltpu

F32 = jnp.float32
BF16 = jnp.bfloat16

D_MODEL = 1024
HEAD_DIM = 64
SBA_HEADS = 8
SBA_W = SBA_HEADS * HEAD_DIM
CONV_W = 256
CONV_K = 31
MEM_HEADS = 4
MEM_W = MEM_HEADS * HEAD_DIM
N_MEM = 256
N_GROUPS = 4
EXPERTS_PER_GROUP = 8
N_EXPERTS = N_GROUPS * EXPERTS_PER_GROUP
D_EXPERT = 256
DEPTH = 1
DEEPNORM_ALPHA = (2 * DEPTH) ** 0.25
QK_SCALE = HEAD_DIM ** -0.5
LN_EPS = 1e-5
LOG2E = 1.4426950408889634

LANES = 128
SUBLANES = 8
HALO = 32
MOE_TM = 1024
MOE_UNROLL = 8
SP_LINEAR = 100.0
VMEM_LIMIT = 48 * 1024 * 1024


def _cparams(sem):
    return pltpu.CompilerParams(dimension_semantics=sem, vmem_limit_bytes=VMEM_LIMIT)


def _ln(x, g, b):
    mu = jnp.mean(x, axis=-1, keepdims=True)
    xc = x - mu
    var = jnp.mean(xc * xc, axis=-1, keepdims=True)
    return xc * lax.rsqrt(var + LN_EPS) * g + b


def _dot(a, b):
    return jnp.dot(a, b, preferred_element_type=F32)


def _dot_nt(a, b):
    return lax.dot_general(a, b, (((1,), (1,)), ((), ())), preferred_element_type=F32)


def _in_proj_kernel(x_ref, g_ref, b_ref, w_ref, q_ref, kb_ref, vb_ref, k_ref, v_ref, glu_ref, mq_ref):
    xn = _ln(x_ref[...], g_ref[...], b_ref[...]).astype(BF16)
    q = _dot(xn, w_ref[:, 0:SBA_W])
    q_ref[...] = (q * (QK_SCALE * LOG2E)).astype(BF16)
    k = _dot(xn, w_ref[:, SBA_W:2 * SBA_W])
    k_ref[...] = k
    kb_ref[...] = k.astype(BF16)
    v = _dot(xn, w_ref[:, 2 * SBA_W:3 * SBA_W])
    v_ref[...] = v
    vb_ref[...] = v.astype(BF16)
    c0 = 3 * SBA_W
    cv = _dot(xn, w_ref[:, c0:c0 + CONV_W])
    cg = _dot(xn, w_ref[:, c0 + CONV_W:c0 + 2 * CONV_W])
    glu_ref[...] = cv * jax.nn.sigmoid(cg)
    mq = _dot(xn, w_ref[:, c0 + 2 * CONV_W:c0 + 2 * CONV_W + MEM_W])
    mq_ref[...] = (mq * QK_SCALE).astype(BF16)


def _in_proj_t_kernel(x_ref, g_ref, b_ref, w_ref, wkt_ref, wvt_ref,
                      q_ref, ktb_ref, vtb_ref, kt_ref, vt_ref, glu_ref, mq_ref, *, tq):
    xn = _ln(x_ref[...], g_ref[...], b_ref[...]).astype(BF16)
    tm = xn.shape[0]
    q = _dot(xn, w_ref[:, 0:SBA_W])
    q_ref[...] = (q * (QK_SCALE * LOG2E)).astype(BF16)
    for wt_ref, t_ref, tb_ref in ((wkt_ref, kt_ref, ktb_ref), (wvt_ref, vt_ref, vtb_ref)):
        t = _dot_nt(wt_ref[...], xn)
        t_ref[0] = t
        for i in range(tm // tq):
            tb_ref[0, i] = t[:, i * tq:(i + 1) * tq].astype(BF16)
    c0 = 3 * SBA_W
    cv = _dot(xn, w_ref[:, c0:c0 + CONV_W])
    cg = _dot(xn, w_ref[:, c0 + CONV_W:c0 + 2 * CONV_W])
    glu_ref[...] = cv * jax.nn.sigmoid(cg)
    mq = _dot(xn, w_ref[:, c0 + 2 * CONV_W:c0 + 2 * CONV_W + MEM_W])
    mq_ref[...] = (mq * QK_SCALE).astype(BF16)


def _in_proj_t(x2d, g, b, w_bf16, wkt, wvt, batch, seq, tm, tq):
    n = x2d.shape[0]
    in_w = w_bf16.shape[1]
    nt = seq // tm
    row = lambda w: pl.BlockSpec((tm, w), lambda bi, i: (bi * nt + i, 0))
    full = lambda s: pl.BlockSpec(s, lambda bi, i: (0, 0))
    tr = pl.BlockSpec((1, SBA_W, tm), lambda bi, i: (bi, 0, i))
    trb = pl.BlockSpec((1, tm // tq, SBA_W, tq), lambda bi, i: (bi, i, 0, 0))
    return pl.pallas_call(
        functools.partial(_in_proj_t_kernel, tq=tq),
        grid=(batch, nt),
        in_specs=[row(D_MODEL), full((1, D_MODEL)), full((1, D_MODEL)), full((D_MODEL, in_w)),
                  full((SBA_W, D_MODEL)), full((SBA_W, D_MODEL))],
        out_specs=[row(SBA_W), trb, trb, tr, tr, row(CONV_W), row(MEM_W)],
        out_shape=[
            jax.ShapeDtypeStruct((n, SBA_W), BF16),
            jax.ShapeDtypeStruct((batch, seq // tq, SBA_W, tq), BF16),
            jax.ShapeDtypeStruct((batch, seq // tq, SBA_W, tq), BF16),
            jax.ShapeDtypeStruct((batch, SBA_W, seq), F32),
            jax.ShapeDtypeStruct((batch, SBA_W, seq), F32),
            jax.ShapeDtypeStruct((n, CONV_W), F32),
            jax.ShapeDtypeStruct((n, MEM_W), BF16),
        ],
        compiler_params=_cparams(("parallel", "parallel")),
        name="in_proj_t",
    )(x2d, g, b, w_bf16, wkt, wvt)


def _in_proj(x2d, g, b, w_bf16, tm):
    n = x2d.shape[0]
    in_w = w_bf16.shape[1]
    row = lambda w: pl.BlockSpec((tm, w), lambda i: (i, 0))
    full = lambda s: pl.BlockSpec(s, lambda i: (0, 0))
    return pl.pallas_call(
        _in_proj_kernel,
        grid=(n // tm,),
        in_specs=[row(D_MODEL), full((1, D_MODEL)), full((1, D_MODEL)), full((D_MODEL, in_w))],
        out_specs=[row(SBA_W), row(SBA_W), row(SBA_W), row(SBA_W), row(SBA_W), row(CONV_W), row(MEM_W)],
        out_shape=[
            jax.ShapeDtypeStruct((n, SBA_W), BF16),
            jax.ShapeDtypeStruct((n, SBA_W), BF16),
            jax.ShapeDtypeStruct((n, SBA_W), BF16),
            jax.ShapeDtypeStruct((n, SBA_W), F32),
            jax.ShapeDtypeStruct((n, SBA_W), F32),
            jax.ShapeDtypeStruct((n, CONV_W), F32),
            jax.ShapeDtypeStruct((n, MEM_W), BF16),
        ],
        compiler_params=_cparams(("parallel",)),
        name="in_proj",
    )(x2d, g, b, w_bf16)


def _head_masks(shape):
    lane = lax.broadcasted_iota(jnp.int32, shape, 1)
    return lane < HEAD_DIM


def _stack_heads(q, low):
    zero = jnp.zeros_like(q)
    return jnp.concatenate([jnp.where(low, q, zero), jnp.where(low, zero, q)], axis=0)


def _sba_tile(z, log2_weight, mask):
    sp = jnp.where(z > SP_LINEAR, z, jnp.log(1.0 + jnp.exp2(z)) * LOG2E)
    if mask is not None:
        sp = jnp.where(mask, sp, 0.0)
    w = jnp.exp2(log2_weight(sp.astype(BF16)))
    if mask is not None:
        w = jnp.where(mask, w, 0.0)
    return w.astype(BF16), jnp.sum(sp, axis=1, keepdims=True)


def _sba_prompt_kernel(q_ref, k_ref, v_ref, u_ref, o_ref, *, tq, hp):
    qi = pl.program_id(2)
    low = _head_masks((tq, LANES))
    u = u_ref[...]
    row = lax.broadcasted_iota(jnp.int32, (2 * tq, tq), 0) & (tq - 1)
    col = lax.broadcasted_iota(jnp.int32, (2 * tq, tq), 1)
    qs = [_stack_heads(q_ref[:, p * LANES:(p + 1) * LANES], low) for p in range(hp)]

    def blk(j, carry, mask):
        out = []
        for p in range(hp):
            c, acc = carry[p]
            z = _dot(qs[p], k_ref[0, j, p * LANES:(p + 1) * LANES, :])
            w, rs = _sba_tile(z, lambda sp: (z - c) - _dot(sp, u), mask)
            acc = acc + _dot_nt(w, v_ref[0, j, p * LANES:(p + 1) * LANES, :])
            out.append((c + rs, acc))
        return tuple(out)

    carry = tuple((jnp.zeros((2 * tq, 1), F32), jnp.zeros((2 * tq, LANES), F32)) for _ in range(hp))
    carry = blk(qi, carry, col < row)
    carry = lax.fori_loop(0, qi // 2,
                          lambda t, cr: blk(qi - 2 - 2 * t, blk(qi - 1 - 2 * t, cr, None), None), carry)
    carry = lax.cond(qi % 2 == 1, lambda cr: blk(0, cr, None), lambda cr: cr, carry)
    for p in range(hp):
        acc = carry[p][1]
        o_ref[:, p * LANES:(p + 1) * LANES] = jnp.where(low, acc[0:tq], acc[tq:2 * tq]).astype(o_ref.dtype)


def _sba_prompt(q, kt, vt, u, batch, seq, tq, hp):
    n = q.shape[0]
    nq = seq // tq
    groups = SBA_W // (LANES * hp)
    kv = pl.BlockSpec((1, nq, hp * LANES, tq), lambda b, p, i: (b, 0, p, 0))
    return pl.pallas_call(
        functools.partial(_sba_prompt_kernel, tq=tq, hp=hp),
        grid=(batch, groups, nq),
        in_specs=[
            pl.BlockSpec((tq, hp * LANES), lambda b, p, i: (b * nq + i, p)),
            kv, kv,
            pl.BlockSpec((tq, tq), lambda b, p, i: (0, 0)),
        ],
        out_specs=pl.BlockSpec((tq, hp * LANES), lambda b, p, i: (b * nq + i, p)),
        out_shape=jax.ShapeDtypeStruct((n, SBA_W), BF16),
        compiler_params=_cparams(("parallel", "parallel", "arbitrary")),
        name="sba_prompt",
    )(q, kt, vt, u)


def _sba_sample_kernel(q_ref, kn_ref, vn_ref, kc_ref, vc_ref, u_ref, un_ref, o_ref, qs_ref, c_ref, acc_ref,
                       *, tn, tk, nsub):
    j = pl.program_id(1)
    nj = pl.num_programs(1)
    low = _head_masks((tn, LANES))
    pairs = SBA_W // LANES
    rows = 2 * tn

    def step(logits, weighted, u, mask):
        z = jnp.concatenate([logits(qs_ref[p], p) for p in range(pairs)], axis=0)
        c = c_ref[...]
        w, rs = _sba_tile(z, lambda sp: (z - c) - _dot(sp, u), mask)
        acc_ref[...] += jnp.concatenate([weighted(w[p * rows:(p + 1) * rows], p) for p in range(pairs)], axis=0)
        c_ref[...] = c + rs

    def lanes(p):
        return slice(p * LANES, (p + 1) * LANES)

    @pl.when(j == 0)
    def _():
        for p in range(pairs):
            qs_ref[p] = _stack_heads(q_ref[:, lanes(p)], low)
        c_ref[...] = jnp.zeros_like(c_ref)
        acc_ref[...] = jnp.zeros_like(acc_ref)
        row = lax.broadcasted_iota(jnp.int32, (pairs * rows, tn), 0) & (tn - 1)
        col = lax.broadcasted_iota(jnp.int32, (pairs * rows, tn), 1)
        step(lambda q, p: _dot_nt(q, kn_ref[:, lanes(p)]), lambda w, p: _dot(w, vn_ref[:, lanes(p)]),
             un_ref[...], col < row)

    for s in reversed(range(nsub)):
        keys = slice(s * tk, (s + 1) * tk)
        step(lambda q, p: _dot(q, kc_ref[0, lanes(p), keys].astype(BF16)),
             lambda w, p: _dot_nt(w, vc_ref[0, lanes(p), keys].astype(BF16)),
             u_ref[...], None)

    @pl.when(j == nj - 1)
    def _():
        for p in range(pairs):
            a0 = acc_ref[p * rows:p * rows + tn, :]
            a1 = acc_ref[p * rows + tn:(p + 1) * rows, :]
            o_ref[:, lanes(p)] = jnp.where(low, a0, a1).astype(o_ref.dtype)


def _sba_sample(q, kn, vn, kc, vc, u, un, batch, tn, tk, nsub):
    n = q.shape[0]
    past = kc.shape[2]
    tkb = tk * nsub
    nj = past // tkb
    rows = SBA_HEADS * tn
    cache = pl.BlockSpec((1, SBA_W, tkb), lambda b, j: (b, 0, nj - 1 - j))
    return pl.pallas_call(
        functools.partial(_sba_sample_kernel, tn=tn, tk=tk, nsub=nsub),
        grid=(batch, nj),
        in_specs=[
            pl.BlockSpec((tn, SBA_W), lambda b, j: (b, 0)),
            pl.BlockSpec((tn, SBA_W), lambda b, j: (b, 0)),
            pl.BlockSpec((tn, SBA_W), lambda b, j: (b, 0)),
            cache, cache,
            pl.BlockSpec((tk, tk), lambda b, j: (0, 0)),
            pl.BlockSpec((tn, tn), lambda b, j: (0, 0)),
        ],
        out_specs=pl.BlockSpec((tn, SBA_W), lambda b, j: (b, 0)),
        out_shape=jax.ShapeDtypeStruct((n, SBA_W), BF16),
        scratch_shapes=[pltpu.VMEM((SBA_W // LANES, 2 * tn, LANES), BF16),
                        pltpu.VMEM((rows, 1), F32), pltpu.VMEM((rows, LANES), F32)],
        compiler_params=_cparams(("parallel", "arbitrary")),
        name="sba_sample",
    )(q, kn, vn, kc, vc, u, un)


def _conv_kernel(h0_ref, glu_ref, wdw_ref, bdw_ref, g_ref, b_ref, wpw_ref, o_ref, buf_ref, *, tt, rc):
    ti = pl.program_id(1)
    rows = HALO + tt

    @pl.when(ti == 0)
    def _():
        buf_ref[0, 0:HALO, :] = h0_ref[0]

    @pl.when(ti > 0)
    def _():
        buf_ref[0, 0:HALO, :] = buf_ref[0, tt:tt + HALO, :]

    buf_ref[0, HALO:rows, :] = glu_ref[...]
    buf_ref[0, rows:rows + SUBLANES, :] = jnp.zeros((SUBLANES, CONV_W), F32)
    for s in range(1, SUBLANES):
        buf_ref[s, 0:rows, :] = buf_ref[0, pl.ds(s, rows), :]
    base = HALO - (CONV_K - 1)
    wdw = wdw_ref[...]
    for r0 in range(0, tt, rc):
        acc = jnp.zeros((rc, CONV_W), F32)
        for kk in range(CONV_K):
            off = base + r0 + kk
            acc = acc + buf_ref[off % SUBLANES, pl.ds(off - off % SUBLANES, rc), :] * wdw[kk:kk + 1, :]
        u = _ln(acc + bdw_ref[...], g_ref[...], b_ref[...])
        s = u * jax.nn.sigmoid(u)
        o_ref[pl.ds(r0, rc), :] = _dot(s.astype(BF16), wpw_ref[...]).astype(o_ref.dtype)


def _conv_module(h0, glu, w_dw, b_dw, g, b, w_pw_bf16, batch, seq, tt):
    n = glu.shape[0]
    nt = seq // tt
    rc = min(tt, 64)
    full = lambda s: pl.BlockSpec(s, lambda bi, ti: (0,) * len(s))
    return pl.pallas_call(
        functools.partial(_conv_kernel, tt=tt, rc=rc),
        grid=(batch, nt),
        in_specs=[
            pl.BlockSpec((1, HALO, CONV_W), lambda bi, ti: (bi, 0, 0)),
            pl.BlockSpec((tt, CONV_W), lambda bi, ti: (bi * nt + ti, 0)),
            full((CONV_K, CONV_W)), full((1, CONV_W)), full((1, CONV_W)), full((1, CONV_W)),
            full((CONV_W, CONV_W)),
        ],
        out_specs=pl.BlockSpec((tt, CONV_W), lambda bi, ti: (bi * nt + ti, 0)),
        out_shape=jax.ShapeDtypeStruct((n, CONV_W), BF16),
        scratch_shapes=[pltpu.VMEM((SUBLANES, HALO + tt + SUBLANES, CONV_W), F32)],
        compiler_params=_cparams(("parallel", "arbitrary")),
        name="conv_module",
    )(h0, glu, w_dw, b_dw, g, b, w_pw_bf16)


def _mem_kv_kernel(m_ref, wkt_ref, wvt_ref, k_ref, v_ref):
    m = m_ref[...].astype(BF16)
    k_ref[0] = _dot_nt(wkt_ref[...], m)
    v_ref[0] = _dot_nt(wvt_ref[...], m)


def _mem_kv(mem2d, wkt, wvt, batch):
    full = pl.BlockSpec((MEM_W, D_MODEL), lambda i: (0, 0))
    out = pl.BlockSpec((1, MEM_W, N_MEM), lambda i: (i, 0, 0))
    return pl.pallas_call(
        _mem_kv_kernel,
        grid=(batch,),
        in_specs=[pl.BlockSpec((N_MEM, D_MODEL), lambda i: (i, 0)), full, full],
        out_specs=[out, out],
        out_shape=[jax.ShapeDtypeStruct((batch, MEM_W, N_MEM), F32)] * 2,
        compiler_params=_cparams(("parallel",)),
        name="mem_kv",
    )(mem2d, wkt, wvt)


def _mem_attn_kernel(q_ref, k_ref, v_ref, o_ref):
    q = q_ref[...]
    tq = q.shape[0]
    low = _head_masks((tq, LANES))
    for p in range(MEM_W // LANES):
        qs = _stack_heads(q[:, p * LANES:(p + 1) * LANES], low)
        kt = k_ref[0, p * LANES:(p + 1) * LANES, :].astype(BF16)
        vt = v_ref[0, p * LANES:(p + 1) * LANES, :].astype(BF16)
        s = _dot(qs, kt)
        e = jnp.exp(s - jnp.max(s, axis=-1, keepdims=True))
        o = _dot_nt(e.astype(BF16), vt) / jnp.sum(e, axis=-1, keepdims=True)
        o_ref[:, p * LANES:(p + 1) * LANES] = jnp.where(low, o[0:tq], o[tq:2 * tq]).astype(o_ref.dtype)


def _mem_attn(mq, mkt, mvt, batch, seq, tq):
    n = mq.shape[0]
    nq = seq // tq
    kv = pl.BlockSpec((1, MEM_W, N_MEM), lambda b, i: (b, 0, 0))
    return pl.pallas_call(
        _mem_attn_kernel,
        grid=(batch, nq),
        in_specs=[pl.BlockSpec((tq, MEM_W), lambda b, i: (b * nq + i, 0)), kv, kv],
        out_specs=pl.BlockSpec((tq, MEM_W), lambda b, i: (b * nq + i, 0)),
        out_shape=jax.ShapeDtypeStruct((n, MEM_W), BF16),
        compiler_params=_cparams(("parallel", "parallel")),
        name="mem_attn",
    )(mq, mkt, mvt)


def _store_token_tiles(ref, x):
    rows = x.shape[0]
    for c in range(SUBLANES):
        ref[pl.ds(c, rows, stride=SUBLANES), :] = x[:, c * LANES:(c + 1) * LANES]


def _load_token_tiles(ref):
    rows = ref.shape[0] // SUBLANES
    return jnp.concatenate([ref[pl.ds(c, rows, stride=SUBLANES), :] for c in range(SUBLANES)], axis=1)


def _split2(x):
    a = x.astype(BF16)
    b = (x - a.astype(F32)).astype(BF16)
    return a, b


def _post_body(x_ref, sba_ref, conv_ref, mem_ref, g0_ref, b0_ref, wo_ref, g1_ref, b1_ref,
               wr_ref, br_ref, x1_ref, route_ref):
    xn = _ln(x_ref[...], g0_ref[...], b0_ref[...])
    mix = _dot(sba_ref[...], wo_ref[0:SBA_W, :])
    mix = mix + _dot(conv_ref[...], wo_ref[SBA_W:SBA_W + CONV_W, :])
    mix = mix + _dot(mem_ref[...], wo_ref[SBA_W + CONV_W:SBA_W + CONV_W + MEM_W, :])
    x1 = _ln(DEEPNORM_ALPHA * xn + mix, g1_ref[...], b1_ref[...])
    _store_token_tiles(x1_ref, x1)

    a0, a1 = _split2(x1)
    hi = _dot(a0, wr_ref[...])
    logits = hi[:, 0:LANES] + (hi[:, LANES:2 * LANES] + _dot(a1, wr_ref[:, 0:LANES])) + br_ref[...]
    tm = logits.shape[0]
    lane = lax.broadcasted_iota(jnp.int32, (tm, LANES), 1).astype(F32)
    neg = jnp.float32(-jnp.inf)
    big = jnp.float32(LANES)
    is_g = jnp.logical_and(lane >= N_EXPERTS, lane < N_EXPERTS + N_GROUPS)
    gl = jnp.where(is_g, logits, neg)
    gmax = jnp.max(gl, axis=-1, keepdims=True)
    g_idx = jnp.min(jnp.where(gl == gmax, lane, big), axis=-1, keepdims=True) - N_EXPERTS
    g_w = 1.0 / jnp.sum(jnp.exp(gl - gmax), axis=-1, keepdims=True)
    in_grp = jnp.logical_and(lane >= g_idx * EXPERTS_PER_GROUP, lane < (g_idx + 1.0) * EXPERTS_PER_GROUP)
    el = jnp.where(in_grp, logits, neg)
    v1 = jnp.max(el, axis=-1, keepdims=True)
    i1 = jnp.min(jnp.where(el == v1, lane, big), axis=-1, keepdims=True)
    el2 = jnp.where(lane == i1, neg, el)
    v2 = jnp.max(el2, axis=-1, keepdims=True)
    i2 = jnp.min(jnp.where(el2 == v2, lane, big), axis=-1, keepdims=True)
    e2 = jnp.exp(v2 - v1)
    p1 = 1.0 / (1.0 + e2)
    p2 = e2 / (1.0 + e2)
    route_ref[...] = jnp.where(lane == 0.0, i1, jnp.where(lane == 1.0, i2, jnp.where(
        lane == 2.0, p1 * g_w, jnp.where(lane == 3.0, p2 * g_w, 0.0))))


def _post_kernel(xp_ref, xs_ref, sbap_ref, sbas_ref, convp_ref, convs_ref, memp_ref, mems_ref,
                 g0_ref, b0_ref, wo_ref, g1_ref, b1_ref, wr_ref, br_ref, x1_ref, route_ref, *, prompt_tiles):
    i = pl.program_id(0)
    shared = (g0_ref, b0_ref, wo_ref, g1_ref, b1_ref, wr_ref, br_ref, x1_ref, route_ref)

    @pl.when(i < prompt_tiles)
    def _():
        _post_body(xp_ref, sbap_ref, convp_ref, memp_ref, *shared)

    @pl.when(i >= prompt_tiles)
    def _():
        _post_body(xs_ref, sbas_ref, convs_ref, mems_ref, *shared)


def _post(xp, xs, sbap, sbas, convp, convs, memp, mems, g0, b0, wo_bf16, g1, b1, wr3, br, tm):
    pt = xp.shape[0] // tm
    st = xs.shape[0] // tm
    n = xp.shape[0] + xs.shape[0]
    prow = lambda w: pl.BlockSpec((tm, w), lambda i: (jnp.minimum(i, pt - 1), 0))
    srow = lambda w: pl.BlockSpec((tm, w), lambda i: (jnp.maximum(i - pt, 0), 0))
    full = lambda s: pl.BlockSpec(s, lambda i: (0,) * len(s))
    widths = (D_MODEL, SBA_W, CONV_W, MEM_W)
    return pl.pallas_call(
        functools.partial(_post_kernel, prompt_tiles=pt),
        grid=(pt + st,),
        in_specs=[spec(w) for w in widths for spec in (prow, srow)] + [
            full((1, D_MODEL)), full((1, D_MODEL)), full((D_MODEL, D_MODEL)),
            full((1, D_MODEL)), full((1, D_MODEL)),
            full((D_MODEL, 2 * LANES)), full((1, LANES))],
        out_specs=[pl.BlockSpec((tm * SUBLANES, LANES), lambda i: (i, 0)),
                   pl.BlockSpec((tm, LANES), lambda i: (i, 0))],
        out_shape=[jax.ShapeDtypeStruct((n * SUBLANES, LANES), F32),
                   jax.ShapeDtypeStruct((n, LANES), F32)],
        compiler_params=_cparams(("arbitrary",)),
        name="post",
    )(xp, xs, sbap, sbas, convp, convs, memp, mems, g0, b0, wo_bf16, g1, b1, wr3, br)


def _moe_plan(route, n):
    tiles = n // MOE_TM + N_GROUPS
    e1 = route[:, 0].astype(jnp.int32)
    e2 = route[:, 1].astype(jnp.int32)
    g = e1 // EXPERTS_PER_GROUP
    _, tok_s, l1_s, l2_s, w1_s, w2_s = lax.sort(
        (g, jnp.arange(n, dtype=jnp.int32), e1 % EXPERTS_PER_GROUP, e2 % EXPERTS_PER_GROUP,
         route[:, 2], route[:, 3]), num_keys=1, is_stable=True)
    counts = jnp.sum((g[:, None] == jnp.arange(N_GROUPS, dtype=jnp.int32)[None, :]).astype(jnp.int32), axis=0)
    ntile = (counts + MOE_TM - 1) // MOE_TM
    tile_end = jnp.cumsum(ntile)
    tile_start = tile_end - ntile
    first = jnp.cumsum(counts) - counts
    n_active = tile_end[-1]
    t = jnp.arange(tiles, dtype=jnp.int32)
    tg = jnp.minimum(jnp.sum((t[:, None] >= tile_end[None, :]).astype(jnp.int32), axis=1), N_GROUPS - 1)
    done = (t - tile_start[tg]) * MOE_TM
    n_valid = jnp.where(t < n_active, jnp.clip(counts[tg] - done, 0, MOE_TM), 0)
    r = jnp.arange(MOE_TM, dtype=jnp.int32)
    valid = r[None, :] < n_valid[:, None]
    shift = tile_start * MOE_TM - first
    max_shift = N_GROUPS * MOE_TM
    row_group = jnp.repeat(tg, MOE_TM)

    def runs(a):
        ext = jnp.concatenate([jnp.zeros((max_shift,), a.dtype), a, jnp.zeros((tiles * MOE_TM - n,), a.dtype)])
        out = jnp.zeros((tiles * MOE_TM,), a.dtype)
        for gi in range(N_GROUPS):
            moved = lax.dynamic_slice(ext, (max_shift - shift[gi],), (tiles * MOE_TM,))
            out = jnp.where(row_group == gi, moved, out)
        return out.reshape(tiles, MOE_TM)

    tok = runs(tok_s)
    src = jnp.where(valid, tok, 0)
    dst = jnp.where(valid, tok, n + r[None, :])
    ids = jnp.arange(EXPERTS_PER_GROUP, dtype=jnp.int32)
    gate = (jnp.where(runs(l1_s)[..., None] == ids, runs(w1_s)[..., None], 0.0)
            + jnp.where(runs(l2_s)[..., None] == ids, runs(w2_s)[..., None], 0.0))
    gate = jnp.where(valid[..., None], gate, 0.0).reshape(tiles * MOE_TM, EXPERTS_PER_GROUP)
    as_rows = lambda a: (a * SUBLANES).astype(jnp.int32).reshape(tiles, 1, MOE_TM)
    return tg.astype(jnp.int32), n_active.reshape(1).astype(jnp.int32), as_rows(src), as_rows(dst), gate


def _moe_kernel(tg_ref, na_ref, src0_ref, src1_ref, dst_ref, dstp_ref, gate_ref, x_hbm, wg_ref, wu_ref, wd_ref,
                y_hbm, xg_ref, xb_ref, acc_ref, yo_ref, gsem, ssem):
    t = pl.program_id(0)
    j = pl.program_id(1)
    n_active = na_ref[0]
    tile_rows = MOE_TM * SUBLANES

    def token(ref, row0):
        return ref.at[pl.ds(pl.multiple_of(row0, SUBLANES), SUBLANES)]

    def start_gather(src_ref, slot):
        def body(i, carry):
            for k in range(MOE_UNROLL):
                r = i * MOE_UNROLL + k
                pltpu.make_async_copy(token(x_hbm, src_ref[0, 0, r]), token(xg_ref.at[slot], r * SUBLANES),
                                      gsem.at[slot]).start(priority=k % 2)
            return carry
        lax.fori_loop(0, MOE_TM // MOE_UNROLL, body, 0)

    def wait_gather(slot):
        pltpu.make_async_copy(x_hbm.at[pl.ds(0, tile_rows)], xg_ref.at[slot], gsem.at[slot]).wait()

    def wait_scatter():
        pltpu.make_async_copy(yo_ref, y_hbm.at[pl.ds(0, tile_rows)], ssem.at[0]).wait()

    @pl.when(t < n_active)
    def _():
        slot = lax.rem(t, 2)

        @pl.when(j == 0)
        def _():
            @pl.when(t == 0)
            def _():
                start_gather(src0_ref, 0)
                yo_ref[...] = jnp.zeros_like(yo_ref)
                sink = pltpu.make_async_copy(yo_ref, y_hbm.at[pl.ds(y_hbm.shape[0] - tile_rows, tile_rows)],
                                             ssem.at[0])
                sink.start()
                sink.wait()

            wait_gather(slot)
            xb_ref[...] = _load_token_tiles(xg_ref.at[slot]).astype(BF16)
            acc_ref[...] = jnp.zeros_like(acc_ref)

        x = xb_ref[...]
        a = _dot(x, wg_ref[0].astype(BF16))
        u = _dot(x, wu_ref[0].astype(BF16))
        gate = gate_ref[...]
        lane = lax.broadcasted_iota(jnp.int32, gate.shape, 1)
        ge = jnp.sum(jnp.where(lane == j, gate, 0.0), axis=1, keepdims=True)
        hid = (a * jax.nn.sigmoid(a)) * u * ge
        acc_ref[...] += _dot(hid.astype(BF16), wd_ref[0].astype(BF16))

        share = MOE_TM // EXPERTS_PER_GROUP
        for k in range(share):
            r = j * share + k
            pltpu.make_async_copy(token(x_hbm, src1_ref[0, 0, r]), token(xg_ref.at[1 - slot], r * SUBLANES),
                                  gsem.at[1 - slot]).start(priority=k % 2)
            pltpu.make_async_copy(token(yo_ref, r * SUBLANES), token(y_hbm, dstp_ref[0, 0, r]),
                                  ssem.at[0]).start(priority=(k + 1) % 2)

        @pl.when(j == EXPERTS_PER_GROUP - 1)
        def _():
            wait_scatter()
            _store_token_tiles(yo_ref, acc_ref[...])

            @pl.when(t == n_active - 1)
            def _():
                def body(i, carry):
                    for k in range(MOE_UNROLL):
                        r = i * MOE_UNROLL + k
                        pltpu.make_async_copy(token(yo_ref, r * SUBLANES), token(y_hbm, dst_ref[0, 0, r]),
                                              ssem.at[0]).start(priority=k % 2)
                    return carry
                lax.fori_loop(0, MOE_TM // MOE_UNROLL, body, 0)
                wait_scatter()
                wait_gather(1 - slot)


def _moe(x1, plan, w_eg, w_eu, w_ed):
    tg, n_active, src, dst, gate = plan
    tiles = src.shape[0]
    tile_rows = MOE_TM * SUBLANES
    idx = lambda f: pl.BlockSpec((1, 1, MOE_TM), f, memory_space=pltpu.SMEM)
    expert = lambda a, b: pl.BlockSpec((1, a, b), lambda t, j, tg, na: (tg[t] * EXPERTS_PER_GROUP + j, 0, 0))
    grid_spec = pltpu.PrefetchScalarGridSpec(
        num_scalar_prefetch=2,
        grid=(tiles, EXPERTS_PER_GROUP),
        in_specs=[
            idx(lambda t, j, tg, na: (t, 0, 0)),
            idx(lambda t, j, tg, na: (jnp.minimum(t + 1, tiles - 1), 0, 0)),
            idx(lambda t, j, tg, na: (t, 0, 0)),
            idx(lambda t, j, tg, na: (jnp.maximum(t - 1, 0), 0, 0)),
            pl.BlockSpec((MOE_TM, EXPERTS_PER_GROUP), lambda t, j, tg, na: (t, 0)),
            pl.BlockSpec(memory_space=pl.ANY),
            expert(D_MODEL, D_EXPERT), expert(D_MODEL, D_EXPERT), expert(D_EXPERT, D_MODEL),
        ],
        out_specs=pl.BlockSpec(memory_space=pl.ANY),
        scratch_shapes=[pltpu.VMEM((2, tile_rows, LANES), F32), pltpu.VMEM((MOE_TM, D_MODEL), BF16),
                        pltpu.VMEM((MOE_TM, D_MODEL), F32), pltpu.VMEM((tile_rows, LANES), F32),
                        pltpu.SemaphoreType.DMA((2,)), pltpu.SemaphoreType.DMA((1,))],
    )
    return pl.pallas_call(
        _moe_kernel,
        grid_spec=grid_spec,
        out_shape=jax.ShapeDtypeStruct((x1.shape[0] + tile_rows, LANES), F32),
        compiler_params=_cparams(("arbitrary", "arbitrary")),
        name="moe",
    )(tg, n_active, src, src, dst, dst, gate, x1, w_eg, w_eu, w_ed)


def _combine_kernel(x1_ref, y_ref, g_ref, b_ref, o_ref):
    o_ref[...] = _ln(DEEPNORM_ALPHA * _load_token_tiles(x1_ref) + _load_token_tiles(y_ref), g_ref[...], b_ref[...])


def _combine(x1, y, g2, b2, row0, rows, tm):
    b0 = row0 // tm
    tiles = pl.BlockSpec((tm * SUBLANES, LANES), lambda i: (i + b0, 0))
    full = pl.BlockSpec((1, D_MODEL), lambda i: (0, 0))
    return pl.pallas_call(
        _combine_kernel,
        grid=(rows // tm,),
        in_specs=[tiles, tiles, full, full],
        out_specs=pl.BlockSpec((tm, D_MODEL), lambda i: (i, 0)),
        out_shape=jax.ShapeDtypeStruct((rows, D_MODEL), F32),
        compiler_params=_cparams(("parallel",)),
        name="combine",
    )(x1, y, g2, b2)


def _later_or_same(n):
    r = lax.broadcasted_iota(jnp.int32, (n, n), 0)
    c = lax.broadcasted_iota(jnp.int32, (n, n), 1)
    return (r >= c).astype(BF16)


def kernel(x_prompt, x_sample, mem_prompt, cache_sba_k, cache_sba_v, cache_conv, cache_mem_k, cache_mem_v,
           ln0_g, ln0_b, w_in, w_dw, b_dw, lnc_g, lnc_b, w_cpw, w_mk, w_mv, w_out, ln1_g, ln1_b,
           w_rg, b_rg, w_re, b_re, w_eg, w_eu, w_ed, ln2_g, ln2_b):
    bp, tp, _ = x_prompt.shape
    bs, ts, _ = x_sample.shape
    past = cache_sba_k.shape[2]
    l = 0
    r2 = lambda a: a.reshape(1, -1)

    w_in_b = w_in[l].astype(BF16)
    w_out_b = w_out[l].astype(BF16)
    w_cpw_b = w_cpw[l].astype(BF16)
    wr = jnp.zeros((D_MODEL, LANES), F32)
    wr = wr.at[:, 0:N_EXPERTS].set(w_re[l]).at[:, N_EXPERTS:N_EXPERTS + N_GROUPS].set(w_rg[l])
    wr0 = wr.astype(BF16)
    wr1 = (wr - wr0.astype(F32)).astype(BF16)
    wr3 = jnp.concatenate([wr0, wr1], axis=1)
    br = jnp.zeros((1, LANES), F32)
    br = br.at[0, 0:N_EXPERTS].set(b_re[l]).at[0, N_EXPERTS:N_EXPERTS + N_GROUPS].set(b_rg[l])
    g0, b0 = r2(ln0_g), r2(ln0_b)

    tq = 256
    tk_s = 256
    u_p = _later_or_same(tq)
    u_s = _later_or_same(tk_s)
    u_n = _later_or_same(ts)

    xp2d = x_prompt.reshape(bp * tp, D_MODEL)
    xs2d = x_sample.reshape(bs * ts, D_MODEL)
    wkt = w_in[l][:, SBA_W:2 * SBA_W].T.astype(BF16)
    wvt = w_in[l][:, 2 * SBA_W:3 * SBA_W].T.astype(BF16)
    qp, ktbp, vtbp, ktp, vtp, glup, mqp = _in_proj_t(xp2d, g0, b0, w_in_b, wkt, wvt, bp, tp, 512, tq)
    qs, kbs, vbs, ks, vs, glus, mqs = _in_proj(xs2d, g0, b0, w_in_b, 512)

    sba_p = _sba_prompt(qp, ktbp, vtbp, u_p, bp, tp, tq, 4)
    kc = jnp.transpose(cache_sba_k[l], (0, 2, 3, 1)).reshape(bs, SBA_W, past)
    vc = jnp.transpose(cache_sba_v[l], (0, 2, 3, 1)).reshape(bs, SBA_W, past)
    sba_s = _sba_sample(qs, kbs, vbs, kc, vc, u_s, u_n, bs, ts, tk_s, 8)

    conv_w = (w_dw[l], r2(b_dw[l]), r2(lnc_g[l]), r2(lnc_b[l]), w_cpw_b)
    h0_p = jnp.zeros((bp, HALO, CONV_W), F32)
    h0_s = jnp.pad(cache_conv[l], ((0, 0), (HALO - (CONV_K - 1), 0), (0, 0)))
    conv_p = _conv_module(h0_p, glup, *conv_w, bp, tp, 256)
    conv_s = _conv_module(h0_s, glus, *conv_w, bs, ts, ts)

    mkt, mvt = _mem_kv(mem_prompt.reshape(bp * N_MEM, D_MODEL), w_mk[l].T.astype(BF16),
                       w_mv[l].T.astype(BF16), bp)
    mem_p = _mem_attn(mqp, mkt, mvt, bp, tp, 512)
    mem_s = _mem_attn(mqs, jnp.transpose(cache_mem_k[l], (0, 2, 3, 1)).reshape(bs, MEM_W, N_MEM),
                      jnp.transpose(cache_mem_v[l], (0, 2, 3, 1)).reshape(bs, MEM_W, N_MEM), bs, ts, ts)

    post_w = (g0, b0, w_out_b, r2(ln1_g[l]), r2(ln1_b[l]), wr3, br)
    x1, route = _post(xp2d, xs2d, sba_p, sba_s, conv_p, conv_s, mem_p, mem_s, *post_w, 512)
    f = _moe(x1, _moe_plan(route, route.shape[0]), w_eg[l], w_eu[l], w_ed[l])
    g2, b2 = r2(ln2_g[l]), r2(ln2_b[l])
    yp = _combine(x1, f, g2, b2, 0, bp * tp, 512)
    ys = _combine(x1, f, g2, b2, bp * tp, bs * ts, 512)

    hd = (SBA_HEADS, HEAD_DIM)
    glup3 = glup.reshape(bp, tp, CONV_W)
    glus3 = glus.reshape(bs, ts, CONV_W)
    conv_tail_s = jnp.concatenate([cache_conv[l], glus3], axis=1)[:, -(CONV_K - 1):]
    return (
        yp.reshape(bp, tp, D_MODEL),
        ys.reshape(bs, ts, D_MODEL),
        jnp.transpose(ktp.reshape(bp, *hd, tp), (0, 3, 1, 2))[None],
        jnp.transpose(vtp.reshape(bp, *hd, tp), (0, 3, 1, 2))[None],
        glup3[:, -(CONV_K - 1):][None],
        jnp.transpose(mkt.reshape(bp, MEM_HEADS, HEAD_DIM, N_MEM), (0, 3, 1, 2))[None],
        jnp.transpose(mvt.reshape(bp, MEM_HEADS, HEAD_DIM, N_MEM), (0, 3, 1, 2))[None],
        ks.reshape(1, bs, ts, *hd),
        vs.reshape(1, bs, ts, *hd),
        conv_tail_s[None],
    )
```

```python
import functools

import jax
import jax.numpy as jnp
from jax import lax
from jax.experimental import pallas as pl
from jax.experimental.pallas import tpu as pltpu

F32 = jnp.float32
BF16 = jnp.bfloat16

D_MODEL = 1024
HEAD_DIM = 64
SBA_HEADS = 8
SBA_W = SBA_HEADS * HEAD_DIM
CONV_W = 256
CONV_K = 31
MEM_HEADS = 4
MEM_W = MEM_HEADS * HEAD_DIM
N_MEM = 256
N_GROUPS = 4
EXPERTS_PER_GROUP = 8
N_EXPERTS = N_GROUPS * EXPERTS_PER_GROUP
D_EXPERT = 256
DEPTH = 1
DEEPNORM_ALPHA = (2 * DEPTH) ** 0.25
QK_SCALE = HEAD_DIM ** -0.5
LN_EPS = 1e-5
LOG2E = 1.4426950408889634

LANES = 128
SUBLANES = 8
HALO = 32
MOE_TM = 1024
MOE_UNROLL = 8
SP_LINEAR = 100.0
VMEM_LIMIT = 48 * 1024 * 1024


def _cparams(sem):
    return pltpu.CompilerParams(dimension_semantics=sem, vmem_limit_bytes=VMEM_LIMIT)


def _ln(x, g, b):
    mu = jnp.mean(x, axis=-1, keepdims=True)
    xc = x - mu
    var = jnp.mean(xc * xc, axis=-1, keepdims=True)
    return xc * lax.rsqrt(var + LN_EPS) * g + b


def _dot(a, b):
    return jnp.dot(a, b, preferred_element_type=F32)


def _dot_nt(a, b):
    return lax.dot_general(a, b, (((1,), (1,)), ((), ())), preferred_element_type=F32)


def _in_proj_kernel(x_ref, g_ref, b_ref, w_ref, q_ref, kb_ref, vb_ref, k_ref, v_ref, glu_ref, mq_ref):
    xn = _ln(x_ref[...], g_ref[...], b_ref[...]).astype(BF16)
    q = _dot(xn, w_ref[:, 0:SBA_W])
    q_ref[...] = (q * (QK_SCALE * LOG2E)).astype(BF16)
    k = _dot(xn, w_ref[:, SBA_W:2 * SBA_W])
    k_ref[...] = k
    kb_ref[...] = k.astype(BF16)
    v = _dot(xn, w_ref[:, 2 * SBA_W:3 * SBA_W])
    v_ref[...] = v
    vb_ref[...] = v.astype(BF16)
    c0 = 3 * SBA_W
    cv = _dot(xn, w_ref[:, c0:c0 + CONV_W])
    cg = _dot(xn, w_ref[:, c0 + CONV_W:c0 + 2 * CONV_W])
    glu_ref[...] = cv * jax.nn.sigmoid(cg)
    mq = _dot(xn, w_ref[:, c0 + 2 * CONV_W:c0 + 2 * CONV_W + MEM_W])
    mq_ref[...] = (mq * QK_SCALE).astype(BF16)


def _in_proj_t_kernel(x_ref, g_ref, b_ref, w_ref, wkt_ref, wvt_ref,
                      q_ref, ktb_ref, vtb_ref, kt_ref, vt_ref, glu_ref, mq_ref, *, tq):
    xn = _ln(x_ref[...], g_ref[...], b_ref[...]).astype(BF16)
    tm = xn.shape[0]
    q = _dot(xn, w_ref[:, 0:SBA_W])
    q_ref[...] = (q * (QK_SCALE * LOG2E)).astype(BF16)
    for wt_ref, t_ref, tb_ref in ((wkt_ref, kt_ref, ktb_ref), (wvt_ref, vt_ref, vtb_ref)):
        t = _dot_nt(wt_ref[...], xn)
        t_ref[0] = t
        for i in range(tm // tq):
            tb_ref[0, i] = t[:, i * tq:(i + 1) * tq].astype(BF16)
    c0 = 3 * SBA_W
    cv = _dot(xn, w_ref[:, c0:c0 + CONV_W])
    cg = _dot(xn, w_ref[:, c0 + CONV_W:c0 + 2 * CONV_W])
    glu_ref[...] = cv * jax.nn.sigmoid(cg)
    mq = _dot(xn, w_ref[:, c0 + 2 * CONV_W:c0 + 2 * CONV_W + MEM_W])
    mq_ref[...] = (mq * QK_SCALE).astype(BF16)


def _in_proj_t(x2d, g, b, w_bf16, wkt, wvt, batch, seq, tm, tq):
    n = x2d.shape[0]
    in_w = w_bf16.shape[1]
    nt = seq // tm
    row = lambda w: pl.BlockSpec((tm, w), lambda bi, i: (bi * nt + i, 0))
    full = lambda s: pl.BlockSpec(s, lambda bi, i: (0, 0))
    tr = pl.BlockSpec((1, SBA_W, tm), lambda bi, i: (bi, 0, i))
    trb = pl.BlockSpec((1, tm // tq, SBA_W, tq), lambda bi, i: (bi, i, 0, 0))
    return pl.pallas_call(
        functools.partial(_in_proj_t_kernel, tq=tq),
        grid=(batch, nt),
        in_specs=[row(D_MODEL), full((1, D_MODEL)), full((1, D_MODEL)), full((D_MODEL, in_w)),
                  full((SBA_W, D_MODEL)), full((SBA_W, D_MODEL))],
        out_specs=[row(SBA_W), trb, trb, tr, tr, row(CONV_W), row(MEM_W)],
        out_shape=[
            jax.ShapeDtypeStruct((n, SBA_W), BF16),
            jax.ShapeDtypeStruct((batch, seq // tq, SBA_W, tq), BF16),
            jax.ShapeDtypeStruct((batch, seq // tq, SBA_W, tq), BF16),
            jax.ShapeDtypeStruct((batch, SBA_W, seq), F32),
            jax.ShapeDtypeStruct((batch, SBA_W, seq), F32),
            jax.ShapeDtypeStruct((n, CONV_W), F32),
            jax.ShapeDtypeStruct((n, MEM_W), BF16),
        ],
        compiler_params=_cparams(("parallel", "parallel")),
        name="in_proj_t",
    )(x2d, g, b, w_bf16, wkt, wvt)


def _in_proj(x2d, g, b, w_bf16, tm):
    n = x2d.shape[0]
    in_w = w_bf16.shape[1]
    row = lambda w: pl.BlockSpec((tm, w), lambda i: (i, 0))
    full = lambda s: pl.BlockSpec(s, lambda i: (0, 0))
    return pl.pallas_call(
        _in_proj_kernel,
        grid=(n // tm,),
        in_specs=[row(D_MODEL), full((1, D_MODEL)), full((1, D_MODEL)), full((D_MODEL, in_w))],
        out_specs=[row(SBA_W), row(SBA_W), row(SBA_W), row(SBA_W), row(SBA_W), row(CONV_W), row(MEM_W)],
        out_shape=[
            jax.ShapeDtypeStruct((n, SBA_W), BF16),
            jax.ShapeDtypeStruct((n, SBA_W), BF16),
            jax.ShapeDtypeStruct((n, SBA_W), BF16),
            jax.ShapeDtypeStruct((n, SBA_W), F32),
            jax.ShapeDtypeStruct((n, SBA_W), F32),
            jax.ShapeDtypeStruct((n, CONV_W), F32),
            jax.ShapeDtypeStruct((n, MEM_W), BF16),
        ],
        compiler_params=_cparams(("parallel",)),
        name="in_proj",
    )(x2d, g, b, w_bf16)


def _head_masks(shape):
    lane = lax.broadcasted_iota(jnp.int32, shape, 1)
    return lane < HEAD_DIM


def _stack_heads(q, low):
    zero = jnp.zeros_like(q)
    return jnp.concatenate([jnp.where(low, q, zero), jnp.where(low, zero, q)], axis=0)


def _sba_tile(z, log2_weight, mask):
    sp = jnp.where(z > SP_LINEAR, z, jnp.log(1.0 + jnp.exp2(z)) * LOG2E)
    if mask is not None:
        sp = jnp.where(mask, sp, 0.0)
    w = jnp.exp2(log2_weight(sp.astype(BF16)))
    if mask is not None:
        w = jnp.where(mask, w, 0.0)
    return w.astype(BF16), jnp.sum(sp, axis=1, keepdims=True)


def _sba_prompt_kernel(q_ref, k_ref, v_ref, u_ref, o_ref, *, tq, hp):
    qi = pl.program_id(2)
    low = _head_masks((tq, LANES))
    u = u_ref[...]
    row = lax.broadcasted_iota(jnp.int32, (2 * tq, tq), 0) & (tq - 1)
    col = lax.broadcasted_iota(jnp.int32, (2 * tq, tq), 1)
    qs = [_stack_heads(q_ref[:, p * LANES:(p + 1) * LANES], low) for p in range(hp)]

    def blk(j, carry, mask):
        out = []
        for p in range(hp):
            c, acc = carry[p]
            z = _dot(qs[p], k_ref[0, j, p * LANES:(p + 1) * LANES, :])
            w, rs = _sba_tile(z, lambda sp: (z - c) - _dot(sp, u), mask)
            acc = acc + _dot_nt(w, v_ref[0, j, p * LANES:(p + 1) * LANES, :])
            out.append((c + rs, acc))
        return tuple(out)

    carry = tuple((jnp.zeros((2 * tq, 1), F32), jnp.zeros((2 * tq, LANES), F32)) for _ in range(hp))
    carry = blk(qi, carry, col < row)
    carry = lax.fori_loop(0, qi // 2,
                          lambda t, cr: blk(qi - 2 - 2 * t, blk(qi - 1 - 2 * t, cr, None), None), carry)
    carry = lax.cond(qi % 2 == 1, lambda cr: blk(0, cr, None), lambda cr: cr, carry)
    for p in range(hp):
        acc = carry[p][1]
        o_ref[:, p * LANES:(p + 1) * LANES] = jnp.where(low, acc[0:tq], acc[tq:2 * tq]).astype(o_ref.dtype)


def _sba_prompt(q, kt, vt, u, batch, seq, tq, hp):
    n = q.shape[0]
    nq = seq // tq
    groups = SBA_W // (LANES * hp)
    kv = pl.BlockSpec((1, nq, hp * LANES, tq), lambda b, p, i: (b, 0, p, 0))
    return pl.pallas_call(
        functools.partial(_sba_prompt_kernel, tq=tq, hp=hp),
        grid=(batch, groups, nq),
        in_specs=[
            pl.BlockSpec((tq, hp * LANES), lambda b, p, i: (b * nq + i, p)),
            kv, kv,
            pl.BlockSpec((tq, tq), lambda b, p, i: (0, 0)),
        ],
        out_specs=pl.BlockSpec((tq, hp * LANES), lambda b, p, i: (b * nq + i, p)),
        out_shape=jax.ShapeDtypeStruct((n, SBA_W), BF16),
        compiler_params=_cparams(("parallel", "parallel", "arbitrary")),
        name="sba_prompt",
    )(q, kt, vt, u)


def _sba_sample_kernel(q_ref, kn_ref, vn_ref, kc_ref, vc_ref, u_ref, un_ref, o_ref, qs_ref, c_ref, acc_ref,
                       *, tn, tk, nsub):
    j = pl.program_id(1)
    nj = pl.num_programs(1)
    low = _head_masks((tn, LANES))
    pairs = SBA_W // LANES
    rows = 2 * tn

    def step(logits, weighted, u, mask):
        z = jnp.concatenate([logits(qs_ref[p], p) for p in range(pairs)], axis=0)
        c = c_ref[...]
        w, rs = _sba_tile(z, lambda sp: (z - c) - _dot(sp, u), mask)
        acc_ref[...] += jnp.concatenate([weighted(w[p * rows:(p + 1) * rows], p) for p in range(pairs)], axis=0)
        c_ref[...] = c + rs

    def lanes(p):
        return slice(p * LANES, (p + 1) * LANES)

    @pl.when(j == 0)
    def _():
        for p in range(pairs):
            qs_ref[p] = _stack_heads(q_ref[:, lanes(p)], low)
        c_ref[...] = jnp.zeros_like(c_ref)
        acc_ref[...] = jnp.zeros_like(acc_ref)
        row = lax.broadcasted_iota(jnp.int32, (pairs * rows, tn), 0) & (tn - 1)
        col = lax.broadcasted_iota(jnp.int32, (pairs * rows, tn), 1)
        step(lambda q, p: _dot_nt(q, kn_ref[:, lanes(p)]), lambda w, p: _dot(w, vn_ref[:, lanes(p)]),
             un_ref[...], col < row)

    for s in reversed(range(nsub)):
        keys = slice(s * tk, (s + 1) * tk)
        step(lambda q, p: _dot(q, kc_ref[0, lanes(p), keys].astype(BF16)),
             lambda w, p: _dot_nt(w, vc_ref[0, lanes(p), keys].astype(BF16)),
             u_ref[...], None)

    @pl.when(j == nj - 1)
    def _():
        for p in range(pairs):
            a0 = acc_ref[p * rows:p * rows + tn, :]
            a1 = acc_ref[p * rows + tn:(p + 1) * rows, :]
            o_ref[:, lanes(p)] = jnp.where(low, a0, a1).astype(o_ref.dtype)


def _sba_sample(q, kn, vn, kc, vc, u, un, batch, tn, tk, nsub):
    n = q.shape[0]
    past = kc.shape[2]
    tkb = tk * nsub
    nj = past // tkb
    rows = SBA_HEADS * tn
    cache = pl.BlockSpec((1, SBA_W, tkb), lambda b, j: (b, 0, nj - 1 - j))
    return pl.pallas_call(
        functools.partial(_sba_sample_kernel, tn=tn, tk=tk, nsub=nsub),
        grid=(batch, nj),
        in_specs=[
            pl.BlockSpec((tn, SBA_W), lambda b, j: (b, 0)),
            pl.BlockSpec((tn, SBA_W), lambda b, j: (b, 0)),
            pl.BlockSpec((tn, SBA_W), lambda b, j: (b, 0)),
            cache, cache,
            pl.BlockSpec((tk, tk), lambda b, j: (0, 0)),
            pl.BlockSpec((tn, tn), lambda b, j: (0, 0)),
        ],
        out_specs=pl.BlockSpec((tn, SBA_W), lambda b, j: (b, 0)),
        out_shape=jax.ShapeDtypeStruct((n, SBA_W), BF16),
        scratch_shapes=[pltpu.VMEM((SBA_W // LANES, 2 * tn, LANES), BF16),
                        pltpu.VMEM((rows, 1), F32), pltpu.VMEM((rows, LANES), F32)],
        compiler_params=_cparams(("parallel", "arbitrary")),
        name="sba_sample",
    )(q, kn, vn, kc, vc, u, un)


def _conv_kernel(h0_ref, glu_ref, wdw_ref, bdw_ref, g_ref, b_ref, wpw_ref, o_ref, buf_ref, *, tt, rc):
    ti = pl.program_id(1)
    rows = HALO + tt

    @pl.when(ti == 0)
    def _():
        buf_ref[0, 0:HALO, :] = h0_ref[0]

    @pl.when(ti > 0)
    def _():
        buf_ref[0, 0:HALO, :] = buf_ref[0, tt:tt + HALO, :]

    buf_ref[0, HALO:rows, :] = glu_ref[...]
    buf_ref[0, rows:rows + SUBLANES, :] = jnp.zeros((SUBLANES, CONV_W), F32)
    for s in range(1, SUBLANES):
        buf_ref[s, 0:rows, :] = buf_ref[0, pl.ds(s, rows), :]
    base = HALO - (CONV_K - 1)
    wdw = wdw_ref[...]
    for r0 in range(0, tt, rc):
        acc = jnp.zeros((rc, CONV_W), F32)
        for kk in range(CONV_K):
            off = base + r0 + kk
            acc = acc + buf_ref[off % SUBLANES, pl.ds(off - off % SUBLANES, rc), :] * wdw[kk:kk + 1, :]
        u = _ln(acc + bdw_ref[...], g_ref[...], b_ref[...])
        s = u * jax.nn.sigmoid(u)
        o_ref[pl.ds(r0, rc), :] = _dot(s.astype(BF16), wpw_ref[...]).astype(o_ref.dtype)


def _conv_module(h0, glu, w_dw, b_dw, g, b, w_pw_bf16, batch, seq, tt):
    n = glu.shape[0]
    nt = seq // tt
    rc = min(tt, 64)
    full = lambda s: pl.BlockSpec(s, lambda bi, ti: (0,) * len(s))
    return pl.pallas_call(
        functools.partial(_conv_kernel, tt=tt, rc=rc),
        grid=(batch, nt),
        in_specs=[
            pl.BlockSpec((1, HALO, CONV_W), lambda bi, ti: (bi, 0, 0)),
            pl.BlockSpec((tt, CONV_W), lambda bi, ti: (bi * nt + ti, 0)),
            full((CONV_K, CONV_W)), full((1, CONV_W)), full((1, CONV_W)), full((1, CONV_W)),
            full((CONV_W, CONV_W)),
        ],
        out_specs=pl.BlockSpec((tt, CONV_W), lambda bi, ti: (bi * nt + ti, 0)),
        out_shape=jax.ShapeDtypeStruct((n, CONV_W), BF16),
        scratch_shapes=[pltpu.VMEM((SUBLANES, HALO + tt + SUBLANES, CONV_W), F32)],
        compiler_params=_cparams(("parallel", "arbitrary")),
        name="conv_module",
    )(h0, glu, w_dw, b_dw, g, b, w_pw_bf16)


def _mem_kv_kernel(m_ref, wkt_ref, wvt_ref, k_ref, v_ref):
    m = m_ref[...].astype(BF16)
    k_ref[0] = _dot_nt(wkt_ref[...], m)
    v_ref[0] = _dot_nt(wvt_ref[...], m)


def _mem_kv(mem2d, wkt, wvt, batch):
    full = pl.BlockSpec((MEM_W, D_MODEL), lambda i: (0, 0))
    out = pl.BlockSpec((1, MEM_W, N_MEM), lambda i: (i, 0, 0))
    return pl.pallas_call(
        _mem_kv_kernel,
        grid=(batch,),
        in_specs=[pl.BlockSpec((N_MEM, D_MODEL), lambda i: (i, 0)), full, full],
        out_specs=[out, out],
        out_shape=[jax.ShapeDtypeStruct((batch, MEM_W, N_MEM), F32)] * 2,
        compiler_params=_cparams(("parallel",)),
        name="mem_kv",
    )(mem2d, wkt, wvt)


def _mem_attn_kernel(q_ref, k_ref, v_ref, o_ref):
    q = q_ref[...]
    tq = q.shape[0]
    low = _head_masks((tq, LANES))
    for p in range(MEM_W // LANES):
        qs = _stack_heads(q[:, p * LANES:(p + 1) * LANES], low)
        kt = k_ref[0, p * LANES:(p + 1) * LANES, :].astype(BF16)
        vt = v_ref[0, p * LANES:(p + 1) * LANES, :].astype(BF16)
        s = _dot(qs, kt)
        e = jnp.exp(s - jnp.max(s, axis=-1, keepdims=True))
        o = _dot_nt(e.astype(BF16), vt) / jnp.sum(e, axis=-1, keepdims=True)
        o_ref[:, p * LANES:(p + 1) * LANES] = jnp.where(low, o[0:tq], o[tq:2 * tq]).astype(o_ref.dtype)


def _mem_attn(mq, mkt, mvt, batch, seq, tq):
    n = mq.shape[0]
    nq = seq // tq
    kv = pl.BlockSpec((1, MEM_W, N_MEM), lambda b, i: (b, 0, 0))
    return pl.pallas_call(
        _mem_attn_kernel,
        grid=(batch, nq),
        in_specs=[pl.BlockSpec((tq, MEM_W), lambda b, i: (b * nq + i, 0)), kv, kv],
        out_specs=pl.BlockSpec((tq, MEM_W), lambda b, i: (b * nq + i, 0)),
        out_shape=jax.ShapeDtypeStruct((n, MEM_W), BF16),
        compiler_params=_cparams(("parallel", "parallel")),
        name="mem_attn",
    )(mq, mkt, mvt)


def _store_token_tiles(ref, x):
    rows = x.shape[0]
    for c in range(SUBLANES):
        ref[pl.ds(c, rows, stride=SUBLANES), :] = x[:, c * LANES:(c + 1) * LANES]


def _load_token_tiles(ref):
    rows = ref.shape[0] // SUBLANES
    return jnp.concatenate([ref[pl.ds(c, rows, stride=SUBLANES), :] for c in range(SUBLANES)], axis=1)


def _split2(x):
    a = x.astype(BF16)
    b = (x - a.astype(F32)).astype(BF16)
    return a, b


def _post_body(x_ref, sba_ref, conv_ref, mem_ref, g0_ref, b0_ref, wo_ref, g1_ref, b1_ref,
               wr_ref, br_ref, x1_ref, route_ref):
    xn = _ln(x_ref[...], g0_ref[...], b0_ref[...])
    mix = _dot(sba_ref[...], wo_ref[0:SBA_W, :])
    mix = mix + _dot(conv_ref[...], wo_ref[SBA_W:SBA_W + CONV_W, :])
    mix = mix + _dot(mem_ref[...], wo_ref[SBA_W + CONV_W:SBA_W + CONV_W + MEM_W, :])
    x1 = _ln(DEEPNORM_ALPHA * xn + mix, g1_ref[...], b1_ref[...])
    _store_token_tiles(x1_ref, x1)

    a0, a1 = _split2(x1)
    hi = _dot(a0, wr_ref[...])
    logits = hi[:, 0:LANES] + (hi[:, LANES:2 * LANES] + _dot(a1, wr_ref[:, 0:LANES])) + br_ref[...]
    tm = logits.shape[0]
    lane = lax.broadcasted_iota(jnp.int32, (tm, LANES), 1).astype(F32)
    neg = jnp.float32(-jnp.inf)
    big = jnp.float32(LANES)
    is_g = jnp.logical_and(lane >= N_EXPERTS, lane < N_EXPERTS + N_GROUPS)
    gl = jnp.where(is_g, logits, neg)
    gmax = jnp.max(gl, axis=-1, keepdims=True)
    g_idx = jnp.min(jnp.where(gl == gmax, lane, big), axis=-1, keepdims=True) - N_EXPERTS
    g_w = 1.0 / jnp.sum(jnp.exp(gl - gmax), axis=-1, keepdims=True)
    in_grp = jnp.logical_and(lane >= g_idx * EXPERTS_PER_GROUP, lane < (g_idx + 1.0) * EXPERTS_PER_GROUP)
    el = jnp.where(in_grp, logits, neg)
    v1 = jnp.max(el, axis=-1, keepdims=True)
    i1 = jnp.min(jnp.where(el == v1, lane, big), axis=-1, keepdims=True)
    el2 = jnp.where(lane == i1, neg, el)
    v2 = jnp.max(el2, axis=-1, keepdims=True)
    i2 = jnp.min(jnp.where(el2 == v2, lane, big), axis=-1, keepdims=True)
    e2 = jnp.exp(v2 - v1)
    p1 = 1.0 / (1.0 + e2)
    p2 = e2 / (1.0 + e2)
    route_ref[...] = jnp.where(lane == 0.0, i1, jnp.where(lane == 1.0, i2, jnp.where(
        lane == 2.0, p1 * g_w, jnp.where(lane == 3.0, p2 * g_w, 0.0))))


def _post_kernel(xp_ref, xs_ref, sbap_ref, sbas_ref, convp_ref, convs_ref, memp_ref, mems_ref,
                 g0_ref, b0_ref, wo_ref, g1_ref, b1_ref, wr_ref, br_ref, x1_ref, route_ref, *, prompt_tiles):
    i = pl.program_id(0)
    shared = (g0_ref, b0_ref, wo_ref, g1_ref, b1_ref, wr_ref, br_ref, x1_ref, route_ref)

    @pl.when(i < prompt_tiles)
    def _():
        _post_body(xp_ref, sbap_ref, convp_ref, memp_ref, *shared)

    @pl.when(i >= prompt_tiles)
    def _():
        _post_body(xs_ref, sbas_ref, convs_ref, mems_ref, *shared)


def _post(xp, xs, sbap, sbas, convp, convs, memp, mems, g0, b0, wo_bf16, g1, b1, wr3, br, tm):
    pt = xp.shape[0] // tm
    st = xs.shape[0] // tm
    n = xp.shape[0] + xs.shape[0]
    prow = lambda w: pl.BlockSpec((tm, w), lambda i: (jnp.minimum(i, pt - 1), 0))
    srow = lambda w: pl.BlockSpec((tm, w), lambda i: (jnp.maximum(i - pt, 0), 0))
    full = lambda s: pl.BlockSpec(s, lambda i: (0,) * len(s))
    widths = (D_MODEL, SBA_W, CONV_W, MEM_W)
    return pl.pallas_call(
        functools.partial(_post_kernel, prompt_tiles=pt),
        grid=(pt + st,),
        in_specs=[spec(w) for w in widths for spec in (prow, srow)] + [
            full((1, D_MODEL)), full((1, D_MODEL)), full((D_MODEL, D_MODEL)),
            full((1, D_MODEL)), full((1, D_MODEL)),
            full((D_MODEL, 2 * LANES)), full((1, LANES))],
        out_specs=[pl.BlockSpec((tm * SUBLANES, LANES), lambda i: (i, 0)),
                   pl.BlockSpec((tm, LANES), lambda i: (i, 0))],
        out_shape=[jax.ShapeDtypeStruct((n * SUBLANES, LANES), F32),
                   jax.ShapeDtypeStruct((n, LANES), F32)],
        compiler_params=_cparams(("arbitrary",)),
        name="post",
    )(xp, xs, sbap, sbas, convp, convs, memp, mems, g0, b0, wo_bf16, g1, b1, wr3, br)


def _moe_plan(route, n):
    tiles = n // MOE_TM + N_GROUPS
    e1 = route[:, 0].astype(jnp.int32)
    e2 = route[:, 1].astype(jnp.int32)
    g = e1 // EXPERTS_PER_GROUP
    _, tok_s, l1_s, l2_s, w1_s, w2_s = lax.sort(
        (g, jnp.arange(n, dtype=jnp.int32), e1 % EXPERTS_PER_GROUP, e2 % EXPERTS_PER_GROUP,
         route[:, 2], route[:, 3]), num_keys=1, is_stable=True)
    counts = jnp.sum((g[:, None] == jnp.arange(N_GROUPS, dtype=jnp.int32)[None, :]).astype(jnp.int32), axis=0)
    ntile = (counts + MOE_TM - 1) // MOE_TM
    tile_end = jnp.cumsum(ntile)
    tile_start = tile_end - ntile
    first = jnp.cumsum(counts) - counts
    n_active = tile_end[-1]
    t = jnp.arange(tiles, dtype=jnp.int32)
    tg = jnp.minimum(jnp.sum((t[:, None] >= tile_end[None, :]).astype(jnp.int32), axis=1), N_GROUPS - 1)
    done = (t - tile_start[tg]) * MOE_TM
    n_valid = jnp.where(t < n_active, jnp.clip(counts[tg] - done, 0, MOE_TM), 0)
    r = jnp.arange(MOE_TM, dtype=jnp.int32)
    valid = r[None, :] < n_valid[:, None]
    shift = tile_start * MOE_TM - first
    max_shift = N_GROUPS * MOE_TM
    row_group = jnp.repeat(tg, MOE_TM)

    def runs(a):
        ext = jnp.concatenate([jnp.zeros((max_shift,), a.dtype), a, jnp.zeros((tiles * MOE_TM - n,), a.dtype)])
        out = jnp.zeros((tiles * MOE_TM,), a.dtype)
        for gi in range(N_GROUPS):
            moved = lax.dynamic_slice(ext, (max_shift - shift[gi],), (tiles * MOE_TM,))
            out = jnp.where(row_group == gi, moved, out)
        return out.reshape(tiles, MOE_TM)

    tok = runs(tok_s)
    src = jnp.where(valid, tok, 0)
    dst = jnp.where(valid, tok, n + r[None, :])
    ids = jnp.arange(EXPERTS_PER_GROUP, dtype=jnp.int32)
    gate = (jnp.where(runs(l1_s)[..., None] == ids, runs(w1_s)[..., None], 0.0)
            + jnp.where(runs(l2_s)[..., None] == ids, runs(w2_s)[..., None], 0.0))
    gate = jnp.where(valid[..., None], gate, 0.0).reshape(tiles * MOE_TM, EXPERTS_PER_GROUP)
    as_rows = lambda a: (a * SUBLANES).astype(jnp.int32).reshape(tiles, 1, MOE_TM)
    return tg.astype(jnp.int32), n_active.reshape(1).astype(jnp.int32), as_rows(src), as_rows(dst), gate


def _moe_kernel(tg_ref, na_ref, src0_ref, src1_ref, dst_ref, dstp_ref, gate_ref, x_hbm, wg_ref, wu_ref, wd_ref,
                y_hbm, xg_ref, xb_ref, acc_ref, yo_ref, gsem, ssem):
    t = pl.program_id(0)
    j = pl.program_id(1)
    n_active = na_ref[0]
    tile_rows = MOE_TM * SUBLANES

    def token(ref, row0):
        return ref.at[pl.ds(pl.multiple_of(row0, SUBLANES), SUBLANES)]

    def start_gather(src_ref, slot):
        def body(i, carry):
            for k in range(MOE_UNROLL):
                r = i * MOE_UNROLL + k
                pltpu.make_async_copy(token(x_hbm, src_ref[0, 0, r]), token(xg_ref.at[slot], r * SUBLANES),
                                      gsem.at[slot]).start(priority=k % 2)
            return carry
        lax.fori_loop(0, MOE_TM // MOE_UNROLL, body, 0)

    def wait_gather(slot):
        pltpu.make_async_copy(x_hbm.at[pl.ds(0, tile_rows)], xg_ref.at[slot], gsem.at[slot]).wait()

    def wait_scatter():
        pltpu.make_async_copy(yo_ref, y_hbm.at[pl.ds(0, tile_rows)], ssem.at[0]).wait()

    @pl.when(t < n_active)
    def _():
        slot = lax.rem(t, 2)

        @pl.when(j == 0)
        def _():
            @pl.when(t == 0)
            def _():
                start_gather(src0_ref, 0)
                yo_ref[...] = jnp.zeros_like(yo_ref)
                sink = pltpu.make_async_copy(yo_ref, y_hbm.at[pl.ds(y_hbm.shape[0] - tile_rows, tile_rows)],
                                             ssem.at[0])
                sink.start()
                sink.wait()

            wait_gather(slot)
            xb_ref[...] = _load_token_tiles(xg_ref.at[slot]).astype(BF16)
            acc_ref[...] = jnp.zeros_like(acc_ref)

        x = xb_ref[...]
        a = _dot(x, wg_ref[0].astype(BF16))
        u = _dot(x, wu_ref[0].astype(BF16))
        gate = gate_ref[...]
        lane = lax.broadcasted_iota(jnp.int32, gate.shape, 1)
        ge = jnp.sum(jnp.where(lane == j, gate, 0.0), axis=1, keepdims=True)
        hid = (a * jax.nn.sigmoid(a)) * u * ge
        acc_ref[...] += _dot(hid.astype(BF16), wd_ref[0].astype(BF16))

        share = MOE_TM // EXPERTS_PER_GROUP
        for k in range(share):
            r = j * share + k
            pltpu.make_async_copy(token(x_hbm, src1_ref[0, 0, r]), token(xg_ref.at[1 - slot], r * SUBLANES),
                                  gsem.at[1 - slot]).start(priority=k % 2)
            pltpu.make_async_copy(token(yo_ref, r * SUBLANES), token(y_hbm, dstp_ref[0, 0, r]),
                                  ssem.at[0]).start(priority=(k + 1) % 2)

        @pl.when(j == EXPERTS_PER_GROUP - 1)
        def _():
            wait_scatter()
            _store_token_tiles(yo_ref, acc_ref[...])

            @pl.when(t == n_active - 1)
            def _():
                def body(i, carry):
                    for k in range(MOE_UNROLL):
                        r = i * MOE_UNROLL + k
                        pltpu.make_async_copy(token(yo_ref, r * SUBLANES), token(y_hbm, dst_ref[0, 0, r]),
                                              ssem.at[0]).start(priority=k % 2)
                    return carry
                lax.fori_loop(0, MOE_TM // MOE_UNROLL, body, 0)
                wait_scatter()
                wait_gather(1 - slot)


def _moe(x1, plan, w_eg, w_eu, w_ed):
    tg, n_active, src, dst, gate = plan
    tiles = src.shape[0]
    tile_rows = MOE_TM * SUBLANES
    idx = lambda f: pl.BlockSpec((1, 1, MOE_TM), f, memory_space=pltpu.SMEM)
    expert = lambda a, b: pl.BlockSpec((1, a, b), lambda t, j, tg, na: (tg[t] * EXPERTS_PER_GROUP + j, 0, 0))
    grid_spec = pltpu.PrefetchScalarGridSpec(
        num_scalar_prefetch=2,
        grid=(tiles, EXPERTS_PER_GROUP),
        in_specs=[
            idx(lambda t, j, tg, na: (t, 0, 0)),
            idx(lambda t, j, tg, na: (jnp.minimum(t + 1, tiles - 1), 0, 0)),
            idx(lambda t, j, tg, na: (t, 0, 0)),
            idx(lambda t, j, tg, na: (jnp.maximum(t - 1, 0), 0, 0)),
            pl.BlockSpec((MOE_TM, EXPERTS_PER_GROUP), lambda t, j, tg, na: (t, 0)),
            pl.BlockSpec(memory_space=pl.ANY),
            expert(D_MODEL, D_EXPERT), expert(D_MODEL, D_EXPERT), expert(D_EXPERT, D_MODEL),
        ],
        out_specs=pl.BlockSpec(memory_space=pl.ANY),
        scratch_shapes=[pltpu.VMEM((2, tile_rows, LANES), F32), pltpu.VMEM((MOE_TM, D_MODEL), BF16),
                        pltpu.VMEM((MOE_TM, D_MODEL), F32), pltpu.VMEM((tile_rows, LANES), F32),
                        pltpu.SemaphoreType.DMA((2,)), pltpu.SemaphoreType.DMA((1,))],
    )
    return pl.pallas_call(
        _moe_kernel,
        grid_spec=grid_spec,
        out_shape=jax.ShapeDtypeStruct((x1.shape[0] + tile_rows, LANES), F32),
        compiler_params=_cparams(("arbitrary", "arbitrary")),
        name="moe",
    )(tg, n_active, src, src, dst, dst, gate, x1, w_eg, w_eu, w_ed)


def _combine_kernel(x1_ref, y_ref, g_ref, b_ref, o_ref):
    o_ref[...] = _ln(DEEPNORM_ALPHA * _load_token_tiles(x1_ref) + _load_token_tiles(y_ref), g_ref[...], b_ref[...])


def _combine(x1, y, g2, b2, row0, rows, tm):
    b0 = row0 // tm
    tiles = pl.BlockSpec((tm * SUBLANES, LANES), lambda i: (i + b0, 0))
    full = pl.BlockSpec((1, D_MODEL), lambda i: (0, 0))
    return pl.pallas_call(
        _combine_kernel,
        grid=(rows // tm,),
        in_specs=[tiles, tiles, full, full],
        out_specs=pl.BlockSpec((tm, D_MODEL), lambda i: (i, 0)),
        out_shape=jax.ShapeDtypeStruct((rows, D_MODEL), F32),
        compiler_params=_cparams(("parallel",)),
        name="combine",
    )(x1, y, g2, b2)


def _later_or_same(n):
    r = lax.broadcasted_iota(jnp.int32, (n, n), 0)
    c = lax.broadcasted_iota(jnp.int32, (n, n), 1)
    return (r >= c).astype(BF16)


def kernel(x_prompt, x_sample, mem_prompt, cache_sba_k, cache_sba_v, cache_conv, cache_mem_k, cache_mem_v,
           ln0_g, ln0_b, w_in, w_dw, b_dw, lnc_g, lnc_b, w_cpw, w_mk, w_mv, w_out, ln1_g, ln1_b,
           w_rg, b_rg, w_re, b_re, w_eg, w_eu, w_ed, ln2_g, ln2_b):
    bp, tp, _ = x_prompt.shape
    bs, ts, _ = x_sample.shape
    past = cache_sba_k.shape[2]
    l = 0
    r2 = lambda a: a.reshape(1, -1)

    w_in_b = w_in[l].astype(BF16)
    w_out_b = w_out[l].astype(BF16)
    w_cpw_b = w_cpw[l].astype(BF16)
    wr = jnp.zeros((D_MODEL, LANES), F32)
    wr = wr.at[:, 0:N_EXPERTS].set(w_re[l]).at[:, N_EXPERTS:N_EXPERTS + N_GROUPS].set(w_rg[l])
    wr0 = wr.astype(BF16)
    wr1 = (wr - wr0.astype(F32)).astype(BF16)
    wr3 = jnp.concatenate([wr0, wr1], axis=1)
    br = jnp.zeros((1, LANES), F32)
    br = br.at[0, 0:N_EXPERTS].set(b_re[l]).at[0, N_EXPERTS:N_EXPERTS + N_GROUPS].set(b_rg[l])
    g0, b0 = r2(ln0_g), r2(ln0_b)

    tq = 256
    tk_s = 256
    u_p = _later_or_same(tq)
    u_s = _later_or_same(tk_s)
    u_n = _later_or_same(ts)

    xp2d = x_prompt.reshape(bp * tp, D_MODEL)
    xs2d = x_sample.reshape(bs * ts, D_MODEL)
    wkt = w_in[l][:, SBA_W:2 * SBA_W].T.astype(BF16)
    wvt = w_in[l][:, 2 * SBA_W:3 * SBA_W].T.astype(BF16)
    qp, ktbp, vtbp, ktp, vtp, glup, mqp = _in_proj_t(xp2d, g0, b0, w_in_b, wkt, wvt, bp, tp, 512, tq)
    qs, kbs, vbs, ks, vs, glus, mqs = _in_proj(xs2d, g0, b0, w_in_b, 512)

    sba_p = _sba_prompt(qp, ktbp, vtbp, u_p, bp, tp, tq, 4)
    kc = jnp.transpose(cache_sba_k[l], (0, 2, 3, 1)).reshape(bs, SBA_W, past)
    vc = jnp.transpose(cache_sba_v[l], (0, 2, 3, 1)).reshape(bs, SBA_W, past)
    sba_s = _sba_sample(qs, kbs, vbs, kc, vc, u_s, u_n, bs, ts, tk_s, 8)

    conv_w = (w_dw[l], r2(b_dw[l]), r2(lnc_g[l]), r2(lnc_b[l]), w_cpw_b)
    h0_p = jnp.zeros((bp, HALO, CONV_W), F32)
    h0_s = jnp.pad(cache_conv[l], ((0, 0), (HALO - (CONV_K - 1), 0), (0, 0)))
    conv_p = _conv_module(h0_p, glup, *conv_w, bp, tp, 512)
    conv_s = _conv_module(h0_s, glus, *conv_w, bs, ts, ts)

    mkt, mvt = _mem_kv(mem_prompt.reshape(bp * N_MEM, D_MODEL), w_mk[l].T.astype(BF16),
                       w_mv[l].T.astype(BF16), bp)
    mem_p = _mem_attn(mqp, mkt, mvt, bp, tp, 512)
    mem_s = _mem_attn(mqs, jnp.transpose(cache_mem_k[l], (0, 2, 3, 1)).reshape(bs, MEM_W, N_MEM),
                      jnp.transpose(cache_mem_v[l], (0, 2, 3, 1)).reshape(bs, MEM_W, N_MEM), bs, ts, ts)

    post_w = (g0, b0, w_out_b, r2(ln1_g[l]), r2(ln1_b[l]), wr3, br)
    x1, route = _post(xp2d, xs2d, sba_p, sba_s, conv_p, conv_s, mem_p, mem_s, *post_w, 512)
    f = _moe(x1, _moe_plan(route, route.shape[0]), w_eg[l], w_eu[l], w_ed[l])
    g2, b2 = r2(ln2_g[l]), r2(ln2_b[l])
    yp = _combine(x1, f, g2, b2, 0, bp * tp, 1024)
    ys = _combine(x1, f, g2, b2, bp * tp, bs * ts, 1024)

    hd = (SBA_HEADS, HEAD_DIM)
    glup3 = glup.reshape(bp, tp, CONV_W)
    glus3 = glus.reshape(bs, ts, CONV_W)
    conv_tail_s = jnp.concatenate([cache_conv[l], glus3], axis=1)[:, -(CONV_K - 1):]
    return (
        yp.reshape(bp, tp, D_MODEL),
        ys.reshape(bs, ts, D_MODEL),
        jnp.transpose(ktp.reshape(bp, *hd, tp), (0, 3, 1, 2))[None],
        jnp.transpose(vtp.reshape(bp, *hd, tp), (0, 3, 1, 2))[None],
        glup3[:, -(CONV_K - 1):][None],
        jnp.transpose(mkt.reshape(bp, MEM_HEADS, HEAD_DIM, N_MEM), (0, 3, 1, 2))[None],
        jnp.transpose(mvt.reshape(bp, MEM_HEADS, HEAD_DIM, N_MEM), (0, 3, 1, 2))[None],
        ks.reshape(1, bs, ts, *hd),
        vs.reshape(1, bs, ts, *hd),
        conv_tail_s[None],
    )
```

```python
import functools

import jax
import jax.numpy as jnp
from jax import lax
from jax.experimental import pallas as pl
from jax.experimental.pallas import tpu as pltpu

F32 = jnp.float32
BF16 = jnp.bfloat16

D_MODEL = 1024
HEAD_DIM = 64
SBA_HEADS = 8
SBA_W = SBA_HEADS * HEAD_DIM
CONV_W = 256
CONV_K = 31
MEM_HEADS = 4
MEM_W = MEM_HEADS * HEAD_DIM
N_MEM = 256
N_GROUPS = 4
EXPERTS_PER_GROUP = 8
N_EXPERTS = N_GROUPS * EXPERTS_PER_GROUP
D_EXPERT = 256
DEPTH = 1
DEEPNORM_ALPHA = (2 * DEPTH) ** 0.25
QK_SCALE = HEAD_DIM ** -0.5
LN_EPS = 1e-5
LOG2E = 1.4426950408889634

LANES = 128
SUBLANES = 8
HALO = 32
MOE_TM = 1024
MOE_UNROLL = 8
SP_LINEAR = 100.0
VMEM_LIMIT = 48 * 1024 * 1024


def _cparams(sem):
    return pltpu.CompilerParams(dimension_semantics=sem, vmem_limit_bytes=VMEM_LIMIT)


def _ln(x, g, b):
    mu = jnp.mean(x, axis=-1, keepdims=True)
    xc = x - mu
    var = jnp.mean(xc * xc, axis=-1, keepdims=True)
    return xc * lax.rsqrt(var + LN_EPS) * g + b


def _dot(a, b):
    return jnp.dot(a, b, preferred_element_type=F32)


def _dot_nt(a, b):
    return lax.dot_general(a, b, (((1,), (1,)), ((), ())), preferred_element_type=F32)


def _in_proj_kernel(x_ref, g_ref, b_ref, w_ref, q_ref, kb_ref, vb_ref, k_ref, v_ref, glu_ref, mq_ref):
    xn = _ln(x_ref[...], g_ref[...], b_ref[...]).astype(BF16)
    q = _dot(xn, w_ref[:, 0:SBA_W])
    q_ref[...] = (q * (QK_SCALE * LOG2E)).astype(BF16)
    k = _dot(xn, w_ref[:, SBA_W:2 * SBA_W])
    k_ref[...] = k
    kb_ref[...] = k.astype(BF16)
    v = _dot(xn, w_ref[:, 2 * SBA_W:3 * SBA_W])
    v_ref[...] = v
    vb_ref[...] = v.astype(BF16)
    c0 = 3 * SBA_W
    cv = _dot(xn, w_ref[:, c0:c0 + CONV_W])
    cg = _dot(xn, w_ref[:, c0 + CONV_W:c0 + 2 * CONV_W])
    glu_ref[...] = cv * jax.nn.sigmoid(cg)
    mq = _dot(xn, w_ref[:, c0 + 2 * CONV_W:c0 + 2 * CONV_W + MEM_W])
    mq_ref[...] = (mq * QK_SCALE).astype(BF16)


def _in_proj_t_kernel(x_ref, g_ref, b_ref, w_ref, wkt_ref, wvt_ref,
                      q_ref, ktb_ref, vtb_ref, kt_ref, vt_ref, glu_ref, mq_ref, *, tq):
    xn = _ln(x_ref[...], g_ref[...], b_ref[...]).astype(BF16)
    tm = xn.shape[0]
    q = _dot(xn, w_ref[:, 0:SBA_W])
    q_ref[...] = (q * (QK_SCALE * LOG2E)).astype(BF16)
    for wt_ref, t_ref, tb_ref in ((wkt_ref, kt_ref, ktb_ref), (wvt_ref, vt_ref, vtb_ref)):
        t = _dot_nt(wt_ref[...], xn)
        t_ref[0] = t
        for i in range(tm // tq):
            tb_ref[0, i] = t[:, i * tq:(i + 1) * tq].astype(BF16)
    c0 = 3 * SBA_W
    cv = _dot(xn, w_ref[:, c0:c0 + CONV_W])
    cg = _dot(xn, w_ref[:, c0 + CONV_W:c0 + 2 * CONV_W])
    glu_ref[...] = cv * jax.nn.sigmoid(cg)
    mq = _dot(xn, w_ref[:, c0 + 2 * CONV_W:c0 + 2 * CONV_W + MEM_W])
    mq_ref[...] = (mq * QK_SCALE).astype(BF16)


def _in_proj_t(x2d, g, b, w_bf16, wkt, wvt, batch, seq, tm, tq):
    n = x2d.shape[0]
    in_w = w_bf16.shape[1]
    nt = seq // tm
    row = lambda w: pl.BlockSpec((tm, w), lambda bi, i: (bi * nt + i, 0))
    full = lambda s: pl.BlockSpec(s, lambda bi, i: (0, 0))
    tr = pl.BlockSpec((1, SBA_W, tm), lambda bi, i: (bi, 0, i))
    trb = pl.BlockSpec((1, tm // tq, SBA_W, tq), lambda bi, i: (bi, i, 0, 0))
    return pl.pallas_call(
        functools.partial(_in_proj_t_kernel, tq=tq),
        grid=(batch, nt),
        in_specs=[row(D_MODEL), full((1, D_MODEL)), full((1, D_MODEL)), full((D_MODEL, in_w)),
                  full((SBA_W, D_MODEL)), full((SBA_W, D_MODEL))],
        out_specs=[row(SBA_W), trb, trb, tr, tr, row(CONV_W), row(MEM_W)],
        out_shape=[
            jax.ShapeDtypeStruct((n, SBA_W), BF16),
            jax.ShapeDtypeStruct((batch, seq // tq, SBA_W, tq), BF16),
            jax.ShapeDtypeStruct((batch, seq // tq, SBA_W, tq), BF16),
            jax.ShapeDtypeStruct((batch, SBA_W, seq), F32),
            jax.ShapeDtypeStruct((batch, SBA_W, seq), F32),
            jax.ShapeDtypeStruct((n, CONV_W), F32),
            jax.ShapeDtypeStruct((n, MEM_W), BF16),
        ],
        compiler_params=_cparams(("parallel", "parallel")),
        name="in_proj_t",
    )(x2d, g, b, w_bf16, wkt, wvt)


def _in_proj(x2d, g, b, w_bf16, tm):
    n = x2d.shape[0]
    in_w = w_bf16.shape[1]
    row = lambda w: pl.BlockSpec((tm, w), lambda i: (i, 0))
    full = lambda s: pl.BlockSpec(s, lambda i: (0, 0))
    return pl.pallas_call(
        _in_proj_kernel,
        grid=(n // tm,),
        in_specs=[row(D_MODEL), full((1, D_MODEL)), full((1, D_MODEL)), full((D_MODEL, in_w))],
        out_specs=[row(SBA_W), row(SBA_W), row(SBA_W), row(SBA_W), row(SBA_W), row(CONV_W), row(MEM_W)],
        out_shape=[
            jax.ShapeDtypeStruct((n, SBA_W), BF16),
            jax.ShapeDtypeStruct((n, SBA_W), BF16),
            jax.ShapeDtypeStruct((n, SBA_W), BF16),
            jax.ShapeDtypeStruct((n, SBA_W), F32),
            jax.ShapeDtypeStruct((n, SBA_W), F32),
            jax.ShapeDtypeStruct((n, CONV_W), F32),
            jax.ShapeDtypeStruct((n, MEM_W), BF16),
        ],
        compiler_params=_cparams(("parallel",)),
        name="in_proj",
    )(x2d, g, b, w_bf16)


def _head_masks(shape):
    lane = lax.broadcasted_iota(jnp.int32, shape, 1)
    return lane < HEAD_DIM


def _stack_heads(q, low):
    zero = jnp.zeros_like(q)
    return jnp.concatenate([jnp.where(low, q, zero), jnp.where(low, zero, q)], axis=0)


def _sba_tile(z, log2_weight, mask):
    sp = jnp.where(z > SP_LINEAR, z, jnp.log(1.0 + jnp.exp2(z)) * LOG2E)
    if mask is not None:
        sp = jnp.where(mask, sp, 0.0)
    w = jnp.exp2(log2_weight(sp.astype(BF16)))
    if mask is not None:
        w = jnp.where(mask, w, 0.0)
    return w.astype(BF16), jnp.sum(sp, axis=1, keepdims=True)


def _sba_prompt_kernel(q_ref, k_ref, v_ref, u_ref, o_ref, *, tq, hp):
    qi = pl.program_id(2)
    low = _head_masks((tq, LANES))
    u = u_ref[...]
    row = lax.broadcasted_iota(jnp.int32, (2 * tq, tq), 0) & (tq - 1)
    col = lax.broadcasted_iota(jnp.int32, (2 * tq, tq), 1)
    qs = [_stack_heads(q_ref[:, p * LANES:(p + 1) * LANES], low) for p in range(hp)]

    def blk(j, carry, mask):
        out = []
        for p in range(hp):
            c, acc = carry[p]
            z = _dot(qs[p], k_ref[0, j, p * LANES:(p + 1) * LANES, :])
            w, rs = _sba_tile(z, lambda sp: (z - c) - _dot(sp, u), mask)
            acc = acc + _dot_nt(w, v_ref[0, j, p * LANES:(p + 1) * LANES, :])
            out.append((c + rs, acc))
        return tuple(out)

    carry = tuple((jnp.zeros((2 * tq, 1), F32), jnp.zeros((2 * tq, LANES), F32)) for _ in range(hp))
    carry = blk(qi, carry, col < row)
    carry = lax.fori_loop(0, qi // 2,
                          lambda t, cr: blk(qi - 2 - 2 * t, blk(qi - 1 - 2 * t, cr, None), None), carry)
    carry = lax.cond(qi % 2 == 1, lambda cr: blk(0, cr, None), lambda cr: cr, carry)
    for p in range(hp):
        acc = carry[p][1]
        o_ref[:, p * LANES:(p + 1) * LANES] = jnp.where(low, acc[0:tq], acc[tq:2 * tq]).astype(o_ref.dtype)


def _sba_prompt(q, kt, vt, u, batch, seq, tq, hp):
    n = q.shape[0]
    nq = seq // tq
    groups = SBA_W // (LANES * hp)
    kv = pl.BlockSpec((1, nq, hp * LANES, tq), lambda b, p, i: (b, 0, p, 0))
    return pl.pallas_call(
        functools.partial(_sba_prompt_kernel, tq=tq, hp=hp),
        grid=(batch, groups, nq),
        in_specs=[
            pl.BlockSpec((tq, hp * LANES), lambda b, p, i: (b * nq + i, p)),
            kv, kv,
            pl.BlockSpec((tq, tq), lambda b, p, i: (0, 0)),
        ],
        out_specs=pl.BlockSpec((tq, hp * LANES), lambda b, p, i: (b * nq + i, p)),
        out_shape=jax.ShapeDtypeStruct((n, SBA_W), BF16),
        compiler_params=_cparams(("parallel", "parallel", "arbitrary")),
        name="sba_prompt",
    )(q, kt, vt, u)


def _sba_sample_kernel(q_ref, kn_ref, vn_ref, kc_ref, vc_ref, u_ref, un_ref, o_ref, qs_ref, c_ref, acc_ref,
                       *, tn, tk, nsub):
    j = pl.program_id(1)
    nj = pl.num_programs(1)
    low = _head_masks((tn, LANES))
    pairs = SBA_W // LANES
    rows = 2 * tn

    def step(logits, weighted, u, mask):
        z = jnp.concatenate([logits(qs_ref[p], p) for p in range(pairs)], axis=0)
        c = c_ref[...]
        w, rs = _sba_tile(z, lambda sp: (z - c) - _dot(sp, u), mask)
        acc_ref[...] += jnp.concatenate([weighted(w[p * rows:(p + 1) * rows], p) for p in range(pairs)], axis=0)
        c_ref[...] = c + rs

    def lanes(p):
        return slice(p * LANES, (p + 1) * LANES)

    @pl.when(j == 0)
    def _():
        for p in range(pairs):
            qs_ref[p] = _stack_heads(q_ref[:, lanes(p)], low)
        c_ref[...] = jnp.zeros_like(c_ref)
        acc_ref[...] = jnp.zeros_like(acc_ref)
        row = lax.broadcasted_iota(jnp.int32, (pairs * rows, tn), 0) & (tn - 1)
        col = lax.broadcasted_iota(jnp.int32, (pairs * rows, tn), 1)
        step(lambda q, p: _dot_nt(q, kn_ref[:, lanes(p)]), lambda w, p: _dot(w, vn_ref[:, lanes(p)]),
             un_ref[...], col < row)

    for s in reversed(range(nsub)):
        keys = slice(s * tk, (s + 1) * tk)
        step(lambda q, p: _dot(q, kc_ref[0, lanes(p), keys].astype(BF16)),
             lambda w, p: _dot_nt(w, vc_ref[0, lanes(p), keys].astype(BF16)),
             u_ref[...], None)

    @pl.when(j == nj - 1)
    def _():
        for p in range(pairs):
            a0 = acc_ref[p * rows:p * rows + tn, :]
            a1 = acc_ref[p * rows + tn:(p + 1) * rows, :]
            o_ref[:, lanes(p)] = jnp.where(low, a0, a1).astype(o_ref.dtype)


def _sba_sample(q, kn, vn, kc, vc, u, un, batch, tn, tk, nsub):
    n = q.shape[0]
    past = kc.shape[2]
    tkb = tk * nsub
    nj = past // tkb
    rows = SBA_HEADS * tn
    cache = pl.BlockSpec((1, SBA_W, tkb), lambda b, j: (b, 0, nj - 1 - j))
    return pl.pallas_call(
        functools.partial(_sba_sample_kernel, tn=tn, tk=tk, nsub=nsub),
        grid=(batch, nj),
        in_specs=[
            pl.BlockSpec((tn, SBA_W), lambda b, j: (b, 0)),
            pl.BlockSpec((tn, SBA_W), lambda b, j: (b, 0)),
            pl.BlockSpec((tn, SBA_W), lambda b, j: (b, 0)),
            cache, cache,
            pl.BlockSpec((tk, tk), lambda b, j: (0, 0)),
            pl.BlockSpec((tn, tn), lambda b, j: (0, 0)),
        ],
        out_specs=pl.BlockSpec((tn, SBA_W), lambda b, j: (b, 0)),
        out_shape=jax.ShapeDtypeStruct((n, SBA_W), BF16),
        scratch_shapes=[pltpu.VMEM((SBA_W // LANES, 2 * tn, LANES), BF16),
                        pltpu.VMEM((rows, 1), F32), pltpu.VMEM((rows, LANES), F32)],
        compiler_params=_cparams(("parallel", "arbitrary")),
        name="sba_sample",
    )(q, kn, vn, kc, vc, u, un)


def _conv_kernel(h0_ref, glu_ref, wdw_ref, bdw_ref, g_ref, b_ref, wpw_ref, o_ref, buf_ref, *, tt, rc):
    ti = pl.program_id(1)
    rows = HALO + tt

    @pl.when(ti == 0)
    def _():
        buf_ref[0, 0:HALO, :] = h0_ref[0]

    @pl.when(ti > 0)
    def _():
        buf_ref[0, 0:HALO, :] = buf_ref[0, tt:tt + HALO, :]

    buf_ref[0, HALO:rows, :] = glu_ref[...]
    buf_ref[0, rows:rows + SUBLANES, :] = jnp.zeros((SUBLANES, CONV_W), F32)
    for s in range(1, SUBLANES):
        buf_ref[s, 0:rows, :] = buf_ref[0, pl.ds(s, rows), :]
    base = HALO - (CONV_K - 1)
    wdw = wdw_ref[...]
    for r0 in range(0, tt, rc):
        acc = jnp.zeros((rc, CONV_W), F32)
        for kk in range(CONV_K):
            off = base + r0 + kk
            acc = acc + buf_ref[off % SUBLANES, pl.ds(off - off % SUBLANES, rc), :] * wdw[kk:kk + 1, :]
        u = _ln(acc + bdw_ref[...], g_ref[...], b_ref[...])
        s = u * jax.nn.sigmoid(u)
        o_ref[pl.ds(r0, rc), :] = _dot(s.astype(BF16), wpw_ref[...]).astype(o_ref.dtype)


def _conv_module(h0, glu, w_dw, b_dw, g, b, w_pw_bf16, batch, seq, tt):
    n = glu.shape[0]
    nt = seq // tt
    rc = min(tt, 64)
    full = lambda s: pl.BlockSpec(s, lambda bi, ti: (0,) * len(s))
    return pl.pallas_call(
        functools.partial(_conv_kernel, tt=tt, rc=rc),
        grid=(batch, nt),
        in_specs=[
            pl.BlockSpec((1, HALO, CONV_W), lambda bi, ti: (bi, 0, 0)),
            pl.BlockSpec((tt, CONV_W), lambda bi, ti: (bi * nt + ti, 0)),
            full((CONV_K, CONV_W)), full((1, CONV_W)), full((1, CONV_W)), full((1, CONV_W)),
            full((CONV_W, CONV_W)),
        ],
        out_specs=pl.BlockSpec((tt, CONV_W), lambda bi, ti: (bi * nt + ti, 0)),
        out_shape=jax.ShapeDtypeStruct((n, CONV_W), BF16),
        scratch_shapes=[pltpu.VMEM((SUBLANES, HALO + tt + SUBLANES, CONV_W), F32)],
        compiler_params=_cparams(("parallel", "arbitrary")),
        name="conv_module",
    )(h0, glu, w_dw, b_dw, g, b, w_pw_bf16)


def _mem_kv_kernel(m_ref, wkt_ref, wvt_ref, k_ref, v_ref):
    m = m_ref[...].astype(BF16)
    k_ref[0] = _dot_nt(wkt_ref[...], m)
    v_ref[0] = _dot_nt(wvt_ref[...], m)


def _mem_kv(mem2d, wkt, wvt, batch):
    full = pl.BlockSpec((MEM_W, D_MODEL), lambda i: (0, 0))
    out = pl.BlockSpec((1, MEM_W, N_MEM), lambda i: (i, 0, 0))
    return pl.pallas_call(
        _mem_kv_kernel,
        grid=(batch,),
        in_specs=[pl.BlockSpec((N_MEM, D_MODEL), lambda i: (i, 0)), full, full],
        out_specs=[out, out],
        out_shape=[jax.ShapeDtypeStruct((batch, MEM_W, N_MEM), F32)] * 2,
        compiler_params=_cparams(("parallel",)),
        name="mem_kv",
    )(mem2d, wkt, wvt)


def _mem_attn_kernel(q_ref, k_ref, v_ref, o_ref):
    q = q_ref[...]
    tq = q.shape[0]
    low = _head_masks((tq, LANES))
    for p in range(MEM_W // LANES):
        qs = _stack_heads(q[:, p * LANES:(p + 1) * LANES], low)
        kt = k_ref[0, p * LANES:(p + 1) * LANES, :].astype(BF16)
        vt = v_ref[0, p * LANES:(p + 1) * LANES, :].astype(BF16)
        s = _dot(qs, kt)
        e = jnp.exp(s - jnp.max(s, axis=-1, keepdims=True))
        o = _dot_nt(e.astype(BF16), vt) / jnp.sum(e, axis=-1, keepdims=True)
        o_ref[:, p * LANES:(p + 1) * LANES] = jnp.where(low, o[0:tq], o[tq:2 * tq]).astype(o_ref.dtype)


def _mem_attn(mq, mkt, mvt, batch, seq, tq):
    n = mq.shape[0]
    nq = seq // tq
    kv = pl.BlockSpec((1, MEM_W, N_MEM), lambda b, i: (b, 0, 0))
    return pl.pallas_call(
        _mem_attn_kernel,
        grid=(batch, nq),
        in_specs=[pl.BlockSpec((tq, MEM_W), lambda b, i: (b * nq + i, 0)), kv, kv],
        out_specs=pl.BlockSpec((tq, MEM_W), lambda b, i: (b * nq + i, 0)),
        out_shape=jax.ShapeDtypeStruct((n, MEM_W), BF16),
        compiler_params=_cparams(("parallel", "parallel")),
        name="mem_attn",
    )(mq, mkt, mvt)


def _store_token_tiles(ref, x):
    rows = x.shape[0]
    for c in range(SUBLANES):
        ref[pl.ds(c, rows, stride=SUBLANES), :] = x[:, c * LANES:(c + 1) * LANES]


def _load_token_tiles(ref):
    rows = ref.shape[0] // SUBLANES
    return jnp.concatenate([ref[pl.ds(c, rows, stride=SUBLANES), :] for c in range(SUBLANES)], axis=1)


def _split2(x):
    a = x.astype(BF16)
    b = (x - a.astype(F32)).astype(BF16)
    return a, b


def _post_body(x_ref, sba_ref, conv_ref, mem_ref, g0_ref, b0_ref, wo_ref, g1_ref, b1_ref,
               wr_ref, br_ref, x1_ref, route_ref):
    xn = _ln(x_ref[...], g0_ref[...], b0_ref[...])
    mix = _dot(sba_ref[...], wo_ref[0:SBA_W, :])
    mix = mix + _dot(conv_ref[...], wo_ref[SBA_W:SBA_W + CONV_W, :])
    mix = mix + _dot(mem_ref[...], wo_ref[SBA_W + CONV_W:SBA_W + CONV_W + MEM_W, :])
    x1 = _ln(DEEPNORM_ALPHA * xn + mix, g1_ref[...], b1_ref[...])
    _store_token_tiles(x1_ref, x1)

    a0, a1 = _split2(x1)
    hi = _dot(a0, wr_ref[...])
    logits = hi[:, 0:LANES] + (hi[:, LANES:2 * LANES] + _dot(a1, wr_ref[:, 0:LANES])) + br_ref[...]
    tm = logits.shape[0]
    lane = lax.broadcasted_iota(jnp.int32, (tm, LANES), 1).astype(F32)
    neg = jnp.float32(-jnp.inf)
    big = jnp.float32(LANES)
    is_g = jnp.logical_and(lane >= N_EXPERTS, lane < N_EXPERTS + N_GROUPS)
    gl = jnp.where(is_g, logits, neg)
    gmax = jnp.max(gl, axis=-1, keepdims=True)
    g_idx = jnp.min(jnp.where(gl == gmax, lane, big), axis=-1, keepdims=True) - N_EXPERTS
    g_w = 1.0 / jnp.sum(jnp.exp(gl - gmax), axis=-1, keepdims=True)
    in_grp = jnp.logical_and(lane >= g_idx * EXPERTS_PER_GROUP, lane < (g_idx + 1.0) * EXPERTS_PER_GROUP)
    el = jnp.where(in_grp, logits, neg)
    v1 = jnp.max(el, axis=-1, keepdims=True)
    i1 = jnp.min(jnp.where(el == v1, lane, big), axis=-1, keepdims=True)
    el2 = jnp.where(lane == i1, neg, el)
    v2 = jnp.max(el2, axis=-1, keepdims=True)
    i2 = jnp.min(jnp.where(el2 == v2, lane, big), axis=-1, keepdims=True)
    e2 = jnp.exp(v2 - v1)
    p1 = 1.0 / (1.0 + e2)
    p2 = e2 / (1.0 + e2)
    route_ref[...] = jnp.where(lane == 0.0, i1, jnp.where(lane == 1.0, i2, jnp.where(
        lane == 2.0, p1 * g_w, jnp.where(lane == 3.0, p2 * g_w, 0.0))))


def _post_kernel(xp_ref, xs_ref, sbap_ref, sbas_ref, convp_ref, convs_ref, memp_ref, mems_ref,
                 g0_ref, b0_ref, wo_ref, g1_ref, b1_ref, wr_ref, br_ref, x1_ref, route_ref, *, prompt_tiles):
    i = pl.program_id(0)
    shared = (g0_ref, b0_ref, wo_ref, g1_ref, b1_ref, wr_ref, br_ref, x1_ref, route_ref)

    @pl.when(i < prompt_tiles)
    def _():
        _post_body(xp_ref, sbap_ref, convp_ref, memp_ref, *shared)

    @pl.when(i >= prompt_tiles)
    def _():
        _post_body(xs_ref, sbas_ref, convs_ref, mems_ref, *shared)


def _post(xp, xs, sbap, sbas, convp, convs, memp, mems, g0, b0, wo_bf16, g1, b1, wr3, br, tm):
    pt = xp.shape[0] // tm
    st = xs.shape[0] // tm
    n = xp.shape[0] + xs.shape[0]
    prow = lambda w: pl.BlockSpec((tm, w), lambda i: (jnp.minimum(i, pt - 1), 0))
    srow = lambda w: pl.BlockSpec((tm, w), lambda i: (jnp.maximum(i - pt, 0), 0))
    full = lambda s: pl.BlockSpec(s, lambda i: (0,) * len(s))
    widths = (D_MODEL, SBA_W, CONV_W, MEM_W)
    return pl.pallas_call(
        functools.partial(_post_kernel, prompt_tiles=pt),
        grid=(pt + st,),
        in_specs=[spec(w) for w in widths for spec in (prow, srow)] + [
            full((1, D_MODEL)), full((1, D_MODEL)), full((D_MODEL, D_MODEL)),
            full((1, D_MODEL)), full((1, D_MODEL)),
            full((D_MODEL, 2 * LANES)), full((1, LANES))],
        out_specs=[pl.BlockSpec((tm * SUBLANES, LANES), lambda i: (i, 0)),
                   pl.BlockSpec((tm, LANES), lambda i: (i, 0))],
        out_shape=[jax.ShapeDtypeStruct((n * SUBLANES, LANES), F32),
                   jax.ShapeDtypeStruct((n, LANES), F32)],
        compiler_params=_cparams(("arbitrary",)),
        name="post",
    )(xp, xs, sbap, sbas, convp, convs, memp, mems, g0, b0, wo_bf16, g1, b1, wr3, br)


def _moe_plan(route, n):
    tiles = n // MOE_TM + N_GROUPS
    e1 = route[:, 0].astype(jnp.int32)
    e2 = route[:, 1].astype(jnp.int32)
    g = e1 // EXPERTS_PER_GROUP
    span = 1 << (n - 1).bit_length()
    key_s, ids_s, w1_s, w2_s = lax.sort(
        (g * span + jnp.arange(n, dtype=jnp.int32),
         (e1 % EXPERTS_PER_GROUP) * EXPERTS_PER_GROUP + e2 % EXPERTS_PER_GROUP,
         route[:, 2], route[:, 3]), num_keys=1, is_stable=False)
    tok_s, l1_s, l2_s = key_s % span, ids_s // EXPERTS_PER_GROUP, ids_s % EXPERTS_PER_GROUP
    counts = jnp.sum((g[:, None] == jnp.arange(N_GROUPS, dtype=jnp.int32)[None, :]).astype(jnp.int32), axis=0)
    ntile = (counts + MOE_TM - 1) // MOE_TM
    tile_end = jnp.cumsum(ntile)
    tile_start = tile_end - ntile
    first = jnp.cumsum(counts) - counts
    n_active = tile_end[-1]
    t = jnp.arange(tiles, dtype=jnp.int32)
    tg = jnp.minimum(jnp.sum((t[:, None] >= tile_end[None, :]).astype(jnp.int32), axis=1), N_GROUPS - 1)
    done = (t - tile_start[tg]) * MOE_TM
    n_valid = jnp.where(t < n_active, jnp.clip(counts[tg] - done, 0, MOE_TM), 0)
    r = jnp.arange(MOE_TM, dtype=jnp.int32)
    valid = r[None, :] < n_valid[:, None]
    shift = tile_start * MOE_TM - first
    max_shift = N_GROUPS * MOE_TM
    row_group = jnp.repeat(tg, MOE_TM)

    def runs(a):
        ext = jnp.concatenate([jnp.zeros((max_shift,), a.dtype), a, jnp.zeros((tiles * MOE_TM - n,), a.dtype)])
        out = jnp.zeros((tiles * MOE_TM,), a.dtype)
        for gi in range(N_GROUPS):
            moved = lax.dynamic_slice(ext, (max_shift - shift[gi],), (tiles * MOE_TM,))
            out = jnp.where(row_group == gi, moved, out)
        return out.reshape(tiles, MOE_TM)

    tok = runs(tok_s)
    src = jnp.where(valid, tok, 0)
    dst = jnp.where(valid, tok, n + r[None, :])
    ids = jnp.arange(EXPERTS_PER_GROUP, dtype=jnp.int32)
    gate = (jnp.where(runs(l1_s)[..., None] == ids, runs(w1_s)[..., None], 0.0)
            + jnp.where(runs(l2_s)[..., None] == ids, runs(w2_s)[..., None], 0.0))
    gate = jnp.where(valid[..., None], gate, 0.0).reshape(tiles * MOE_TM, EXPERTS_PER_GROUP)
    as_rows = lambda a: (a * SUBLANES).astype(jnp.int32).reshape(tiles, 1, MOE_TM)
    return tg.astype(jnp.int32), n_active.reshape(1).astype(jnp.int32), as_rows(src), as_rows(dst), gate


def _moe_kernel(tg_ref, na_ref, src0_ref, src1_ref, dst_ref, dstp_ref, gate_ref, x_hbm, wg_ref, wu_ref, wd_ref,
                y_hbm, xg_ref, xb_ref, acc_ref, yo_ref, gsem, ssem):
    t = pl.program_id(0)
    j = pl.program_id(1)
    n_active = na_ref[0]
    tile_rows = MOE_TM * SUBLANES

    def token(ref, row0):
        return ref.at[pl.ds(pl.multiple_of(row0, SUBLANES), SUBLANES)]

    def start_gather(src_ref, slot):
        def body(i, carry):
            for k in range(MOE_UNROLL):
                r = i * MOE_UNROLL + k
                pltpu.make_async_copy(token(x_hbm, src_ref[0, 0, r]), token(xg_ref.at[slot], r * SUBLANES),
                                      gsem.at[slot]).start(priority=k % 2)
            return carry
        lax.fori_loop(0, MOE_TM // MOE_UNROLL, body, 0)

    def wait_gather(slot):
        pltpu.make_async_copy(x_hbm.at[pl.ds(0, tile_rows)], xg_ref.at[slot], gsem.at[slot]).wait()

    def wait_scatter():
        pltpu.make_async_copy(yo_ref, y_hbm.at[pl.ds(0, tile_rows)], ssem.at[0]).wait()

    @pl.when(t < n_active)
    def _():
        slot = lax.rem(t, 2)

        @pl.when(j == 0)
        def _():
            @pl.when(t == 0)
            def _():
                start_gather(src0_ref, 0)
                yo_ref[...] = jnp.zeros_like(yo_ref)
                sink = pltpu.make_async_copy(yo_ref, y_hbm.at[pl.ds(y_hbm.shape[0] - tile_rows, tile_rows)],
                                             ssem.at[0])
                sink.start()
                sink.wait()

            wait_gather(slot)
            xb_ref[...] = _load_token_tiles(xg_ref.at[slot]).astype(BF16)
            acc_ref[...] = jnp.zeros_like(acc_ref)

        x = xb_ref[...]
        a = _dot(x, wg_ref[0].astype(BF16))
        u = _dot(x, wu_ref[0].astype(BF16))
        gate = gate_ref[...]
        lane = lax.broadcasted_iota(jnp.int32, gate.shape, 1)
        ge = jnp.sum(jnp.where(lane == j, gate, 0.0), axis=1, keepdims=True)
        hid = (a * jax.nn.sigmoid(a)) * u * ge
        acc_ref[...] += _dot(hid.astype(BF16), wd_ref[0].astype(BF16))

        share = MOE_TM // EXPERTS_PER_GROUP
        for k in range(share):
            r = j * share + k
            pltpu.make_async_copy(token(x_hbm, src1_ref[0, 0, r]), token(xg_ref.at[1 - slot], r * SUBLANES),
                                  gsem.at[1 - slot]).start(priority=k % 2)
            pltpu.make_async_copy(token(yo_ref, r * SUBLANES), token(y_hbm, dstp_ref[0, 0, r]),
                                  ssem.at[0]).start(priority=(k + 1) % 2)

        @pl.when(j == EXPERTS_PER_GROUP - 1)
        def _():
            wait_scatter()
            _store_token_tiles(yo_ref, acc_ref[...])

            @pl.when(t == n_active - 1)
            def _():
                def body(i, carry):
                    for k in range(MOE_UNROLL):
                        r = i * MOE_UNROLL + k
                        pltpu.make_async_copy(token(yo_ref, r * SUBLANES), token(y_hbm, dst_ref[0, 0, r]),
                                              ssem.at[0]).start(priority=k % 2)
                    return carry
                lax.fori_loop(0, MOE_TM // MOE_UNROLL, body, 0)
                wait_scatter()
                wait_gather(1 - slot)


def _moe(x1, plan, w_eg, w_eu, w_ed):
    tg, n_active, src, dst, gate = plan
    tiles = src.shape[0]
    tile_rows = MOE_TM * SUBLANES
    idx = lambda f: pl.BlockSpec((1, 1, MOE_TM), f, memory_space=pltpu.SMEM)
    expert = lambda a, b: pl.BlockSpec((1, a, b), lambda t, j, tg, na: (tg[t] * EXPERTS_PER_GROUP + j, 0, 0))
    grid_spec = pltpu.PrefetchScalarGridSpec(
        num_scalar_prefetch=2,
        grid=(tiles, EXPERTS_PER_GROUP),
        in_specs=[
            idx(lambda t, j, tg, na: (t, 0, 0)),
            idx(lambda t, j, tg, na: (jnp.minimum(t + 1, tiles - 1), 0, 0)),
            idx(lambda t, j, tg, na: (t, 0, 0)),
            idx(lambda t, j, tg, na: (jnp.maximum(t - 1, 0), 0, 0)),
            pl.BlockSpec((MOE_TM, EXPERTS_PER_GROUP), lambda t, j, tg, na: (t, 0)),
            pl.BlockSpec(memory_space=pl.ANY),
            expert(D_MODEL, D_EXPERT), expert(D_MODEL, D_EXPERT), expert(D_EXPERT, D_MODEL),
        ],
        out_specs=pl.BlockSpec(memory_space=pl.ANY),
        scratch_shapes=[pltpu.VMEM((2, tile_rows, LANES), F32), pltpu.VMEM((MOE_TM, D_MODEL), BF16),
                        pltpu.VMEM((MOE_TM, D_MODEL), F32), pltpu.VMEM((tile_rows, LANES), F32),
                        pltpu.SemaphoreType.DMA((2,)), pltpu.SemaphoreType.DMA((1,))],
    )
    return pl.pallas_call(
        _moe_kernel,
        grid_spec=grid_spec,
        out_shape=jax.ShapeDtypeStruct((x1.shape[0] + tile_rows, LANES), F32),
        compiler_params=_cparams(("arbitrary", "arbitrary")),
        name="moe",
    )(tg, n_active, src, src, dst, dst, gate, x1, w_eg, w_eu, w_ed)


def _combine_kernel(x1_ref, y_ref, g_ref, b_ref, o_ref):
    o_ref[...] = _ln(DEEPNORM_ALPHA * _load_token_tiles(x1_ref) + _load_token_tiles(y_ref), g_ref[...], b_ref[...])


def _combine(x1, y, g2, b2, row0, rows, tm):
    b0 = row0 // tm
    tiles = pl.BlockSpec((tm * SUBLANES, LANES), lambda i: (i + b0, 0))
    full = pl.BlockSpec((1, D_MODEL), lambda i: (0, 0))
    return pl.pallas_call(
        _combine_kernel,
        grid=(rows // tm,),
        in_specs=[tiles, tiles, full, full],
        out_specs=pl.BlockSpec((tm, D_MODEL), lambda i: (i, 0)),
        out_shape=jax.ShapeDtypeStruct((rows, D_MODEL), F32),
        compiler_params=_cparams(("parallel",)),
        name="combine",
    )(x1, y, g2, b2)


def _later_or_same(n):
    r = lax.broadcasted_iota(jnp.int32, (n, n), 0)
    c = lax.broadcasted_iota(jnp.int32, (n, n), 1)
    return (r >= c).astype(BF16)


def kernel(x_prompt, x_sample, mem_prompt, cache_sba_k, cache_sba_v, cache_conv, cache_mem_k, cache_mem_v,
           ln0_g, ln0_b, w_in, w_dw, b_dw, lnc_g, lnc_b, w_cpw, w_mk, w_mv, w_out, ln1_g, ln1_b,
           w_rg, b_rg, w_re, b_re, w_eg, w_eu, w_ed, ln2_g, ln2_b):
    bp, tp, _ = x_prompt.shape
    bs, ts, _ = x_sample.shape
    past = cache_sba_k.shape[2]
    l = 0
    r2 = lambda a: a.reshape(1, -1)

    w_in_b = w_in[l].astype(BF16)
    w_out_b = w_out[l].astype(BF16)
    w_cpw_b = w_cpw[l].astype(BF16)
    wr = jnp.zeros((D_MODEL, LANES), F32)
    wr = wr.at[:, 0:N_EXPERTS].set(w_re[l]).at[:, N_EXPERTS:N_EXPERTS + N_GROUPS].set(w_rg[l])
    wr0 = wr.astype(BF16)
    wr1 = (wr - wr0.astype(F32)).astype(BF16)
    wr3 = jnp.concatenate([wr0, wr1], axis=1)
    br = jnp.zeros((1, LANES), F32)
    br = br.at[0, 0:N_EXPERTS].set(b_re[l]).at[0, N_EXPERTS:N_EXPERTS + N_GROUPS].set(b_rg[l])
    g0, b0 = r2(ln0_g), r2(ln0_b)

    tq = 256
    tk_s = 256
    u_p = _later_or_same(tq)
    u_s = _later_or_same(tk_s)
    u_n = _later_or_same(ts)

    xp2d = x_prompt.reshape(bp * tp, D_MODEL)
    xs2d = x_sample.reshape(bs * ts, D_MODEL)
    wkt = w_in[l][:, SBA_W:2 * SBA_W].T.astype(BF16)
    wvt = w_in[l][:, 2 * SBA_W:3 * SBA_W].T.astype(BF16)
    qp, ktbp, vtbp, ktp, vtp, glup, mqp = _in_proj_t(xp2d, g0, b0, w_in_b, wkt, wvt, bp, tp, 512, tq)
    qs, kbs, vbs, ks, vs, glus, mqs = _in_proj(xs2d, g0, b0, w_in_b, 512)

    sba_p = _sba_prompt(qp, ktbp, vtbp, u_p, bp, tp, tq, 4)
    kc = jnp.transpose(cache_sba_k[l], (0, 2, 3, 1)).reshape(bs, SBA_W, past)
    vc = jnp.transpose(cache_sba_v[l], (0, 2, 3, 1)).reshape(bs, SBA_W, past)
    sba_s = _sba_sample(qs, kbs, vbs, kc, vc, u_s, u_n, bs, ts, tk_s, 8)

    conv_w = (w_dw[l], r2(b_dw[l]), r2(lnc_g[l]), r2(lnc_b[l]), w_cpw_b)
    h0_p = jnp.zeros((bp, HALO, CONV_W), F32)
    h0_s = jnp.pad(cache_conv[l], ((0, 0), (HALO - (CONV_K - 1), 0), (0, 0)))
    conv_p = _conv_module(h0_p, glup, *conv_w, bp, tp, 512)
    conv_s = _conv_module(h0_s, glus, *conv_w, bs, ts, ts)

    mkt, mvt = _mem_kv(mem_prompt.reshape(bp * N_MEM, D_MODEL), w_mk[l].T.astype(BF16),
                       w_mv[l].T.astype(BF16), bp)
    mem_p = _mem_attn(mqp, mkt, mvt, bp, tp, 512)
    mem_s = _mem_attn(mqs, jnp.transpose(cache_mem_k[l], (0, 2, 3, 1)).reshape(bs, MEM_W, N_MEM),
                      jnp.transpose(cache_mem_v[l], (0, 2, 3, 1)).reshape(bs, MEM_W, N_MEM), bs, ts, ts)

    post_w = (g0, b0, w_out_b, r2(ln1_g[l]), r2(ln1_b[l]), wr3, br)
    x1, route = _post(xp2d, xs2d, sba_p, sba_s, conv_p, conv_s, mem_p, mem_s, *post_w, 512)
    f = _moe(x1, _moe_plan(route, route.shape[0]), w_eg[l], w_eu[l], w_ed[l])
    g2, b2 = r2(ln2_g[l]), r2(ln2_b[l])
    yp = _combine(x1, f, g2, b2, 0, bp * tp, 1024)
    ys = _combine(x1, f, g2, b2, bp * tp, bs * ts, 1024)

    hd = (SBA_HEADS, HEAD_DIM)
    glup3 = glup.reshape(bp, tp, CONV_W)
    glus3 = glus.reshape(bs, ts, CONV_W)
    conv_tail_s = jnp.concatenate([cache_conv[l], glus3], axis=1)[:, -(CONV_K - 1):]
    return (
        yp.reshape(bp, tp, D_MODEL),
        ys.reshape(bs, ts, D_MODEL),
        jnp.transpose(ktp.reshape(bp, *hd, tp), (0, 3, 1, 2))[None],
        jnp.transpose(vtp.reshape(bp, *hd, tp), (0, 3, 1, 2))[None],
        glup3[:, -(CONV_K - 1):][None],
        jnp.transpose(mkt.reshape(bp, MEM_HEADS, HEAD_DIM, N_MEM), (0, 3, 1, 2))[None],
        jnp.transpose(mvt.reshape(bp, MEM_HEADS, HEAD_DIM, N_MEM), (0, 3, 1, 2))[None],
        ks.reshape(1, bs, ts, *hd),
        vs.reshape(1, bs, ts, *hd),
        conv_tail_s[None],
    )
```

```python
import functools

import jax
import jax.numpy as jnp
from jax import lax
from jax.experimental import pallas as pl
from jax.experimental.pallas import tpu as pltpu

F32 = jnp.float32
BF16 = jnp.bfloat16

D_MODEL = 1024
HEAD_DIM = 64
SBA_HEADS = 8
SBA_W = SBA_HEADS * HEAD_DIM
CONV_W = 256
CONV_K = 31
MEM_HEADS = 4
MEM_W = MEM_HEADS * HEAD_DIM
N_MEM = 256
N_GROUPS = 4
EXPERTS_PER_GROUP = 8
N_EXPERTS = N_GROUPS * EXPERTS_PER_GROUP
D_EXPERT = 256
DEPTH = 1
DEEPNORM_ALPHA = (2 * DEPTH) ** 0.25
QK_SCALE = HEAD_DIM ** -0.5
LN_EPS = 1e-5
LOG2E = 1.4426950408889634

LANES = 128
SUBLANES = 8
HALO = 32
MOE_TM = 1024
MOE_UNROLL = 8
SP_LINEAR = 100.0
VMEM_LIMIT = 48 * 1024 * 1024


def _cparams(sem):
    return pltpu.CompilerParams(dimension_semantics=sem, vmem_limit_bytes=VMEM_LIMIT)


def _ln(x, g, b):
    mu = jnp.mean(x, axis=-1, keepdims=True)
    xc = x - mu
    var = jnp.mean(xc * xc, axis=-1, keepdims=True)
    return xc * lax.rsqrt(var + LN_EPS) * g + b


def _dot(a, b):
    return jnp.dot(a, b, preferred_element_type=F32)


def _dot_nt(a, b):
    return lax.dot_general(a, b, (((1,), (1,)), ((), ())), preferred_element_type=F32)


def _in_proj_kernel(x_ref, g_ref, b_ref, w_ref, q_ref, kb_ref, vb_ref, k_ref, v_ref, glu_ref, mq_ref):
    xn = _ln(x_ref[...], g_ref[...], b_ref[...]).astype(BF16)
    q = _dot(xn, w_ref[:, 0:SBA_W])
    q_ref[...] = (q * (QK_SCALE * LOG2E)).astype(BF16)
    k = _dot(xn, w_ref[:, SBA_W:2 * SBA_W])
    k_ref[...] = k
    kb_ref[...] = k.astype(BF16)
    v = _dot(xn, w_ref[:, 2 * SBA_W:3 * SBA_W])
    v_ref[...] = v
    vb_ref[...] = v.astype(BF16)
    c0 = 3 * SBA_W
    cv = _dot(xn, w_ref[:, c0:c0 + CONV_W])
    cg = _dot(xn, w_ref[:, c0 + CONV_W:c0 + 2 * CONV_W])
    glu_ref[...] = cv * jax.nn.sigmoid(cg)
    mq = _dot(xn, w_ref[:, c0 + 2 * CONV_W:c0 + 2 * CONV_W + MEM_W])
    mq_ref[...] = (mq * QK_SCALE).astype(BF16)


def _in_proj_t_kernel(x_ref, g_ref, b_ref, w_ref, wkt_ref, wvt_ref,
                      q_ref, ktb_ref, vtb_ref, kt_ref, vt_ref, glu_ref, mq_ref, *, tq):
    xn = _ln(x_ref[...], g_ref[...], b_ref[...]).astype(BF16)
    tm = xn.shape[0]
    q = _dot(xn, w_ref[:, 0:SBA_W])
    q_ref[...] = (q * (QK_SCALE * LOG2E)).astype(BF16)
    for wt_ref, t_ref, tb_ref in ((wkt_ref, kt_ref, ktb_ref), (wvt_ref, vt_ref, vtb_ref)):
        t = _dot_nt(wt_ref[...], xn)
        t_ref[0] = t
        for i in range(tm // tq):
            tb_ref[0, i] = t[:, i * tq:(i + 1) * tq].astype(BF16)
    c0 = 3 * SBA_W
    cv = _dot(xn, w_ref[:, c0:c0 + CONV_W])
    cg = _dot(xn, w_ref[:, c0 + CONV_W:c0 + 2 * CONV_W])
    glu_ref[...] = cv * jax.nn.sigmoid(cg)
    mq = _dot(xn, w_ref[:, c0 + 2 * CONV_W:c0 + 2 * CONV_W + MEM_W])
    mq_ref[...] = (mq * QK_SCALE).astype(BF16)


def _in_proj_t(x2d, g, b, w_bf16, wkt, wvt, batch, seq, tm, tq):
    n = x2d.shape[0]
    in_w = w_bf16.shape[1]
    nt = seq // tm
    row = lambda w: pl.BlockSpec((tm, w), lambda bi, i: (bi * nt + i, 0))
    full = lambda s: pl.BlockSpec(s, lambda bi, i: (0, 0))
    tr = pl.BlockSpec((1, SBA_W, tm), lambda bi, i: (bi, 0, i))
    trb = pl.BlockSpec((1, tm // tq, SBA_W, tq), lambda bi, i: (bi, i, 0, 0))
    return pl.pallas_call(
        functools.partial(_in_proj_t_kernel, tq=tq),
        grid=(batch, nt),
        in_specs=[row(D_MODEL), full((1, D_MODEL)), full((1, D_MODEL)), full((D_MODEL, in_w)),
                  full((SBA_W, D_MODEL)), full((SBA_W, D_MODEL))],
        out_specs=[row(SBA_W), trb, trb, tr, tr, row(CONV_W), row(MEM_W)],
        out_shape=[
            jax.ShapeDtypeStruct((n, SBA_W), BF16),
            jax.ShapeDtypeStruct((batch, seq // tq, SBA_W, tq), BF16),
            jax.ShapeDtypeStruct((batch, seq // tq, SBA_W, tq), BF16),
            jax.ShapeDtypeStruct((batch, SBA_W, seq), F32),
            jax.ShapeDtypeStruct((batch, SBA_W, seq), F32),
            jax.ShapeDtypeStruct((n, CONV_W), F32),
            jax.ShapeDtypeStruct((n, MEM_W), BF16),
        ],
        compiler_params=_cparams(("parallel", "parallel")),
        name="in_proj_t",
    )(x2d, g, b, w_bf16, wkt, wvt)


def _in_proj(x2d, g, b, w_bf16, tm):
    n = x2d.shape[0]
    in_w = w_bf16.shape[1]
    row = lambda w: pl.BlockSpec((tm, w), lambda i: (i, 0))
    full = lambda s: pl.BlockSpec(s, lambda i: (0, 0))
    return pl.pallas_call(
        _in_proj_kernel,
        grid=(n // tm,),
        in_specs=[row(D_MODEL), full((1, D_MODEL)), full((1, D_MODEL)), full((D_MODEL, in_w))],
        out_specs=[row(SBA_W), row(SBA_W), row(SBA_W), row(SBA_W), row(SBA_W), row(CONV_W), row(MEM_W)],
        out_shape=[
            jax.ShapeDtypeStruct((n, SBA_W), BF16),
            jax.ShapeDtypeStruct((n, SBA_W), BF16),
            jax.ShapeDtypeStruct((n, SBA_W), BF16),
            jax.ShapeDtypeStruct((n, SBA_W), F32),
            jax.ShapeDtypeStruct((n, SBA_W), F32),
            jax.ShapeDtypeStruct((n, CONV_W), F32),
            jax.ShapeDtypeStruct((n, MEM_W), BF16),
        ],
        compiler_params=_cparams(("parallel",)),
        name="in_proj",
    )(x2d, g, b, w_bf16)


def _head_masks(shape):
    lane = lax.broadcasted_iota(jnp.int32, shape, 1)
    return lane < HEAD_DIM


def _stack_heads(q, low):
    zero = jnp.zeros_like(q)
    return jnp.concatenate([jnp.where(low, q, zero), jnp.where(low, zero, q)], axis=0)


def _sba_tile(z, log2_weight, mask):
    sp = jnp.where(z > SP_LINEAR, z, jnp.log(1.0 + jnp.exp2(z)) * LOG2E)
    if mask is not None:
        sp = jnp.where(mask, sp, 0.0)
    w = jnp.exp2(log2_weight(sp.astype(BF16)))
    if mask is not None:
        w = jnp.where(mask, w, 0.0)
    return w.astype(BF16), jnp.sum(sp, axis=1, keepdims=True)


def _sba_prompt_kernel(q_ref, k_ref, v_ref, u_ref, o_ref, *, tq, hp):
    qi = pl.program_id(2)
    low = _head_masks((tq, LANES))
    u = u_ref[...]
    row = lax.broadcasted_iota(jnp.int32, (2 * tq, tq), 0) & (tq - 1)
    col = lax.broadcasted_iota(jnp.int32, (2 * tq, tq), 1)
    qs = [_stack_heads(q_ref[:, p * LANES:(p + 1) * LANES], low) for p in range(hp)]

    def blk(j, carry, mask):
        out = []
        for p in range(hp):
            c, acc = carry[p]
            z = _dot(qs[p], k_ref[0, j, p * LANES:(p + 1) * LANES, :])
            w, rs = _sba_tile(z, lambda sp: (z - c) - _dot(sp, u), mask)
            acc = acc + _dot_nt(w, v_ref[0, j, p * LANES:(p + 1) * LANES, :])
            out.append((c + rs, acc))
        return tuple(out)

    carry = tuple((jnp.zeros((2 * tq, 1), F32), jnp.zeros((2 * tq, LANES), F32)) for _ in range(hp))
    carry = blk(qi, carry, col < row)
    carry = lax.fori_loop(0, qi // 2,
                          lambda t, cr: blk(qi - 2 - 2 * t, blk(qi - 1 - 2 * t, cr, None), None), carry)
    carry = lax.cond(qi % 2 == 1, lambda cr: blk(0, cr, None), lambda cr: cr, carry)
    for p in range(hp):
        acc = carry[p][1]
        o_ref[:, p * LANES:(p + 1) * LANES] = jnp.where(low, acc[0:tq], acc[tq:2 * tq]).astype(o_ref.dtype)


def _sba_prompt(q, kt, vt, u, batch, seq, tq, hp):
    n = q.shape[0]
    nq = seq // tq
    groups = SBA_W // (LANES * hp)
    kv = pl.BlockSpec((1, nq, hp * LANES, tq), lambda b, p, i: (b, 0, p, 0))
    return pl.pallas_call(
        functools.partial(_sba_prompt_kernel, tq=tq, hp=hp),
        grid=(batch, groups, nq),
        in_specs=[
            pl.BlockSpec((tq, hp * LANES), lambda b, p, i: (b * nq + i, p)),
            kv, kv,
            pl.BlockSpec((tq, tq), lambda b, p, i: (0, 0)),
        ],
        out_specs=pl.BlockSpec((tq, hp * LANES), lambda b, p, i: (b * nq + i, p)),
        out_shape=jax.ShapeDtypeStruct((n, SBA_W), BF16),
        compiler_params=_cparams(("parallel", "parallel", "arbitrary")),
        name="sba_prompt",
    )(q, kt, vt, u)


def _sba_sample_kernel(q_ref, kn_ref, vn_ref, kc_ref, vc_ref, u_ref, un_ref, o_ref, qs_ref, c_ref, acc_ref,
                       *, tn, tk, nsub):
    j = pl.program_id(1)
    nj = pl.num_programs(1)
    low = _head_masks((tn, LANES))
    pairs = SBA_W // LANES
    rows = 2 * tn

    def step(logits, weighted, u, mask):
        z = jnp.concatenate([logits(qs_ref[p], p) for p in range(pairs)], axis=0)
        c = c_ref[...]
        w, rs = _sba_tile(z, lambda sp: (z - c) - _dot(sp, u), mask)
        acc_ref[...] += jnp.concatenate([weighted(w[p * rows:(p + 1) * rows], p) for p in range(pairs)], axis=0)
        c_ref[...] = c + rs

    def lanes(p):
        return slice(p * LANES, (p + 1) * LANES)

    @pl.when(j == 0)
    def _():
        for p in range(pairs):
            qs_ref[p] = _stack_heads(q_ref[:, lanes(p)], low)
        c_ref[...] = jnp.zeros_like(c_ref)
        acc_ref[...] = jnp.zeros_like(acc_ref)
        row = lax.broadcasted_iota(jnp.int32, (pairs * rows, tn), 0) & (tn - 1)
        col = lax.broadcasted_iota(jnp.int32, (pairs * rows, tn), 1)
        step(lambda q, p: _dot_nt(q, kn_ref[:, lanes(p)]), lambda w, p: _dot(w, vn_ref[:, lanes(p)]),
             un_ref[...], col < row)

    for s in reversed(range(nsub)):
        keys = slice(s * tk, (s + 1) * tk)
        step(lambda q, p: _dot(q, kc_ref[0, lanes(p), keys].astype(BF16)),
             lambda w, p: _dot_nt(w, vc_ref[0, lanes(p), keys].astype(BF16)),
             u_ref[...], None)

    @pl.when(j == nj - 1)
    def _():
        for p in range(pairs):
            a0 = acc_ref[p * rows:p * rows + tn, :]
            a1 = acc_ref[p * rows + tn:(p + 1) * rows, :]
            o_ref[:, lanes(p)] = jnp.where(low, a0, a1).astype(o_ref.dtype)


def _sba_sample(q, kn, vn, kc, vc, u, un, batch, tn, tk, nsub):
    n = q.shape[0]
    past = kc.shape[2]
    tkb = tk * nsub
    nj = past // tkb
    rows = SBA_HEADS * tn
    cache = pl.BlockSpec((1, SBA_W, tkb), lambda b, j: (b, 0, nj - 1 - j))
    return pl.pallas_call(
        functools.partial(_sba_sample_kernel, tn=tn, tk=tk, nsub=nsub),
        grid=(batch, nj),
        in_specs=[
            pl.BlockSpec((tn, SBA_W), lambda b, j: (b, 0)),
            pl.BlockSpec((tn, SBA_W), lambda b, j: (b, 0)),
            pl.BlockSpec((tn, SBA_W), lambda b, j: (b, 0)),
            cache, cache,
            pl.BlockSpec((tk, tk), lambda b, j: (0, 0)),
            pl.BlockSpec((tn, tn), lambda b, j: (0, 0)),
        ],
        out_specs=pl.BlockSpec((tn, SBA_W), lambda b, j: (b, 0)),
        out_shape=jax.ShapeDtypeStruct((n, SBA_W), BF16),
        scratch_shapes=[pltpu.VMEM((SBA_W // LANES, 2 * tn, LANES), BF16),
                        pltpu.VMEM((rows, 1), F32), pltpu.VMEM((rows, LANES), F32)],
        compiler_params=_cparams(("parallel", "arbitrary")),
        name="sba_sample",
    )(q, kn, vn, kc, vc, u, un)


def _conv_kernel(h0_ref, glu_ref, wdw_ref, bdw_ref, g_ref, b_ref, wpw_ref, o_ref, buf_ref, *, tt, rc):
    ti = pl.program_id(1)
    rows = HALO + tt

    @pl.when(ti == 0)
    def _():
        buf_ref[0, 0:HALO, :] = h0_ref[0]

    @pl.when(ti > 0)
    def _():
        buf_ref[0, 0:HALO, :] = buf_ref[0, tt:tt + HALO, :]

    buf_ref[0, HALO:rows, :] = glu_ref[...]
    buf_ref[0, rows:rows + SUBLANES, :] = jnp.zeros((SUBLANES, CONV_W), F32)
    for s in range(1, SUBLANES):
        buf_ref[s, 0:rows, :] = buf_ref[0, pl.ds(s, rows), :]
    base = HALO - (CONV_K - 1)
    wdw = wdw_ref[...]
    for r0 in range(0, tt, rc):
        acc = jnp.zeros((rc, CONV_W), F32)
        for kk in range(CONV_K):
            off = base + r0 + kk
            acc = acc + buf_ref[off % SUBLANES, pl.ds(off - off % SUBLANES, rc), :] * wdw[kk:kk + 1, :]
        u = _ln(acc + bdw_ref[...], g_ref[...], b_ref[...])
        s = u * jax.nn.sigmoid(u)
        o_ref[pl.ds(r0, rc), :] = _dot(s.astype(BF16), wpw_ref[...]).astype(o_ref.dtype)


def _conv_module(h0, glu, w_dw, b_dw, g, b, w_pw_bf16, batch, seq, tt):
    n = glu.shape[0]
    nt = seq // tt
    rc = min(tt, 64)
    full = lambda s: pl.BlockSpec(s, lambda bi, ti: (0,) * len(s))
    return pl.pallas_call(
        functools.partial(_conv_kernel, tt=tt, rc=rc),
        grid=(batch, nt),
        in_specs=[
            pl.BlockSpec((1, HALO, CONV_W), lambda bi, ti: (bi, 0, 0)),
            pl.BlockSpec((tt, CONV_W), lambda bi, ti: (bi * nt + ti, 0)),
            full((CONV_K, CONV_W)), full((1, CONV_W)), full((1, CONV_W)), full((1, CONV_W)),
            full((CONV_W, CONV_W)),
        ],
        out_specs=pl.BlockSpec((tt, CONV_W), lambda bi, ti: (bi * nt + ti, 0)),
        out_shape=jax.ShapeDtypeStruct((n, CONV_W), BF16),
        scratch_shapes=[pltpu.VMEM((SUBLANES, HALO + tt + SUBLANES, CONV_W), F32)],
        compiler_params=_cparams(("parallel", "arbitrary")),
        name="conv_module",
    )(h0, glu, w_dw, b_dw, g, b, w_pw_bf16)


def _mem_kv_kernel(m_ref, wkt_ref, wvt_ref, k_ref, v_ref):
    m = m_ref[...].astype(BF16)
    k_ref[0] = _dot_nt(wkt_ref[...], m)
    v_ref[0] = _dot_nt(wvt_ref[...], m)


def _mem_kv(mem2d, wkt, wvt, batch):
    full = pl.BlockSpec((MEM_W, D_MODEL), lambda i: (0, 0))
    out = pl.BlockSpec((1, MEM_W, N_MEM), lambda i: (i, 0, 0))
    return pl.pallas_call(
        _mem_kv_kernel,
        grid=(batch,),
        in_specs=[pl.BlockSpec((N_MEM, D_MODEL), lambda i: (i, 0)), full, full],
        out_specs=[out, out],
        out_shape=[jax.ShapeDtypeStruct((batch, MEM_W, N_MEM), F32)] * 2,
        compiler_params=_cparams(("parallel",)),
        name="mem_kv",
    )(mem2d, wkt, wvt)


def _mem_attn_kernel(q_ref, k_ref, v_ref, o_ref):
    q = q_ref[...]
    tq = q.shape[0]
    low = _head_masks((tq, LANES))
    for p in range(MEM_W // LANES):
        qs = _stack_heads(q[:, p * LANES:(p + 1) * LANES], low)
        kt = k_ref[0, p * LANES:(p + 1) * LANES, :].astype(BF16)
        vt = v_ref[0, p * LANES:(p + 1) * LANES, :].astype(BF16)
        s = _dot(qs, kt)
        e = jnp.exp(s - jnp.max(s, axis=-1, keepdims=True))
        o = _dot_nt(e.astype(BF16), vt) / jnp.sum(e, axis=-1, keepdims=True)
        o_ref[:, p * LANES:(p + 1) * LANES] = jnp.where(low, o[0:tq], o[tq:2 * tq]).astype(o_ref.dtype)


def _mem_attn(mq, mkt, mvt, batch, seq, tq):
    n = mq.shape[0]
    nq = seq // tq
    kv = pl.BlockSpec((1, MEM_W, N_MEM), lambda b, i: (b, 0, 0))
    return pl.pallas_call(
        _mem_attn_kernel,
        grid=(batch, nq),
        in_specs=[pl.BlockSpec((tq, MEM_W), lambda b, i: (b * nq + i, 0)), kv, kv],
        out_specs=pl.BlockSpec((tq, MEM_W), lambda b, i: (b * nq + i, 0)),
        out_shape=jax.ShapeDtypeStruct((n, MEM_W), BF16),
        compiler_params=_cparams(("parallel", "parallel")),
        name="mem_attn",
    )(mq, mkt, mvt)


def _store_token_tiles(ref, x):
    rows = x.shape[0]
    for c in range(SUBLANES):
        ref[pl.ds(c, rows, stride=SUBLANES), :] = x[:, c * LANES:(c + 1) * LANES]


def _load_token_tiles(ref):
    rows = ref.shape[0] // SUBLANES
    return jnp.concatenate([ref[pl.ds(c, rows, stride=SUBLANES), :] for c in range(SUBLANES)], axis=1)


def _split2(x):
    a = x.astype(BF16)
    b = (x - a.astype(F32)).astype(BF16)
    return a, b


def _post_body(x_ref, sba_ref, conv_ref, mem_ref, g0_ref, b0_ref, wo_ref, g1_ref, b1_ref,
               wr_ref, br_ref, x1_ref, route_ref):
    xn = _ln(x_ref[...], g0_ref[...], b0_ref[...])
    mix = _dot(sba_ref[...], wo_ref[0:SBA_W, :])
    mix = mix + _dot(conv_ref[...], wo_ref[SBA_W:SBA_W + CONV_W, :])
    mix = mix + _dot(mem_ref[...], wo_ref[SBA_W + CONV_W:SBA_W + CONV_W + MEM_W, :])
    x1 = _ln(DEEPNORM_ALPHA * xn + mix, g1_ref[...], b1_ref[...])
    _store_token_tiles(x1_ref, x1)

    a0, a1 = _split2(x1)
    hi = _dot(a0, wr_ref[...])
    logits = hi[:, 0:LANES] + (hi[:, LANES:2 * LANES] + _dot(a1, wr_ref[:, 0:LANES])) + br_ref[...]
    tm = logits.shape[0]
    lane = lax.broadcasted_iota(jnp.int32, (tm, LANES), 1).astype(F32)
    neg = jnp.float32(-jnp.inf)
    big = jnp.float32(LANES)
    is_g = jnp.logical_and(lane >= N_EXPERTS, lane < N_EXPERTS + N_GROUPS)
    gl = jnp.where(is_g, logits, neg)
    gmax = jnp.max(gl, axis=-1, keepdims=True)
    g_idx = jnp.min(jnp.where(gl == gmax, lane, big), axis=-1, keepdims=True) - N_EXPERTS
    g_w = 1.0 / jnp.sum(jnp.exp(gl - gmax), axis=-1, keepdims=True)
    in_grp = jnp.logical_and(lane >= g_idx * EXPERTS_PER_GROUP, lane < (g_idx + 1.0) * EXPERTS_PER_GROUP)
    el = jnp.where(in_grp, logits, neg)
    v1 = jnp.max(el, axis=-1, keepdims=True)
    i1 = jnp.min(jnp.where(el == v1, lane, big), axis=-1, keepdims=True)
    el2 = jnp.where(lane == i1, neg, el)
    v2 = jnp.max(el2, axis=-1, keepdims=True)
    i2 = jnp.min(jnp.where(el2 == v2, lane, big), axis=-1, keepdims=True)
    e2 = jnp.exp(v2 - v1)
    p1 = 1.0 / (1.0 + e2)
    p2 = e2 / (1.0 + e2)
    route_ref[...] = jnp.where(lane == 0.0, i1, jnp.where(lane == 1.0, i2, jnp.where(
        lane == 2.0, p1 * g_w, jnp.where(lane == 3.0, p2 * g_w, 0.0))))


def _post_kernel(xp_ref, xs_ref, sbap_ref, sbas_ref, convp_ref, convs_ref, memp_ref, mems_ref,
                 g0_ref, b0_ref, wo_ref, g1_ref, b1_ref, wr_ref, br_ref, x1_ref, route_ref, *, prompt_tiles):
    i = pl.program_id(0)
    shared = (g0_ref, b0_ref, wo_ref, g1_ref, b1_ref, wr_ref, br_ref, x1_ref, route_ref)

    @pl.when(i < prompt_tiles)
    def _():
        _post_body(xp_ref, sbap_ref, convp_ref, memp_ref, *shared)

    @pl.when(i >= prompt_tiles)
    def _():
        _post_body(xs_ref, sbas_ref, convs_ref, mems_ref, *shared)


def _post(xp, xs, sbap, sbas, convp, convs, memp, mems, g0, b0, wo_bf16, g1, b1, wr3, br, tm):
    pt = xp.shape[0] // tm
    st = xs.shape[0] // tm
    n = xp.shape[0] + xs.shape[0]
    prow = lambda w: pl.BlockSpec((tm, w), lambda i: (jnp.minimum(i, pt - 1), 0))
    srow = lambda w: pl.BlockSpec((tm, w), lambda i: (jnp.maximum(i - pt, 0), 0))
    full = lambda s: pl.BlockSpec(s, lambda i: (0,) * len(s))
    widths = (D_MODEL, SBA_W, CONV_W, MEM_W)
    return pl.pallas_call(
        functools.partial(_post_kernel, prompt_tiles=pt),
        grid=(pt + st,),
        in_specs=[spec(w) for w in widths for spec in (prow, srow)] + [
            full((1, D_MODEL)), full((1, D_MODEL)), full((D_MODEL, D_MODEL)),
            full((1, D_MODEL)), full((1, D_MODEL)),
            full((D_MODEL, 2 * LANES)), full((1, LANES))],
        out_specs=[pl.BlockSpec((tm * SUBLANES, LANES), lambda i: (i, 0)),
                   pl.BlockSpec((tm, LANES), lambda i: (i, 0))],
        out_shape=[jax.ShapeDtypeStruct((n * SUBLANES, LANES), F32),
                   jax.ShapeDtypeStruct((n, LANES), F32)],
        compiler_params=_cparams(("arbitrary",)),
        name="post",
    )(xp, xs, sbap, sbas, convp, convs, memp, mems, g0, b0, wo_bf16, g1, b1, wr3, br)


def _moe_plan(route, n):
    tiles = n // MOE_TM + N_GROUPS
    e1 = route[:, 0].astype(jnp.int32)
    e2 = route[:, 1].astype(jnp.int32)
    g = e1 // EXPERTS_PER_GROUP
    span = 1 << (n - 1).bit_length()
    key_s, ids_s, w1_s, w2_s = lax.sort(
        (g * span + jnp.arange(n, dtype=jnp.int32),
         (e1 % EXPERTS_PER_GROUP) * EXPERTS_PER_GROUP + e2 % EXPERTS_PER_GROUP,
         route[:, 2], route[:, 3]), num_keys=1, is_stable=False)
    tok_s, l1_s, l2_s = key_s % span, ids_s // EXPERTS_PER_GROUP, ids_s % EXPERTS_PER_GROUP
    counts = jnp.sum((g[:, None] == jnp.arange(N_GROUPS, dtype=jnp.int32)[None, :]).astype(jnp.int32), axis=0)
    ntile = (counts + MOE_TM - 1) // MOE_TM
    tile_end = jnp.cumsum(ntile)
    tile_start = tile_end - ntile
    first = jnp.cumsum(counts) - counts
    n_active = tile_end[-1]
    t = jnp.arange(tiles, dtype=jnp.int32)
    tg = jnp.minimum(jnp.sum((t[:, None] >= tile_end[None, :]).astype(jnp.int32), axis=1), N_GROUPS - 1)
    done = (t - tile_start[tg]) * MOE_TM
    n_valid = jnp.where(t < n_active, jnp.clip(counts[tg] - done, 0, MOE_TM), 0)
    r = jnp.arange(MOE_TM, dtype=jnp.int32)
    valid = r[None, :] < n_valid[:, None]
    shift = tile_start * MOE_TM - first
    max_shift = N_GROUPS * MOE_TM
    row_group = jnp.repeat(tg, MOE_TM)

    def runs(a):
        ext = jnp.concatenate([jnp.zeros((max_shift,), a.dtype), a, jnp.zeros((tiles * MOE_TM - n,), a.dtype)])
        out = jnp.zeros((tiles * MOE_TM,), a.dtype)
        for gi in range(N_GROUPS):
            moved = lax.dynamic_slice(ext, (max_shift - shift[gi],), (tiles * MOE_TM,))
            out = jnp.where(row_group == gi, moved, out)
        return out.reshape(tiles, MOE_TM)

    tok = runs(tok_s)
    src = jnp.where(valid, tok, 0)
    dst = jnp.where(valid, tok, n + r[None, :])
    ids = jnp.arange(EXPERTS_PER_GROUP, dtype=jnp.int32)
    gate = (jnp.where(runs(l1_s)[..., None] == ids, runs(w1_s)[..., None], 0.0)
            + jnp.where(runs(l2_s)[..., None] == ids, runs(w2_s)[..., None], 0.0))
    gate = jnp.where(valid[..., None], gate, 0.0).reshape(tiles * MOE_TM, EXPERTS_PER_GROUP)
    as_rows = lambda a: (a * SUBLANES).astype(jnp.int32).reshape(tiles, 1, MOE_TM)
    return tg.astype(jnp.int32), n_active.reshape(1).astype(jnp.int32), as_rows(src), as_rows(dst), gate


def _moe_kernel(tg_ref, na_ref, src0_ref, src1_ref, dst_ref, dstp_ref, gate_ref, x_hbm, wg_ref, wu_ref, wd_ref,
                y_hbm, xg_ref, xb_ref, acc_ref, yo_ref, gsem, ssem):
    t = pl.program_id(0)
    j = pl.program_id(1)
    n_active = na_ref[0]
    tile_rows = MOE_TM * SUBLANES

    def token(ref, row0):
        return ref.at[pl.ds(pl.multiple_of(row0, SUBLANES), SUBLANES)]

    def start_gather(src_ref, slot):
        def body(i, carry):
            for k in range(MOE_UNROLL):
                r = i * MOE_UNROLL + k
                pltpu.make_async_copy(token(x_hbm, src_ref[0, 0, r]), token(xg_ref.at[slot], r * SUBLANES),
                                      gsem.at[slot]).start(priority=k % 2)
            return carry
        lax.fori_loop(0, MOE_TM // MOE_UNROLL, body, 0)

    def wait_gather(slot):
        pltpu.make_async_copy(x_hbm.at[pl.ds(0, tile_rows)], xg_ref.at[slot], gsem.at[slot]).wait()

    def wait_scatter():
        pltpu.make_async_copy(yo_ref, y_hbm.at[pl.ds(0, tile_rows)], ssem.at[0]).wait()

    @pl.when(t < n_active)
    def _():
        slot = lax.rem(t, 2)

        @pl.when(j == 0)
        def _():
            @pl.when(t == 0)
            def _():
                start_gather(src0_ref, 0)
                yo_ref[...] = jnp.zeros_like(yo_ref)
                sink = pltpu.make_async_copy(yo_ref, y_hbm.at[pl.ds(y_hbm.shape[0] - tile_rows, tile_rows)],
                                             ssem.at[0])
                sink.start()
                sink.wait()

            wait_gather(slot)
            xb_ref[...] = _load_token_tiles(xg_ref.at[slot]).astype(BF16)
            acc_ref[...] = jnp.zeros_like(acc_ref)

        x = xb_ref[...]
        a = _dot(x, wg_ref[0].astype(BF16))
        u = _dot(x, wu_ref[0].astype(BF16))
        gate = gate_ref[...]
        lane = lax.broadcasted_iota(jnp.int32, gate.shape, 1)
        ge = jnp.sum(jnp.where(lane == j, gate, 0.0), axis=1, keepdims=True)
        hid = (a * jax.nn.sigmoid(a)) * u * ge
        acc_ref[...] += _dot(hid.astype(BF16), wd_ref[0].astype(BF16))

        share = MOE_TM // EXPERTS_PER_GROUP
        for k in range(share):
            r = j * share + k
            pltpu.make_async_copy(token(x_hbm, src1_ref[0, 0, r]), token(xg_ref.at[1 - slot], r * SUBLANES),
                                  gsem.at[1 - slot]).start(priority=k % 2)
            pltpu.make_async_copy(token(yo_ref, r * SUBLANES), token(y_hbm, dstp_ref[0, 0, r]),
                                  ssem.at[0]).start(priority=(k + 1) % 2)

        @pl.when(j == EXPERTS_PER_GROUP - 1)
        def _():
            wait_scatter()
            _store_token_tiles(yo_ref, acc_ref[...])

            @pl.when(t == n_active - 1)
            def _():
                def body(i, carry):
                    for k in range(MOE_UNROLL):
                        r = i * MOE_UNROLL + k
                        pltpu.make_async_copy(token(yo_ref, r * SUBLANES), token(y_hbm, dst_ref[0, 0, r]),
                                              ssem.at[0]).start(priority=k % 2)
                    return carry
                lax.fori_loop(0, MOE_TM // MOE_UNROLL, body, 0)
                wait_scatter()
                wait_gather(1 - slot)


def _moe(x1, plan, w_eg, w_eu, w_ed):
    tg, n_active, src, dst, gate = plan
    tiles = src.shape[0]
    tile_rows = MOE_TM * SUBLANES
    idx = lambda f: pl.BlockSpec((1, 1, MOE_TM), f, memory_space=pltpu.SMEM)
    expert = lambda a, b: pl.BlockSpec((1, a, b), lambda t, j, tg, na: (tg[t] * EXPERTS_PER_GROUP + j, 0, 0))
    grid_spec = pltpu.PrefetchScalarGridSpec(
        num_scalar_prefetch=2,
        grid=(tiles, EXPERTS_PER_GROUP),
        in_specs=[
            idx(lambda t, j, tg, na: (t, 0, 0)),
            idx(lambda t, j, tg, na: (jnp.minimum(t + 1, tiles - 1), 0, 0)),
            idx(lambda t, j, tg, na: (t, 0, 0)),
            idx(lambda t, j, tg, na: (jnp.maximum(t - 1, 0), 0, 0)),
            pl.BlockSpec((MOE_TM, EXPERTS_PER_GROUP), lambda t, j, tg, na: (t, 0)),
            pl.BlockSpec(memory_space=pl.ANY),
            expert(D_MODEL, D_EXPERT), expert(D_MODEL, D_EXPERT), expert(D_EXPERT, D_MODEL),
        ],
        out_specs=pl.BlockSpec(memory_space=pl.ANY),
        scratch_shapes=[pltpu.VMEM((2, tile_rows, LANES), F32), pltpu.VMEM((MOE_TM, D_MODEL), BF16),
                        pltpu.VMEM((MOE_TM, D_MODEL), F32), pltpu.VMEM((tile_rows, LANES), F32),
                        pltpu.SemaphoreType.DMA((2,)), pltpu.SemaphoreType.DMA((1,))],
    )
    return pl.pallas_call(
        _moe_kernel,
        grid_spec=grid_spec,
        out_shape=jax.ShapeDtypeStruct((x1.shape[0] + tile_rows, LANES), F32),
        compiler_params=_cparams(("arbitrary", "arbitrary")),
        name="moe",
    )(tg, n_active, src, src, dst, dst, gate, x1, w_eg, w_eu, w_ed)


def _combine_kernel(x1_ref, y_ref, g_ref, b_ref, o_ref):
    o_ref[...] = _ln(DEEPNORM_ALPHA * _load_token_tiles(x1_ref) + _load_token_tiles(y_ref), g_ref[...], b_ref[...])


def _combine(x1, y, g2, b2, row0, rows, tm):
    b0 = row0 // tm
    tiles = pl.BlockSpec((tm * SUBLANES, LANES), lambda i: (i + b0, 0))
    full = pl.BlockSpec((1, D_MODEL), lambda i: (0, 0))
    return pl.pallas_call(
        _combine_kernel,
        grid=(rows // tm,),
        in_specs=[tiles, tiles, full, full],
        out_specs=pl.BlockSpec((tm, D_MODEL), lambda i: (i, 0)),
        out_shape=jax.ShapeDtypeStruct((rows, D_MODEL), F32),
        compiler_params=_cparams(("parallel",)),
        name="combine",
    )(x1, y, g2, b2)


def _later_or_same(n):
    r = lax.broadcasted_iota(jnp.int32, (n, n), 0)
    c = lax.broadcasted_iota(jnp.int32, (n, n), 1)
    return (r >= c).astype(BF16)


def kernel(x_prompt, x_sample, mem_prompt, cache_sba_k, cache_sba_v, cache_conv, cache_mem_k, cache_mem_v,
           ln0_g, ln0_b, w_in, w_dw, b_dw, lnc_g, lnc_b, w_cpw, w_mk, w_mv, w_out, ln1_g, ln1_b,
           w_rg, b_rg, w_re, b_re, w_eg, w_eu, w_ed, ln2_g, ln2_b):
    bp, tp, _ = x_prompt.shape
    bs, ts, _ = x_sample.shape
    past = cache_sba_k.shape[2]
    l = 0
    r2 = lambda a: a.reshape(1, -1)

    w_in_b = w_in[l].astype(BF16)
    w_out_b = w_out[l].astype(BF16)
    w_cpw_b = w_cpw[l].astype(BF16)
    wr = jnp.zeros((D_MODEL, LANES), F32)
    wr = wr.at[:, 0:N_EXPERTS].set(w_re[l]).at[:, N_EXPERTS:N_EXPERTS + N_GROUPS].set(w_rg[l])
    wr0 = wr.astype(BF16)
    wr1 = (wr - wr0.astype(F32)).astype(BF16)
    wr3 = jnp.concatenate([wr0, wr1], axis=1)
    br = jnp.zeros((1, LANES), F32)
    br = br.at[0, 0:N_EXPERTS].set(b_re[l]).at[0, N_EXPERTS:N_EXPERTS + N_GROUPS].set(b_rg[l])
    g0, b0 = r2(ln0_g), r2(ln0_b)

    tq = 256
    tk_s = 256
    u_p = _later_or_same(tq)
    u_s = _later_or_same(tk_s)
    u_n = _later_or_same(ts)

    xp2d = x_prompt.reshape(bp * tp, D_MODEL)
    xs2d = x_sample.reshape(bs * ts, D_MODEL)
    wkt = w_in[l][:, SBA_W:2 * SBA_W].T.astype(BF16)
    wvt = w_in[l][:, 2 * SBA_W:3 * SBA_W].T.astype(BF16)
    qp, ktbp, vtbp, ktp, vtp, glup, mqp = _in_proj_t(xp2d, g0, b0, w_in_b, wkt, wvt, bp, tp, 512, tq)
    qs, kbs, vbs, ks, vs, glus, mqs = _in_proj(xs2d, g0, b0, w_in_b, 512)

    sba_p = _sba_prompt(qp, ktbp, vtbp, u_p, bp, tp, tq, 4)
    kc = jnp.transpose(cache_sba_k[l], (0, 2, 3, 1)).reshape(bs, SBA_W, past)
    vc = jnp.transpose(cache_sba_v[l], (0, 2, 3, 1)).reshape(bs, SBA_W, past)
    sba_s = _sba_sample(qs, kbs, vbs, kc, vc, u_s, u_n, bs, ts, tk_s, 16)

    conv_w = (w_dw[l], r2(b_dw[l]), r2(lnc_g[l]), r2(lnc_b[l]), w_cpw_b)
    h0_p = jnp.zeros((bp, HALO, CONV_W), F32)
    h0_s = jnp.pad(cache_conv[l], ((0, 0), (HALO - (CONV_K - 1), 0), (0, 0)))
    conv_p = _conv_module(h0_p, glup, *conv_w, bp, tp, 512)
    conv_s = _conv_module(h0_s, glus, *conv_w, bs, ts, ts)

    mkt, mvt = _mem_kv(mem_prompt.reshape(bp * N_MEM, D_MODEL), w_mk[l].T.astype(BF16),
                       w_mv[l].T.astype(BF16), bp)
    mem_p = _mem_attn(mqp, mkt, mvt, bp, tp, 512)
    mem_s = _mem_attn(mqs, jnp.transpose(cache_mem_k[l], (0, 2, 3, 1)).reshape(bs, MEM_W, N_MEM),
                      jnp.transpose(cache_mem_v[l], (0, 2, 3, 1)).reshape(bs, MEM_W, N_MEM), bs, ts, ts)

    post_w = (g0, b0, w_out_b, r2(ln1_g[l]), r2(ln1_b[l]), wr3, br)
    x1, route = _post(xp2d, xs2d, sba_p, sba_s, conv_p, conv_s, mem_p, mem_s, *post_w, 512)
    f = _moe(x1, _moe_plan(route, route.shape[0]), w_eg[l], w_eu[l], w_ed[l])
    g2, b2 = r2(ln2_g[l]), r2(ln2_b[l])
    yp = _combine(x1, f, g2, b2, 0, bp * tp, 1024)
    ys = _combine(x1, f, g2, b2, bp * tp, bs * ts, 1024)

    hd = (SBA_HEADS, HEAD_DIM)
    glup3 = glup.reshape(bp, tp, CONV_W)
    glus3 = glus.reshape(bs, ts, CONV_W)
    conv_tail_s = jnp.concatenate([cache_conv[l], glus3], axis=1)[:, -(CONV_K - 1):]
    return (
        yp.reshape(bp, tp, D_MODEL),
        ys.reshape(bs, ts, D_MODEL),
        jnp.transpose(ktp.reshape(bp, *hd, tp), (0, 3, 1, 2))[None],
        jnp.transpose(vtp.reshape(bp, *hd, tp), (0, 3, 1, 2))[None],
        glup3[:, -(CONV_K - 1):][None],
        jnp.transpose(mkt.reshape(bp, MEM_HEADS, HEAD_DIM, N_MEM), (0, 3, 1, 2))[None],
        jnp.transpose(mvt.reshape(bp, MEM_HEADS, HEAD_DIM, N_MEM), (0, 3, 1, 2))[None],
        ks.reshape(1, bs, ts, *hd),
        vs.reshape(1, bs, ts, *hd),
        conv_tail_s[None],
    )
```
